```python
import jax, jax.numpy as jnp
from jax import lax
import numpy as np

D_MODEL = 2048
BATCH = 1
SEQ = 8192
DEPTH = 1

HEAD_DIM = 128
N_HEADS_MOBA = D_MODEL // (2 * HEAD_DIM)
N_HEADS_FOX = D_MODEL // (2 * HEAD_DIM)
MOBA_WIDTH = N_HEADS_MOBA * HEAD_DIM
FOX_WIDTH = N_HEADS_FOX * HEAD_DIM
MOBA_BLOCK = 256
MOBA_TOPK = 3
MOBA_Q_CHUNK = 64
FOX_Q_BLOCK = 128
D_FF = 4 * D_MODEL
RMS_EPS = 1e-6
IN_WIDTHS = (MOBA_WIDTH, MOBA_WIDTH, MOBA_WIDTH,
             FOX_WIDTH, FOX_WIDTH, FOX_WIDTH,
             N_HEADS_FOX,
             D_MODEL, D_MODEL)
D_IN = sum(IN_WIDTHS)

kernel_name = "moba_fox_gated_hybrid_layer"


def rms_norm(x, g):
    xf = x.astype(jnp.float32)
    y = xf * lax.rsqrt(jnp.mean(xf * xf, axis=-1, keepdims=True) + RMS_EPS)
    return (y * g.astype(jnp.float32)).astype(x.dtype)


def split_heads(t, n_heads):
    b, s, _ = t.shape
    return t.reshape(b, s, n_heads, HEAD_DIM).transpose(0, 2, 1, 3)


def merge_heads(t):
    b, h, s, d = t.shape
    return t.transpose(0, 2, 1, 3).reshape(b, s, h * d)


def to_chunks(a, c):
    a = a.reshape(a.shape[:2] + (a.shape[2] // c, c) + a.shape[3:])
    return jnp.moveaxis(a, 2, 0)


def from_chunks(a):
    a = jnp.moveaxis(a, 0, 2)
    return a.reshape(a.shape[:2] + (a.shape[2] * a.shape[3],) + a.shape[4:])


def alibi_slopes(n_heads):
    return jnp.exp2(-8.0 * jnp.arange(1, n_heads + 1, dtype=jnp.float32) / n_heads)


def moba_attention(q, k, v):
    b, h, s, d = q.shape
    s_pad = -(-s // MOBA_BLOCK) * MOBA_BLOCK
    pad = ((0, 0), (0, 0), (0, s_pad - s), (0, 0))
    q, k, v = jnp.pad(q, pad), jnp.pad(k, pad), jnp.pad(v, pad)
    nb = s_pad // MOBA_BLOCK
    n_sel = min(MOBA_TOPK, nb)
    scale = HEAD_DIM ** -0.5
    slopes = alibi_slopes(h)[None, :, None]
    kb = k.reshape(b, h, nb, MOBA_BLOCK, d)
    vb = v.reshape(b, h, nb, MOBA_BLOCK, d)
    kmean = jnp.mean(kb.astype(jnp.float32), axis=3)
    gate = jnp.einsum('bhtd,bhnd->bhtn', q.astype(jnp.float32), kmean)
    q_blk = jnp.arange(s_pad) // MOBA_BLOCK
    past = jnp.arange(nb)[None, :] < q_blk[:, None]
    gate = jnp.where(past, gate, -jnp.inf)
    top_vals, sel = lax.top_k(gate, n_sel)
    valid = jnp.isfinite(top_vals)
    bi = jnp.arange(b)[:, None, None, None]
    hi = jnp.arange(h)[None, :, None, None]
    blk_off = jnp.arange(MOBA_BLOCK)

    def chunk(args):
        ci, qc, selc, validc = args
        q_pos = ci * MOBA_Q_CHUNK + jnp.arange(MOBA_Q_CHUNK)
        own = (ci * MOBA_Q_CHUNK) // MOBA_BLOCK
        k_own = lax.dynamic_index_in_dim(kb, own, axis=2, keepdims=False)
        v_own = lax.dynamic_index_in_dim(vb, own, axis=2, keepdims=False)
        dist_own = (q_pos[:, None] - (own * MOBA_BLOCK + blk_off)[None, :])
        s_own = jnp.einsum('bhcd,bhsd->bhcs', qc, k_own).astype(jnp.float32) * scale
        s_own = s_own - slopes[..., None] * jnp.abs(dist_own).astype(jnp.float32)
        s_own = jnp.where(dist_own >= 0, s_own, -jnp.inf)
        k_sel = kb[bi, hi, selc]
        v_sel = vb[bi, hi, selc]
        s_sel = jnp.einsum('bhcd,bhcjsd->bhcjs', qc, k_sel).astype(jnp.float32) * scale
        sel_pos = selc[..., None] * MOBA_BLOCK + blk_off
        dist_sel = jnp.abs(q_pos[None, None, :, None, None] - sel_pos).astype(jnp.float32)
        s_sel = s_sel - slopes[..., None, None] * dist_sel
        s_sel = jnp.where(validc[..., None], s_sel, -jnp.inf)
        c, j = selc.shape[2], selc.shape[3]
        scores = jnp.concatenate([s_sel.reshape(b, h, c, j * MOBA_BLOCK), s_own], axis=-1)
        p = jax.nn.softmax(scores, axis=-1).astype(v.dtype)
        p_sel = p[..., :j * MOBA_BLOCK].reshape(b, h, c, j, MOBA_BLOCK)
        p_own = p[..., j * MOBA_BLOCK:]
        return (jnp.einsum('bhcjs,bhcjsd->bhcd', p_sel, v_sel)
                + jnp.einsum('bhcs,bhsd->bhcd', p_own, v_own))

    nc = s_pad // MOBA_Q_CHUNK
    out = lax.map(chunk, (jnp.arange(nc), to_chunks(q, MOBA_Q_CHUNK),
                          to_chunks(sel, MOBA_Q_CHUNK), to_chunks(valid, MOBA_Q_CHUNK)))
    return from_chunks(out)[:, :, :s]


def forgetting_attention(q, k, v, f_logit, b_forget):
    b, h, s, d = q.shape
    scale = HEAD_DIM ** -0.5
    log_f = jax.nn.log_sigmoid(f_logit.astype(jnp.float32) + b_forget.astype(jnp.float32))
    c = jnp.cumsum(log_f, axis=1).transpose(0, 2, 1)
    key_pos = jnp.arange(s)

    def chunk(args):
        ci, qc, cq = args
        sc = jnp.einsum('bhcd,bhsd->bhcs', qc, k).astype(jnp.float32) * scale
        sc = sc + cq[..., None] - c[:, :, None, :]
        q_pos = ci * FOX_Q_BLOCK + jnp.arange(FOX_Q_BLOCK)
        sc = jnp.where(q_pos[:, None] >= key_pos[None, :], sc, -jnp.inf)
        p = jax.nn.softmax(sc, axis=-1).astype(v.dtype)
        return jnp.einsum('bhcs,bhsd->bhcd', p, v)

    nc = s // FOX_Q_BLOCK
    cq = c.reshape(b, h, nc, FOX_Q_BLOCK).transpose(2, 0, 1, 3)
    out = lax.map(chunk, (jnp.arange(nc), to_chunks(q, FOX_Q_BLOCK), cq))
    return from_chunks(out)


def setup_inputs(seed: int = 0) -> dict:
    key = jax.random.key(seed)
    ks = jax.random.split(key, 11)
    f32 = jnp.float32
    def normal(k, shape, scale):
        return jax.random.normal(k, shape, f32) * scale
    return {
        "x": normal(ks[0], (BATCH, SEQ, D_MODEL), 1.0),
        "norm_mix_g": 1.0 + normal(ks[1], (D_MODEL,), 0.02),
        "w_in": normal(ks[2], (D_MODEL, D_IN), D_MODEL ** -0.5),
        "b_forget": 2.0 + normal(ks[3], (N_HEADS_FOX,), 0.5),
        "w_proj_a": normal(ks[4], (MOBA_WIDTH, D_MODEL), MOBA_WIDTH ** -0.5),
        "w_proj_b": normal(ks[5], (FOX_WIDTH, D_MODEL), FOX_WIDTH ** -0.5),
        "w_out": normal(ks[6], (D_MODEL, D_MODEL), D_MODEL ** -0.5),
        "norm_mlp_g": 1.0 + normal(ks[7], (D_MODEL,), 0.02),
        "w_up": normal(ks[8], (D_MODEL, D_FF), D_MODEL ** -0.5),
        "w_down": normal(ks[9], (D_FF, D_MODEL), D_FF ** -0.5),
        "norm_final_g": 1.0 + normal(ks[10], (D_MODEL,), 0.02),
    }


def reference(x, norm_mix_g, w_in, b_forget, w_proj_a, w_proj_b, w_out,
              norm_mlp_g, w_up, w_down, norm_final_g):
    split_idx = list(np.cumsum(IN_WIDTHS)[:-1])
    for _ in range(DEPTH):
        hn = rms_norm(x, norm_mix_g)
        z = hn @ w_in
        q_a, k_a, v_a, q_b, k_b, v_b, f_logit, g_a, g_b = jnp.split(z, split_idx, axis=-1)
        a = moba_attention(split_heads(q_a, N_HEADS_MOBA), split_heads(k_a, N_HEADS_MOBA),
                           split_heads(v_a, N_HEADS_MOBA))
        bfx = forgetting_attention(split_heads(q_b, N_HEADS_FOX), split_heads(k_b, N_HEADS_FOX),
                                   split_heads(v_b, N_HEADS_FOX), f_logit, b_forget)
        a = merge_heads(a) @ w_proj_a
        bfx = merge_heads(bfx) @ w_proj_b
        merged = jax.nn.sigmoid(g_a) * a + jax.nn.sigmoid(g_b) * bfx
        x = x + merged @ w_out
        hm = rms_norm(x, norm_mlp_g)
        x = x + jnp.square(jax.nn.relu(hm @ w_up)) @ w_down
    return rms_norm(x, norm_final_g)
```

```python
import functools
import math

import jax
import jax.numpy as jnp
from jax import lax
from jax.experimental import pallas as pl
from jax.experimental.pallas import tpu as pltpu

D_MODEL = 2048
SEQ = 8192
HEAD_DIM = 128
N_HEADS = 8
WIDTH = N_HEADS * HEAD_DIM
MOBA_BLOCK = 256
MOBA_TOPK = 3
N_BLOCKS = SEQ // MOBA_BLOCK
D_FF = 4 * D_MODEL
RMS_EPS = 1e-6
LOG2E = math.log2(math.e)
SCALE2 = HEAD_DIM ** -0.5 * LOG2E
N_MAIN = 6 * WIDTH + 2 * D_MODEL
F_PAD = 128

BF16 = jnp.bfloat16
F32 = jnp.float32
NEG_INF = float("-inf")
POS_INF = float("inf")

VMEM_LIMIT = 56 * 1024 * 1024

IN_TM, IN_TN = 1024, 1024
ATT_TQ = MOBA_BLOCK
ATT_TK = MOBA_BLOCK
MIX_TM = 512
MLP_TM, MLP_FC = 512, 512
ROW_CHUNK = 128


def _nt_dot(a, b):
    return lax.dot_general(a, b, (((1,), (1,)), ((), ())), preferred_element_type=F32)


def _rms_rows(x, g):
    ms = jnp.mean(x * x, axis=-1, keepdims=True)
    return x * lax.rsqrt(ms + RMS_EPS) * g


def _inproj_kernel(x_ref, g_ref, w_ref, wf_ref, z_ref, f_ref, hn_ref):
    j = pl.program_id(1)

    @pl.when(j == 0)
    def _():
        def body(r, _):
            rows = pl.ds(pl.multiple_of(r * ROW_CHUNK, ROW_CHUNK), ROW_CHUNK)
            hn = _rms_rows(x_ref[rows, :], g_ref[...]).astype(BF16)
            hn_ref[rows, :] = hn
            f_ref[rows, :] = jnp.dot(hn, wf_ref[...], preferred_element_type=F32)
            return 0
        lax.fori_loop(0, IN_TM // ROW_CHUNK, body, 0)

    z = jnp.dot(hn_ref[...], w_ref[...], preferred_element_type=F32).astype(BF16)
    for h in range(IN_TN // HEAD_DIM):
        z_ref[h] = z[:, h * HEAD_DIM:(h + 1) * HEAD_DIM]


def _inproj(x2, g, w_main, w_f):
    n_i, n_j = SEQ // IN_TM, N_MAIN // IN_TN
    slabs = IN_TN // HEAD_DIM
    return pl.pallas_call(
        _inproj_kernel,
        grid=(n_i, n_j),
        in_specs=[
            pl.BlockSpec((IN_TM, D_MODEL), lambda i, j: (i, 0)),
            pl.BlockSpec((1, D_MODEL), lambda i, j: (0, 0)),
            pl.BlockSpec((D_MODEL, IN_TN), lambda i, j: (0, j)),
            pl.BlockSpec((D_MODEL, F_PAD), lambda i, j: (0, 0)),
        ],
        out_specs=[
            pl.BlockSpec((None, slabs, IN_TM, HEAD_DIM), lambda i, j: (j, 0, i, 0)),
            pl.BlockSpec((IN_TM, F_PAD), lambda i, j: (i, 0)),
        ],
        out_shape=[
            jax.ShapeDtypeStruct((n_j, slabs, SEQ, HEAD_DIM), BF16),
            jax.ShapeDtypeStruct((SEQ, F_PAD), F32),
        ],
        scratch_shapes=[pltpu.VMEM((IN_TM, D_MODEL), BF16)],
        compiler_params=pltpu.CompilerParams(
            dimension_semantics=("arbitrary", "arbitrary"), vmem_limit_bytes=VMEM_LIMIT),
        name="inproj",
    )(x2, g, w_main, w_f)


def _fox_prep_kernel(f_ref, b_ref, c_ref):
    n = ROW_CHUNK
    tri = (lax.broadcasted_iota(jnp.int32, (n, n), 0)
           >= lax.broadcasted_iota(jnp.int32, (n, n), 1)).astype(BF16)

    def body(ci, carry):
        rows = pl.ds(pl.multiple_of(ci * n, n), n)
        lf = jax.nn.log_sigmoid(f_ref[rows, :] + b_ref[...])
        hi = lf.astype(BF16)
        r1 = lf - hi.astype(F32)
        mid = r1.astype(BF16)
        lo = (r1 - mid.astype(F32)).astype(BF16)
        cs = (jnp.dot(tri, hi, preferred_element_type=F32)
              + jnp.dot(tri, mid, preferred_element_type=F32)
              + jnp.dot(tri, lo, preferred_element_type=F32)) + carry
        c_ref[rows, :] = cs * LOG2E
        return cs[n - 1:n, :]

    lax.fori_loop(0, SEQ // n, body, jnp.zeros((1, F_PAD), F32))


def _fox_prep(f, b_pad):
    return pl.pallas_call(
        _fox_prep_kernel,
        out_shape=jax.ShapeDtypeStruct((SEQ, F_PAD), F32),
        compiler_params=pltpu.CompilerParams(vmem_limit_bytes=VMEM_LIMIT),
        name="fox_prep",
    )(f, b_pad)


def _tile_iotas():
    ko = lax.broadcasted_iota(jnp.int32, (ATT_TK, ATT_TQ), 0)
    qo = lax.broadcasted_iota(jnp.int32, (ATT_TK, ATT_TQ), 1)
    return ko, qo


def _finish(acc_ref, l_ref, o_ref):
    out = acc_ref[...] / l_ref[...]
    o_ref[...] = out.T.astype(o_ref.dtype)


def _moba_kernel(slope_ref, q_ref, k_ref, vt_ref, o_ref,
                 kmean_ref, nb_ref, selb_ref, m_ref, l_ref, acc_ref):
    h = pl.program_id(0)
    i = pl.program_id(1)
    slope2 = slope_ref[h] * LOG2E
    ko, qo = _tile_iotas()

    @pl.when(i == 0)
    def _():
        def body(b, _):
            rows = pl.ds(pl.multiple_of(b * MOBA_BLOCK, MOBA_BLOCK), MOBA_BLOCK)
            kb = k_ref[rows, :].astype(F32)
            kmean_ref[pl.ds(b, 1), :] = jnp.mean(kb, axis=0, keepdims=True)
            return 0
        lax.fori_loop(0, N_BLOCKS, body, 0)
        nb_ref[...] = (qo - ko).astype(F32) * (-slope2)

    q = q_ref[...]

    km = kmean_ref[...]
    km_hi = km.astype(BF16)
    km_lo = (km - km_hi.astype(F32)).astype(BF16)
    gate = _nt_dot(km_hi, q) + _nt_dot(km_lo, q)
    jj = lax.broadcasted_iota(jnp.int32, (N_BLOCKS, ATT_TQ), 0)
    past = jj < i
    gm = jnp.where(past, gate, NEG_INF)
    cnt = jnp.zeros((N_BLOCKS, ATT_TQ), F32)
    for jp in range(N_BLOCKS):
        row = gm[jp:jp + 1, :]
        tie = jnp.where(jj > jp, 1.0, 0.0)
        cnt = cnt + jnp.where(row > gm, 1.0, jnp.where(row == gm, tie, 0.0))
    chosen = jnp.where(past, cnt, float(MOBA_TOPK)) < float(MOBA_TOPK)
    selb_ref[...] = jnp.where(chosen, 0.0, NEG_INF)

    own = pl.ds(pl.multiple_of(i * MOBA_BLOCK, MOBA_BLOCK), MOBA_BLOCK)
    s = _nt_dot(k_ref[own, :], q) * SCALE2 + nb_ref[...]
    s = jnp.where(ko <= qo, s, NEG_INF)
    m0 = jnp.max(s, axis=0, keepdims=True)
    p = jnp.exp2(s - m0)
    m_ref[...] = m0
    l_ref[...] = jnp.sum(p, axis=0, keepdims=True)
    acc_ref[...] = jnp.dot(vt_ref[i], p.astype(BF16), preferred_element_type=F32)

    def body(j, _):
        rows = pl.ds(pl.multiple_of(j * MOBA_BLOCK, MOBA_BLOCK), MOBA_BLOCK)
        base = _nt_dot(k_ref[rows, :], q) * SCALE2 + nb_ref[...]
        cj = -slope2 * (MOBA_BLOCK * (i - j)).astype(F32)
        selrow = selb_ref[pl.ds(j, 1), :]
        bm = jnp.max(base, axis=0, keepdims=True) + cj + selrow
        m_old = m_ref[...]
        m_new = jnp.maximum(m_old, bm)
        t = jnp.where(selrow == 0.0, m_new - cj, POS_INF)
        p = jnp.exp2(base - t)
        alpha = jnp.exp2(m_old - m_new)
        l_ref[...] = alpha * l_ref[...] + jnp.sum(p, axis=0, keepdims=True)
        acc_ref[...] = alpha * acc_ref[...] + jnp.dot(
            vt_ref[j], p.astype(BF16), preferred_element_type=F32)
        m_ref[...] = m_new
        return 0
    lax.fori_loop(0, i, body, 0)

    _finish(acc_ref, l_ref, o_ref)


def _moba(slopes, z4, vt):
    return pl.pallas_call(
        _moba_kernel,
        grid=(N_HEADS, SEQ // ATT_TQ),
        in_specs=[
            pl.BlockSpec(memory_space=pltpu.SMEM),
            pl.BlockSpec((None, None, ATT_TQ, HEAD_DIM), lambda h, i: (0, h, i, 0)),
            pl.BlockSpec((None, None, SEQ, HEAD_DIM), lambda h, i: (1, h, 0, 0)),
            pl.BlockSpec((None, N_BLOCKS, HEAD_DIM, MOBA_BLOCK), lambda h, i: (h, 0, 0, 0)),
        ],
        out_specs=pl.BlockSpec((None, ATT_TQ, HEAD_DIM), lambda h, i: (h, i, 0)),
        out_shape=jax.ShapeDtypeStruct((N_HEADS, SEQ, HEAD_DIM), BF16),
        scratch_shapes=[
            pltpu.VMEM((N_BLOCKS, HEAD_DIM), F32),
            pltpu.VMEM((ATT_TK, ATT_TQ), F32),
            pltpu.VMEM((N_BLOCKS, ATT_TQ), F32),
            pltpu.VMEM((1, ATT_TQ), F32),
            pltpu.VMEM((1, ATT_TQ), F32),
            pltpu.VMEM((HEAD_DIM, ATT_TQ), F32),
        ],
        compiler_params=pltpu.CompilerParams(
            dimension_semantics=("arbitrary", "arbitrary"), vmem_limit_bytes=VMEM_LIMIT),
        name="moba",
    )(slopes, z4, z4, vt)


def _fox_kernel(q_ref, k_ref, vt_ref, ccol_ref, crow_ref, o_ref, m_ref, l_ref, acc_ref):
    i = pl.program_id(1)
    ko, qo = _tile_iotas()
    q = q_ref[...]
    cq = crow_ref[...]

    def scores(j):
        rows = pl.ds(pl.multiple_of(j * ATT_TK, ATT_TK), ATT_TK)
        return _nt_dot(k_ref[rows, :], q) * SCALE2 - ccol_ref[rows, :]

    s = jnp.where(ko <= qo, scores(i), NEG_INF)
    m0 = jnp.max(s, axis=0, keepdims=True) + cq
    p = jnp.exp2(s - (m0 - cq))
    m_ref[...] = m0
    l_ref[...] = jnp.sum(p, axis=0, keepdims=True)
    acc_ref[...] = jnp.dot(vt_ref[i], p.astype(BF16), preferred_element_type=F32)

    def body(j, _):
        base = scores(j)
        m_old = m_ref[...]
        m_new = jnp.maximum(m_old, jnp.max(base, axis=0, keepdims=True) + cq)
        p = jnp.exp2(base - (m_new - cq))
        alpha = jnp.exp2(m_old - m_new)
        l_ref[...] = alpha * l_ref[...] + jnp.sum(p, axis=0, keepdims=True)
        acc_ref[...] = alpha * acc_ref[...] + jnp.dot(
            vt_ref[j], p.astype(BF16), preferred_element_type=F32)
        m_ref[...] = m_new
        return 0
    lax.fori_loop(0, i, body, 0)

    _finish(acc_ref, l_ref, o_ref)


def _fox(z4, vt, c_col, c_row):
    return pl.pallas_call(
        _fox_kernel,
        grid=(N_HEADS, SEQ // ATT_TQ),
        in_specs=[
            pl.BlockSpec((None, None, ATT_TQ, HEAD_DIM), lambda h, i: (3, h, i, 0)),
            pl.BlockSpec((None, None, SEQ, HEAD_DIM), lambda h, i: (4, h, 0, 0)),
            pl.BlockSpec((None, SEQ // ATT_TK, HEAD_DIM, ATT_TK), lambda h, i: (h, 0, 0, 0)),
            pl.BlockSpec((None, SEQ, 1), lambda h, i: (h, 0, 0)),
            pl.BlockSpec((None, 1, ATT_TQ), lambda h, i: (h, 0, i)),
        ],
        out_specs=pl.BlockSpec((None, ATT_TQ, HEAD_DIM), lambda h, i: (h, i, 0)),
        out_shape=jax.ShapeDtypeStruct((N_HEADS, SEQ, HEAD_DIM), BF16),
        scratch_shapes=[
            pltpu.VMEM((1, ATT_TQ), F32),
            pltpu.VMEM((1, ATT_TQ), F32),
            pltpu.VMEM((HEAD_DIM, ATT_TQ), F32),
        ],
        compiler_params=pltpu.CompilerParams(
            dimension_semantics=("arbitrary", "arbitrary"), vmem_limit_bytes=VMEM_LIMIT),
        name="fox",
    )(z4, z4, vt, c_col, c_row)


def _mix_kernel(a_ref, b_ref, ga_ref, gb_ref, x_ref, wpa_ref, wpb_ref, wo_ref, g_ref,
                x1_ref, hm_ref):
    def heads(ref):
        return jnp.concatenate([ref[h] for h in range(N_HEADS)], axis=1)

    def gate(ref):
        cols = [ref[t, h] for t in range(D_MODEL // WIDTH) for h in range(N_HEADS)]
        return jax.nn.sigmoid(jnp.concatenate(cols, axis=1).astype(F32))

    pa = jnp.dot(heads(a_ref), wpa_ref[...], preferred_element_type=F32)
    pb = jnp.dot(heads(b_ref), wpb_ref[...], preferred_element_type=F32)
    merged = gate(ga_ref) * pa + gate(gb_ref) * pb
    x1 = x_ref[...] + jnp.dot(merged.astype(BF16), wo_ref[...], preferred_element_type=F32)
    x1_ref[...] = x1
    hm_ref[...] = _rms_rows(x1, g_ref[...]).astype(BF16)


def _mix(a, b, z4, x2, wpa, wpb, wo, g):
    tm = MIX_TM
    gate_tiles = D_MODEL // WIDTH
    const = lambda i: (0, 0)
    return pl.pallas_call(
        _mix_kernel,
        grid=(SEQ // tm,),
        in_specs=[
            pl.BlockSpec((N_HEADS, tm, HEAD_DIM), lambda i: (0, i, 0)),
            pl.BlockSpec((N_HEADS, tm, HEAD_DIM), lambda i: (0, i, 0)),
            pl.BlockSpec((gate_tiles, N_HEADS, tm, HEAD_DIM), lambda i: (3, 0, i, 0)),
            pl.BlockSpec((gate_tiles, N_HEADS, tm, HEAD_DIM), lambda i: (4, 0, i, 0)),
            pl.BlockSpec((tm, D_MODEL), lambda i: (i, 0)),
            pl.BlockSpec((WIDTH, D_MODEL), const),
            pl.BlockSpec((WIDTH, D_MODEL), const),
            pl.BlockSpec((D_MODEL, D_MODEL), const),
            pl.BlockSpec((1, D_MODEL), const),
        ],
        out_specs=[
            pl.BlockSpec((tm, D_MODEL), lambda i: (i, 0)),
            pl.BlockSpec((tm, D_MODEL), lambda i: (i, 0)),
        ],
        out_shape=[
            jax.ShapeDtypeStruct((SEQ, D_MODEL), F32),
            jax.ShapeDtypeStruct((SEQ, D_MODEL), BF16),
        ],
        compiler_params=pltpu.CompilerParams(
            dimension_semantics=("arbitrary",), vmem_limit_bytes=VMEM_LIMIT),
        name="mix",
    )(a, b, z4, z4, x2, wpa, wpb, wo, g)


def _mlp_kernel(hm_ref, x1_ref, wu_ref, wd_ref, g_ref, o_ref, acc_ref):
    c = pl.program_id(1)

    @pl.when(c == 0)
    def _():
        acc_ref[...] = x1_ref[...]

    hcol = jnp.dot(hm_ref[...], wu_ref[...], preferred_element_type=F32)
    hcol = jnp.square(jnp.maximum(hcol, 0.0)).astype(BF16)
    acc_ref[...] += jnp.dot(hcol, wd_ref[...], preferred_element_type=F32)

    @pl.when(c == pl.num_programs(1) - 1)
    def _():
        o_ref[...] = _rms_rows(acc_ref[...], g_ref[...])


def _mlp(hm, x1, wu, wd, g):
    tm, fc = MLP_TM, MLP_FC
    return pl.pallas_call(
        _mlp_kernel,
        grid=(SEQ // tm, D_FF // fc),
        in_specs=[
            pl.BlockSpec((tm, D_MODEL), lambda i, c: (i, 0)),
            pl.BlockSpec((tm, D_MODEL), lambda i, c: (i, 0)),
            pl.BlockSpec((D_MODEL, fc), lambda i, c: (0, c)),
            pl.BlockSpec((fc, D_MODEL), lambda i, c: (c, 0)),
            pl.BlockSpec((1, D_MODEL), lambda i, c: (0, 0)),
        ],
        out_specs=pl.BlockSpec((tm, D_MODEL), lambda i, c: (i, 0)),
        out_shape=jax.ShapeDtypeStruct((SEQ, D_MODEL), F32),
        scratch_shapes=[pltpu.VMEM((tm, D_MODEL), F32)],
        compiler_params=pltpu.CompilerParams(
            dimension_semantics=("arbitrary", "arbitrary"), vmem_limit_bytes=VMEM_LIMIT),
        name="mlp",
    )(hm, x1, wu, wd, g)


def _block_transposed(v):
    return v.reshape(N_HEADS, SEQ // ATT_TK, ATT_TK, HEAD_DIM).transpose(0, 1, 3, 2)


def kernel(x, norm_mix_g, w_in, b_forget, w_proj_a, w_proj_b, w_out, norm_mlp_g, w_up, w_down,
           norm_final_g):
    f_lo = 6 * WIDTH
    f_hi = f_lo + N_HEADS
    w_main = jnp.concatenate([w_in[:, :f_lo], w_in[:, f_hi:]], axis=1).astype(BF16)
    w_f = jnp.pad(w_in[:, f_lo:f_hi], ((0, 0), (0, F_PAD - N_HEADS))).astype(BF16)
    row = lambda g: g.reshape(1, D_MODEL).astype(F32)

    x2 = x.reshape(SEQ, D_MODEL)
    z4, f = _inproj(x2, row(norm_mix_g), w_main, w_f)

    b_pad = jnp.pad(b_forget.astype(F32), (0, F_PAD - N_HEADS)).reshape(1, F_PAD)
    c2 = _fox_prep(f, b_pad)[:, :N_HEADS]
    c_col = c2.T.reshape(N_HEADS, SEQ, 1)
    c_row = c2.T.reshape(N_HEADS, 1, SEQ)

    slopes = jnp.exp2(-8.0 * jnp.arange(1, N_HEADS + 1, dtype=F32) / N_HEADS)
    a = _moba(slopes, z4, _block_transposed(z4[2]))
    b = _fox(z4, _block_transposed(z4[5]), c_col, c_row)

    x1, hm = _mix(a, b, z4, x2, w_proj_a.astype(BF16), w_proj_b.astype(BF16),
                  w_out.astype(BF16), row(norm_mlp_g))
    out = _mlp(hm, x1, w_up.astype(BF16), w_down.astype(BF16), row(norm_final_g))
    return out.reshape(x.shape)
```

```python
import math

import jax
import jax.numpy as jnp
from jax import lax
from jax.experimental import pallas as pl
from jax.experimental.pallas import tpu as pltpu

D_MODEL = 2048
SEQ = 8192
HEAD_DIM = 128
N_HEADS = 8
WIDTH = N_HEADS * HEAD_DIM
MOBA_BLOCK = 256
MOBA_TOPK = 3
N_BLOCKS = SEQ // MOBA_BLOCK
D_FF = 4 * D_MODEL
RMS_EPS = 1e-6
LOG2E = math.log2(math.e)
SCALE2 = HEAD_DIM ** -0.5 * LOG2E
N_MAIN = 6 * WIDTH + 2 * D_MODEL
F_PAD = 128

BF16 = jnp.bfloat16
F32 = jnp.float32
NEG_INF = float("-inf")
POS_INF = float("inf")

VMEM_LIMIT = 56 * 1024 * 1024

IN_TM, IN_TN = 1024, 1024
ATT_T = 2 * MOBA_BLOCK
ATT_G = 2
N_TILES = SEQ // ATT_T
MIX_TM = 512
MLP_TM, MLP_FC = 512, 512
ROW_CHUNK = 128


def _nt_dot(a, b):
    return lax.dot_general(a, b, (((1,), (1,)), ((), ())), preferred_element_type=F32)


def _rms_rows(x, g):
    ms = jnp.mean(x * x, axis=-1, keepdims=True)
    return x * lax.rsqrt(ms + RMS_EPS) * g


def _inproj_kernel(x_ref, g_ref, w_ref, wf_ref, z_ref, f_ref, hn_ref):
    j = pl.program_id(1)

    @pl.when(j == 0)
    def _():
        def body(r, _):
            rows = pl.ds(pl.multiple_of(r * ROW_CHUNK, ROW_CHUNK), ROW_CHUNK)
            hn = _rms_rows(x_ref[rows, :], g_ref[...]).astype(BF16)
            hn_ref[rows, :] = hn
            f_ref[rows, :] = jnp.dot(hn, wf_ref[...], preferred_element_type=F32)
            return 0
        lax.fori_loop(0, IN_TM // ROW_CHUNK, body, 0)

    z = jnp.dot(hn_ref[...], w_ref[...], preferred_element_type=F32).astype(BF16)
    for h in range(IN_TN // HEAD_DIM):
        z_ref[h] = z[:, h * HEAD_DIM:(h + 1) * HEAD_DIM]


def _inproj(x2, g, w_main, w_f):
    n_i, n_j = SEQ // IN_TM, N_MAIN // IN_TN
    slabs = IN_TN // HEAD_DIM
    return pl.pallas_call(
        _inproj_kernel,
        grid=(n_i, n_j),
        in_specs=[
            pl.BlockSpec((IN_TM, D_MODEL), lambda i, j: (i, 0)),
            pl.BlockSpec((1, D_MODEL), lambda i, j: (0, 0)),
            pl.BlockSpec((D_MODEL, IN_TN), lambda i, j: (0, j)),
            pl.BlockSpec((D_MODEL, F_PAD), lambda i, j: (0, 0)),
        ],
        out_specs=[
            pl.BlockSpec((None, slabs, IN_TM, HEAD_DIM), lambda i, j: (j, 0, i, 0)),
            pl.BlockSpec((IN_TM, F_PAD), lambda i, j: (i, 0)),
        ],
        out_shape=[
            jax.ShapeDtypeStruct((n_j, slabs, SEQ, HEAD_DIM), BF16),
            jax.ShapeDtypeStruct((SEQ, F_PAD), F32),
        ],
        scratch_shapes=[pltpu.VMEM((IN_TM, D_MODEL), BF16)],
        compiler_params=pltpu.CompilerParams(
            dimension_semantics=("arbitrary", "arbitrary"), vmem_limit_bytes=VMEM_LIMIT),
        name="inproj",
    )(x2, g, w_main, w_f)


def _fox_prep_kernel(f_ref, b_ref, c_ref):
    n = ROW_CHUNK
    tri = (lax.broadcasted_iota(jnp.int32, (n, n), 0)
           >= lax.broadcasted_iota(jnp.int32, (n, n), 1)).astype(BF16)

    def body(ci, carry):
        rows = pl.ds(pl.multiple_of(ci * n, n), n)
        lf = jax.nn.log_sigmoid(f_ref[rows, :] + b_ref[...])
        hi = lf.astype(BF16)
        r1 = lf - hi.astype(F32)
        mid = r1.astype(BF16)
        lo = (r1 - mid.astype(F32)).astype(BF16)
        cs = (jnp.dot(tri, hi, preferred_element_type=F32)
              + jnp.dot(tri, mid, preferred_element_type=F32)
              + jnp.dot(tri, lo, preferred_element_type=F32)) + carry
        c_ref[rows, :] = cs * LOG2E
        return cs[n - 1:n, :]

    lax.fori_loop(0, SEQ // n, body, jnp.zeros((1, F_PAD), F32))


def _fox_prep(f, b_pad):
    return pl.pallas_call(
        _fox_prep_kernel,
        out_shape=jax.ShapeDtypeStruct((SEQ, F_PAD), F32),
        compiler_params=pltpu.CompilerParams(vmem_limit_bytes=VMEM_LIMIT),
        name="fox_prep",
    )(f, b_pad)


def _tile_iotas():
    ko = lax.broadcasted_iota(jnp.int32, (ATT_T, ATT_T), 0)
    qo = lax.broadcasted_iota(jnp.int32, (ATT_T, ATT_T), 1)
    return ko, qo


def _tile_rows(j):
    return pl.ds(pl.multiple_of(j * ATT_T, ATT_T), ATT_T)


def _start(g, s, shift, vt_blk, m_ref, l_ref, acc_ref):
    m0 = jnp.max(s, axis=0, keepdims=True)
    p = jnp.exp2(s - m0)
    m_ref[g] = m0 + shift
    l_ref[g] = jnp.sum(p, axis=0, keepdims=True)
    acc_ref[g] = jnp.dot(vt_blk, p.astype(BF16), preferred_element_type=F32)


def _accumulate(g, p, m_new, vt_blk, m_ref, l_ref, acc_ref):
    alpha = jnp.exp2(m_ref[g] - m_new)
    l_ref[g] = alpha * l_ref[g] + jnp.sum(p, axis=0, keepdims=True)
    acc_ref[g] = alpha * acc_ref[g] + jnp.dot(vt_blk, p.astype(BF16), preferred_element_type=F32)
    m_ref[g] = m_new


def _finish(g, acc_ref, l_ref, o_ref):
    out = acc_ref[g] / l_ref[g]
    o_ref[g] = out.T.astype(o_ref.dtype)


def _moba_kernel(slope_ref, q_ref, k_ref, vt_ref, o_ref,
                 kmean_ref, nb_ref, selb_ref, m_ref, l_ref, acc_ref):
    hg = pl.program_id(0)
    i = pl.program_id(1)
    ko, qo = _tile_iotas()
    half = MOBA_BLOCK
    slope2 = [slope_ref[hg * ATT_G + g] * LOG2E for g in range(ATT_G)]

    @pl.when(i == 0)
    def _():
        for g in range(ATT_G):
            def body(b, _):
                rows = pl.ds(pl.multiple_of(b * MOBA_BLOCK, MOBA_BLOCK), MOBA_BLOCK)
                kb = k_ref[g, rows, :].astype(F32)
                kmean_ref[g, pl.ds(b, 1), :] = jnp.mean(kb, axis=0, keepdims=True)
                return 0
            lax.fori_loop(0, N_BLOCKS, body, 0)
            nb_ref[g] = (qo - ko).astype(F32) * (-slope2[g])

    jj = lax.broadcasted_iota(jnp.int32, (N_BLOCKS, ATT_T), 0)
    lane = lax.broadcasted_iota(jnp.int32, (N_BLOCKS, ATT_T), 1)
    own = 2 * i + jnp.where(lane >= half, 1, 0)
    past = jj < own
    for g in range(ATT_G):
        q = q_ref[g]
        km = kmean_ref[g]
        km_hi = km.astype(BF16)
        km_lo = (km - km_hi.astype(F32)).astype(BF16)
        gate = _nt_dot(km_hi, q) + _nt_dot(km_lo, q)
        gm = jnp.where(past, gate, NEG_INF)
        cnt = jnp.zeros((N_BLOCKS, ATT_T), F32)
        for jp in range(N_BLOCKS):
            row = gm[jp:jp + 1, :]
            tie = jnp.where(jj > jp, 1.0, 0.0)
            cnt = cnt + jnp.where(row > gm, 1.0, jnp.where(row == gm, tie, 0.0))
        chosen = jnp.where(past, cnt, float(MOBA_TOPK)) < float(MOBA_TOPK)
        selb_ref[g] = jnp.where(chosen, 0.0, NEG_INF)

    for g in range(ATT_G):
        base = _nt_dot(k_ref[g, _tile_rows(i), :], q_ref[g]) * SCALE2 + nb_ref[g]
        sel0 = selb_ref[g, pl.ds(2 * i, 1), :]
        vis0 = jnp.where(qo[:1, :] < half, 0.0, sel0)
        s = base + jnp.where(ko < half, vis0, 0.0)
        s = jnp.where(ko <= qo, s, NEG_INF)
        _start(g, s, jnp.zeros((1, ATT_T), F32), vt_ref[g, i], m_ref, l_ref, acc_ref)

    def body(t, _):
        for g in range(ATT_G):
            base = _nt_dot(k_ref[g, _tile_rows(t), :], q_ref[g]) * SCALE2 + nb_ref[g]
            cj = -slope2[g] * (ATT_T * (i - t)).astype(F32)
            r0 = selb_ref[g, pl.ds(2 * t, 1), :]
            r1 = selb_ref[g, pl.ds(2 * t + 1, 1), :]
            b0, b1 = base[:half], base[half:]
            bm = jnp.maximum(jnp.max(b0, axis=0, keepdims=True) + r0,
                             jnp.max(b1, axis=0, keepdims=True) + r1) + cj
            m_new = jnp.maximum(m_ref[g], bm)
            t0 = jnp.where(r0 == 0.0, m_new - cj, POS_INF)
            t1 = jnp.where(r1 == 0.0, m_new - cj, POS_INF)
            p = jnp.concatenate([jnp.exp2(b0 - t0), jnp.exp2(b1 - t1)], axis=0)
            _accumulate(g, p, m_new, vt_ref[g, t], m_ref, l_ref, acc_ref)
        return 0
    lax.fori_loop(0, i, body, 0)

    for g in range(ATT_G):
        _finish(g, acc_ref, l_ref, o_ref)


def _moba(slopes, z4, vt):
    G, T = ATT_G, ATT_T
    return pl.pallas_call(
        _moba_kernel,
        grid=(N_HEADS // G, N_TILES),
        in_specs=[
            pl.BlockSpec(memory_space=pltpu.SMEM),
            pl.BlockSpec((None, G, T, HEAD_DIM), lambda h, i: (0, h, i, 0)),
            pl.BlockSpec((None, G, SEQ, HEAD_DIM), lambda h, i: (1, h, 0, 0)),
            pl.BlockSpec((G, N_TILES, HEAD_DIM, T), lambda h, i: (h, 0, 0, 0)),
        ],
        out_specs=pl.BlockSpec((G, T, HEAD_DIM), lambda h, i: (h, i, 0)),
        out_shape=jax.ShapeDtypeStruct((N_HEADS, SEQ, HEAD_DIM), BF16),
        scratch_shapes=[
            pltpu.VMEM((G, N_BLOCKS, HEAD_DIM), F32),
            pltpu.VMEM((G, T, T), F32),
            pltpu.VMEM((G, N_BLOCKS, T), F32),
            pltpu.VMEM((G, 1, T), F32),
            pltpu.VMEM((G, 1, T), F32),
            pltpu.VMEM((G, HEAD_DIM, T), F32),
        ],
        compiler_params=pltpu.CompilerParams(
            dimension_semantics=("arbitrary", "arbitrary"), vmem_limit_bytes=VMEM_LIMIT),
        name="moba",
    )(slopes, z4, z4, vt)


def _fox_kernel(q_ref, k_ref, vt_ref, ccol_ref, crow_ref, o_ref, m_ref, l_ref, acc_ref):
    i = pl.program_id(1)
    ko, qo = _tile_iotas()

    def scores(g, j):
        rows = _tile_rows(j)
        return _nt_dot(k_ref[g, rows, :], q_ref[g]) * SCALE2 - ccol_ref[g, rows, :]

    for g in range(ATT_G):
        s = jnp.where(ko <= qo, scores(g, i), NEG_INF)
        _start(g, s, crow_ref[g], vt_ref[g, i], m_ref, l_ref, acc_ref)

    def body(j, _):
        for g in range(ATT_G):
            cq = crow_ref[g]
            base = scores(g, j)
            m_new = jnp.maximum(m_ref[g], jnp.max(base, axis=0, keepdims=True) + cq)
            p = jnp.exp2(base - (m_new - cq))
            _accumulate(g, p, m_new, vt_ref[g, j], m_ref, l_ref, acc_ref)
        return 0
    lax.fori_loop(0, i, body, 0)

    for g in range(ATT_G):
        _finish(g, acc_ref, l_ref, o_ref)


def _fox(z4, vt, c_col, c_row):
    G, T = ATT_G, ATT_T
    return pl.pallas_call(
        _fox_kernel,
        grid=(N_HEADS // G, N_TILES),
        in_specs=[
            pl.BlockSpec((None, G, T, HEAD_DIM), lambda h, i: (3, h, i, 0)),
            pl.BlockSpec((None, G, SEQ, HEAD_DIM), lambda h, i: (4, h, 0, 0)),
            pl.BlockSpec((G, N_TILES, HEAD_DIM, T), lambda h, i: (h, 0, 0, 0)),
            pl.BlockSpec((G, SEQ, 1), lambda h, i: (h, 0, 0)),
            pl.BlockSpec((G, 1, T), lambda h, i: (h, 0, i)),
        ],
        out_specs=pl.BlockSpec((G, T, HEAD_DIM), lambda h, i: (h, i, 0)),
        out_shape=jax.ShapeDtypeStruct((N_HEADS, SEQ, HEAD_DIM), BF16),
        scratch_shapes=[
            pltpu.VMEM((G, 1, T), F32),
            pltpu.VMEM((G, 1, T), F32),
            pltpu.VMEM((G, HEAD_DIM, T), F32),
        ],
        compiler_params=pltpu.CompilerParams(
            dimension_semantics=("arbitrary", "arbitrary"), vmem_limit_bytes=VMEM_LIMIT),
        name="fox",
    )(z4, z4, vt, c_col, c_row)


def _mix_kernel(a_ref, b_ref, ga_ref, gb_ref, x_ref, wpa_ref, wpb_ref, wo_ref, g_ref,
                x1_ref, hm_ref):
    def heads(ref):
        return jnp.concatenate([ref[h] for h in range(N_HEADS)], axis=1)

    def gate(ref):
        cols = [ref[t, h] for t in range(D_MODEL // WIDTH) for h in range(N_HEADS)]
        return jax.nn.sigmoid(jnp.concatenate(cols, axis=1).astype(F32))

    pa = jnp.dot(heads(a_ref), wpa_ref[...], preferred_element_type=F32)
    pb = jnp.dot(heads(b_ref), wpb_ref[...], preferred_element_type=F32)
    merged = gate(ga_ref) * pa + gate(gb_ref) * pb
    x1 = x_ref[...] + jnp.dot(merged.astype(BF16), wo_ref[...], preferred_element_type=F32)
    x1_ref[...] = x1
    hm_ref[...] = _rms_rows(x1, g_ref[...]).astype(BF16)


def _mix(a, b, z4, x2, wpa, wpb, wo, g):
    tm = MIX_TM
    gate_tiles = D_MODEL // WIDTH
    const = lambda i: (0, 0)
    return pl.pallas_call(
        _mix_kernel,
        grid=(SEQ // tm,),
        in_specs=[
            pl.BlockSpec((N_HEADS, tm, HEAD_DIM), lambda i: (0, i, 0)),
            pl.BlockSpec((N_HEADS, tm, HEAD_DIM), lambda i: (0, i, 0)),
            pl.BlockSpec((gate_tiles, N_HEADS, tm, HEAD_DIM), lambda i: (3, 0, i, 0)),
            pl.BlockSpec((gate_tiles, N_HEADS, tm, HEAD_DIM), lambda i: (4, 0, i, 0)),
            pl.BlockSpec((tm, D_MODEL), lambda i: (i, 0)),
            pl.BlockSpec((WIDTH, D_MODEL), const),
            pl.BlockSpec((WIDTH, D_MODEL), const),
            pl.BlockSpec((D_MODEL, D_MODEL), const),
            pl.BlockSpec((1, D_MODEL), const),
        ],
        out_specs=[
            pl.BlockSpec((tm, D_MODEL), lambda i: (i, 0)),
            pl.BlockSpec((tm, D_MODEL), lambda i: (i, 0)),
        ],
        out_shape=[
            jax.ShapeDtypeStruct((SEQ, D_MODEL), F32),
            jax.ShapeDtypeStruct((SEQ, D_MODEL), BF16),
        ],
        compiler_params=pltpu.CompilerParams(
            dimension_semantics=("arbitrary",), vmem_limit_bytes=VMEM_LIMIT),
        name="mix",
    )(a, b, z4, z4, x2, wpa, wpb, wo, g)


def _mlp_kernel(hm_ref, x1_ref, wu_ref, wd_ref, g_ref, o_ref, acc_ref):
    c = pl.program_id(1)

    @pl.when(c == 0)
    def _():
        acc_ref[...] = x1_ref[...]

    hcol = jnp.dot(hm_ref[...], wu_ref[...], preferred_element_type=F32)
    hcol = jnp.square(jnp.maximum(hcol, 0.0)).astype(BF16)
    acc_ref[...] += jnp.dot(hcol, wd_ref[...], preferred_element_type=F32)

    @pl.when(c == pl.num_programs(1) - 1)
    def _():
        o_ref[...] = _rms_rows(acc_ref[...], g_ref[...])


def _mlp(hm, x1, wu, wd, g):
    tm, fc = MLP_TM, MLP_FC
    return pl.pallas_call(
        _mlp_kernel,
        grid=(SEQ // tm, D_FF // fc),
        in_specs=[
            pl.BlockSpec((tm, D_MODEL), lambda i, c: (i, 0)),
            pl.BlockSpec((tm, D_MODEL), lambda i, c: (i, 0)),
            pl.BlockSpec((D_MODEL, fc), lambda i, c: (0, c)),
            pl.BlockSpec((fc, D_MODEL), lambda i, c: (c, 0)),
            pl.BlockSpec((1, D_MODEL), lambda i, c: (0, 0)),
        ],
        out_specs=pl.BlockSpec((tm, D_MODEL), lambda i, c: (i, 0)),
        out_shape=jax.ShapeDtypeStruct((SEQ, D_MODEL), F32),
        scratch_shapes=[pltpu.VMEM((tm, D_MODEL), F32)],
        compiler_params=pltpu.CompilerParams(
            dimension_semantics=("arbitrary", "arbitrary"), vmem_limit_bytes=VMEM_LIMIT),
        name="mlp",
    )(hm, x1, wu, wd, g)


def _tiles_transposed(v):
    return v.reshape(N_HEADS, N_TILES, ATT_T, HEAD_DIM).transpose(0, 1, 3, 2)


def kernel(x, norm_mix_g, w_in, b_forget, w_proj_a, w_proj_b, w_out, norm_mlp_g, w_up, w_down,
           norm_final_g):
    f_lo = 6 * WIDTH
    f_hi = f_lo + N_HEADS
    w_main = jnp.concatenate([w_in[:, :f_lo], w_in[:, f_hi:]], axis=1).astype(BF16)
    w_f = jnp.pad(w_in[:, f_lo:f_hi], ((0, 0), (0, F_PAD - N_HEADS))).astype(BF16)
    row = lambda g: g.reshape(1, D_MODEL).astype(F32)

    x2 = x.reshape(SEQ, D_MODEL)
    z4, f = _inproj(x2, row(norm_mix_g), w_main, w_f)

    b_pad = jnp.pad(b_forget.astype(F32), (0, F_PAD - N_HEADS)).reshape(1, F_PAD)
    c2 = _fox_prep(f, b_pad)[:, :N_HEADS]
    c_col = c2.T.reshape(N_HEADS, SEQ, 1)
    c_row = c2.T.reshape(N_HEADS, 1, SEQ)

    slopes = jnp.exp2(-8.0 * jnp.arange(1, N_HEADS + 1, dtype=F32) / N_HEADS)
    a = _moba(slopes, z4, _tiles_transposed(z4[2]))
    b = _fox(z4, _tiles_transposed(z4[5]), c_col, c_row)

    x1, hm = _mix(a, b, z4, x2, w_proj_a.astype(BF16), w_proj_b.astype(BF16),
                  w_out.astype(BF16), row(norm_mlp_g))
    out = _mlp(hm, x1, w_up.astype(BF16), w_down.astype(BF16), row(norm_final_g))
    return out.reshape(x.shape)
```

```python
import math

import jax
import jax.numpy as jnp
from jax import lax
from jax.experimental import pallas as pl
from jax.experimental.pallas import tpu as pltpu

D_MODEL = 2048
SEQ = 8192
HEAD_DIM = 128
N_HEADS = 8
WIDTH = N_HEADS * HEAD_DIM
MOBA_BLOCK = 256
MOBA_TOPK = 3
N_BLOCKS = SEQ // MOBA_BLOCK
D_FF = 4 * D_MODEL
RMS_EPS = 1e-6
LOG2E = math.log2(math.e)
SCALE2 = HEAD_DIM ** -0.5 * LOG2E
N_MAIN = 6 * WIDTH + 2 * D_MODEL
F_PAD = 128

BF16 = jnp.bfloat16
F32 = jnp.float32
NEG_INF = float("-inf")
POS_INF = float("inf")

VMEM_LIMIT = 56 * 1024 * 1024

IN_TM, IN_TN = 1024, 1024
Q_A_TILE, K_A_TILE, V_A_TILE, Q_B_TILE, K_B_TILE, V_B_TILE = range(6)
ATT_T = 2 * MOBA_BLOCK
ATT_G = 4
N_SPLIT = 3
N_TILES = SEQ // ATT_T
MIX_TM = 512
MLP_TM, MLP_FC = 512, 512
ROW_CHUNK = 128


def _nt_dot(a, b):
    return lax.dot_general(a, b, (((1,), (1,)), ((), ())), preferred_element_type=F32)


def _rms_rows(x, g):
    ms = jnp.mean(x * x, axis=-1, keepdims=True)
    return x * lax.rsqrt(ms + RMS_EPS) * g


def _inproj_kernel(x_ref, g_ref, w_ref, wf_ref, z_ref, f_ref, hn_ref):
    j = pl.program_id(1)

    @pl.when(j == 0)
    def _():
        def body(r, _):
            rows = pl.ds(pl.multiple_of(r * ROW_CHUNK, ROW_CHUNK), ROW_CHUNK)
            hn = _rms_rows(x_ref[rows, :], g_ref[...]).astype(BF16)
            hn_ref[rows, :] = hn
            f_ref[rows, :] = jnp.dot(hn, wf_ref[...], preferred_element_type=F32)
            return 0
        lax.fori_loop(0, IN_TM // ROW_CHUNK, body, 0)

    scale = jnp.where((j == Q_A_TILE) | (j == Q_B_TILE), SCALE2, 1.0)
    z = (jnp.dot(hn_ref[...], w_ref[...], preferred_element_type=F32) * scale).astype(BF16)
    for h in range(IN_TN // HEAD_DIM):
        z_ref[h] = z[:, h * HEAD_DIM:(h + 1) * HEAD_DIM]


def _inproj(x2, g, w_main, w_f):
    n_i, n_j = SEQ // IN_TM, N_MAIN // IN_TN
    slabs = IN_TN // HEAD_DIM
    return pl.pallas_call(
        _inproj_kernel,
        grid=(n_i, n_j),
        in_specs=[
            pl.BlockSpec((IN_TM, D_MODEL), lambda i, j: (i, 0)),
            pl.BlockSpec((1, D_MODEL), lambda i, j: (0, 0)),
            pl.BlockSpec((D_MODEL, IN_TN), lambda i, j: (0, j)),
            pl.BlockSpec((D_MODEL, F_PAD), lambda i, j: (0, 0)),
        ],
        out_specs=[
            pl.BlockSpec((None, slabs, IN_TM, HEAD_DIM), lambda i, j: (j, 0, i, 0)),
            pl.BlockSpec((IN_TM, F_PAD), lambda i, j: (i, 0)),
        ],
        out_shape=[
            jax.ShapeDtypeStruct((n_j, slabs, SEQ, HEAD_DIM), BF16),
            jax.ShapeDtypeStruct((SEQ, F_PAD), F32),
        ],
        scratch_shapes=[pltpu.VMEM((IN_TM, D_MODEL), BF16)],
        compiler_params=pltpu.CompilerParams(
            dimension_semantics=("arbitrary", "arbitrary"), vmem_limit_bytes=VMEM_LIMIT),
        name="inproj",
    )(x2, g, w_main, w_f)


def _split(v):
    pieces = []
    for _ in range(N_SPLIT):
        p = v.astype(BF16).astype(F32)
        pieces.append(p)
        v = v - p
    return pieces


def _bias_features(v, lane, pieces_at, const_at, const):
    f = jnp.zeros_like(v)
    for n, p in enumerate(_split(v)):
        f = jnp.where(lane == pieces_at + n, p, f)
    f = jnp.where((lane >= const_at) & (lane < const_at + N_SPLIT), const, f)
    return f.astype(BF16)


def _fox_prep_kernel(f_ref, b_ref, ek_ref, eq_ref):
    n = ROW_CHUNK
    tri = (lax.broadcasted_iota(jnp.int32, (n, n), 0)
           >= lax.broadcasted_iota(jnp.int32, (n, n), 1)).astype(BF16)
    lane = lax.broadcasted_iota(jnp.int32, (n, F_PAD), 1)

    def body(ci, carry):
        rows = pl.ds(pl.multiple_of(ci * n, n), n)
        lf = jax.nn.log_sigmoid(f_ref[rows, :] + b_ref[...])
        cs = carry
        for p in _split(lf):
            cs = cs + jnp.dot(tri, p.astype(BF16), preferred_element_type=F32)
        c2 = cs * LOG2E
        for h in range(N_HEADS):
            col = jnp.broadcast_to(c2[:, h:h + 1], (n, F_PAD))
            ek_ref[h, rows, :] = _bias_features(col, lane, 0, N_SPLIT, 1.0)
            eq_ref[h, rows, :] = _bias_features(col, lane, N_SPLIT, 0, -1.0)
        return cs[n - 1:n, :]

    lax.fori_loop(0, SEQ // n, body, jnp.zeros((1, F_PAD), F32))


def _fox_prep(f, b_pad):
    feat = jax.ShapeDtypeStruct((N_HEADS, SEQ, F_PAD), BF16)
    return pl.pallas_call(
        _fox_prep_kernel,
        out_shape=[feat, feat],
        compiler_params=pltpu.CompilerParams(vmem_limit_bytes=VMEM_LIMIT),
        name="fox_prep",
    )(f, b_pad)


def _tile_iotas():
    ko = lax.broadcasted_iota(jnp.int32, (ATT_T, ATT_T), 0)
    qo = lax.broadcasted_iota(jnp.int32, (ATT_T, ATT_T), 1)
    return ko, qo


def _tile_rows(j):
    return pl.ds(pl.multiple_of(j * ATT_T, ATT_T), ATT_T)


def _scores(k_ref, ek, q_ref, eq, g, j):
    keys = jnp.concatenate([k_ref[g, _tile_rows(j), :], ek], axis=1)
    queries = jnp.concatenate([q_ref[g], eq], axis=1)
    return _nt_dot(keys, queries)


def _start(g, s, vt_blk, m_ref, l_ref, acc_ref):
    m0 = jnp.max(s, axis=0, keepdims=True)
    p = jnp.exp2(s - m0)
    m_ref[g] = m0
    l_ref[g] = jnp.sum(p, axis=0, keepdims=True)
    acc_ref[g] = jnp.dot(vt_blk, p.astype(BF16), preferred_element_type=F32)


def _accumulate(g, p, m_new, vt_blk, m_ref, l_ref, acc_ref):
    alpha = jnp.exp2(m_ref[g] - m_new)
    l_ref[g] = alpha * l_ref[g] + jnp.sum(p, axis=0, keepdims=True)
    acc_ref[g] = alpha * acc_ref[g] + jnp.dot(vt_blk, p.astype(BF16), preferred_element_type=F32)
    m_ref[g] = m_new


def _staggered(n, score_fn, softmax_fn, update_fn):
    bases, probs = {}, {}
    for step in range(n + 2):
        if step < n:
            bases[step] = score_fn(step)
        if 0 <= step - 1 < n:
            probs[step - 1] = softmax_fn(step - 1, bases.pop(step - 1))
        if 0 <= step - 2 < n:
            update_fn(step - 2, probs.pop(step - 2))


def _finish(g, acc_ref, l_ref, o_ref):
    out = acc_ref[g] / l_ref[g]
    o_ref[g] = out.T.astype(o_ref.dtype)


def _moba_kernel(slope_ref, q_ref, k_ref, vt_ref, o_ref,
                 kmean_ref, ek_ref, eq_ref, selb_ref, m_ref, l_ref, acc_ref):
    hg = pl.program_id(0)
    i = pl.program_id(1)
    ko, qo = _tile_iotas()
    half = MOBA_BLOCK
    slope2 = [slope_ref[hg * ATT_G + g] * LOG2E for g in range(ATT_G)]

    @pl.when(i == 0)
    def _():
        off = lax.broadcasted_iota(jnp.int32, (ATT_T, HEAD_DIM), 0).astype(F32)
        lane = lax.broadcasted_iota(jnp.int32, (ATT_T, HEAD_DIM), 1)
        for g in range(ATT_G):
            def body(b, _):
                rows = pl.ds(pl.multiple_of(b * MOBA_BLOCK, MOBA_BLOCK), MOBA_BLOCK)
                kb = k_ref[g, rows, :].astype(F32)
                kmean_ref[g, pl.ds(b, 1), :] = jnp.mean(kb, axis=0, keepdims=True)
                return 0
            lax.fori_loop(0, N_BLOCKS, body, 0)
            ek_ref[g] = _bias_features(off * slope2[g], lane, 0, N_SPLIT, 1.0)
            eq_ref[g] = _bias_features(off * (-slope2[g]), lane, N_SPLIT, 0, 1.0)

    jj = lax.broadcasted_iota(jnp.int32, (N_BLOCKS, ATT_T), 0)
    lane = lax.broadcasted_iota(jnp.int32, (N_BLOCKS, ATT_T), 1)
    own = 2 * i + jnp.where(lane >= half, 1, 0)
    past = jj < own
    for g in range(ATT_G):
        q = q_ref[g]
        km = kmean_ref[g]
        km_hi = km.astype(BF16)
        km_lo = (km - km_hi.astype(F32)).astype(BF16)
        gate = _nt_dot(km_hi, q) + _nt_dot(km_lo, q)
        gm = jnp.where(past, gate, NEG_INF)
        cnt = jnp.zeros((N_BLOCKS, ATT_T), F32)
        for jp in range(N_BLOCKS):
            row = gm[jp:jp + 1, :]
            tie = jnp.where(jj > jp, 1.0, 0.0)
            cnt = cnt + jnp.where(row > gm, 1.0, jnp.where(row == gm, tie, 0.0))
        chosen = jnp.where(past, cnt, float(MOBA_TOPK)) < float(MOBA_TOPK)
        selb_ref[g] = jnp.where(chosen, 0.0, NEG_INF)

    for g in range(ATT_G):
        base = _scores(k_ref, ek_ref[g], q_ref, eq_ref[g], g, i)
        sel0 = selb_ref[g, pl.ds(2 * i, 1), :]
        vis0 = jnp.where(qo[:1, :] < half, 0.0, sel0)
        s = base + jnp.where(ko < half, vis0, 0.0)
        s = jnp.where(ko <= qo, s, NEG_INF)
        _start(g, s, vt_ref[g, i], m_ref, l_ref, acc_ref)

    def body(t, _):
        def softmax(g, base):
            cj = -slope2[g] * (ATT_T * (i - t)).astype(F32)
            r0 = selb_ref[g, pl.ds(2 * t, 1), :]
            r1 = selb_ref[g, pl.ds(2 * t + 1, 1), :]
            b0, b1 = base[:half], base[half:]
            bm = jnp.maximum(jnp.max(b0, axis=0, keepdims=True) + r0,
                             jnp.max(b1, axis=0, keepdims=True) + r1) + cj
            m_new = jnp.maximum(m_ref[g], bm)
            t0 = jnp.where(r0 == 0.0, m_new - cj, POS_INF)
            t1 = jnp.where(r1 == 0.0, m_new - cj, POS_INF)
            return jnp.concatenate([jnp.exp2(b0 - t0), jnp.exp2(b1 - t1)], axis=0), m_new

        def update(g, pm):
            _accumulate(g, pm[0], pm[1], vt_ref[g, t], m_ref, l_ref, acc_ref)

        _staggered(ATT_G, lambda g: _scores(k_ref, ek_ref[g], q_ref, eq_ref[g], g, t),
                   softmax, update)
        return 0
    lax.fori_loop(0, i, body, 0)

    for g in range(ATT_G):
        _finish(g, acc_ref, l_ref, o_ref)


def _moba(slopes, z4, vt):
    G, T = ATT_G, ATT_T
    return pl.pallas_call(
        _moba_kernel,
        grid=(N_HEADS // G, N_TILES),
        in_specs=[
            pl.BlockSpec(memory_space=pltpu.SMEM),
            pl.BlockSpec((None, G, T, HEAD_DIM), lambda h, i: (Q_A_TILE, h, i, 0)),
            pl.BlockSpec((None, G, SEQ, HEAD_DIM), lambda h, i: (K_A_TILE, h, 0, 0),
                         pipeline_mode=pl.Buffered(1)),
            pl.BlockSpec((G, N_TILES, HEAD_DIM, T), lambda h, i: (h, 0, 0, 0),
                         pipeline_mode=pl.Buffered(1)),
        ],
        out_specs=pl.BlockSpec((G, T, HEAD_DIM), lambda h, i: (h, i, 0)),
        out_shape=jax.ShapeDtypeStruct((N_HEADS, SEQ, HEAD_DIM), BF16),
        scratch_shapes=[
            pltpu.VMEM((G, N_BLOCKS, HEAD_DIM), F32),
            pltpu.VMEM((G, T, HEAD_DIM), BF16),
            pltpu.VMEM((G, T, HEAD_DIM), BF16),
            pltpu.VMEM((G, N_BLOCKS, T), F32),
            pltpu.VMEM((G, 1, T), F32),
            pltpu.VMEM((G, 1, T), F32),
            pltpu.VMEM((G, HEAD_DIM, T), F32),
        ],
        compiler_params=pltpu.CompilerParams(
            dimension_semantics=("arbitrary", "arbitrary"), vmem_limit_bytes=VMEM_LIMIT),
        name="moba",
    )(slopes, z4, z4, vt)


def _fox_kernel(q_ref, eq_ref, k_ref, ek_ref, vt_ref, o_ref, m_ref, l_ref, acc_ref):
    i = pl.program_id(1)
    ko, qo = _tile_iotas()

    def scores(g, j):
        return _scores(k_ref, ek_ref[g, _tile_rows(j), :], q_ref, eq_ref[g], g, j)

    for g in range(ATT_G):
        s = jnp.where(ko <= qo, scores(g, i), NEG_INF)
        _start(g, s, vt_ref[g, i], m_ref, l_ref, acc_ref)

    def body(j, _):
        def softmax(g, base):
            m_new = jnp.maximum(m_ref[g], jnp.max(base, axis=0, keepdims=True))
            return jnp.exp2(base - m_new), m_new

        def update(g, pm):
            _accumulate(g, pm[0], pm[1], vt_ref[g, j], m_ref, l_ref, acc_ref)

        _staggered(ATT_G, lambda g: scores(g, j), softmax, update)
        return 0
    lax.fori_loop(0, i, body, 0)

    for g in range(ATT_G):
        _finish(g, acc_ref, l_ref, o_ref)


def _fox(z4, vt, ek, eq):
    G, T = ATT_G, ATT_T
    resident = pl.Buffered(1)
    return pl.pallas_call(
        _fox_kernel,
        grid=(N_HEADS // G, N_TILES),
        in_specs=[
            pl.BlockSpec((None, G, T, HEAD_DIM), lambda h, i: (Q_B_TILE, h, i, 0)),
            pl.BlockSpec((G, T, F_PAD), lambda h, i: (h, i, 0)),
            pl.BlockSpec((None, G, SEQ, HEAD_DIM), lambda h, i: (K_B_TILE, h, 0, 0),
                         pipeline_mode=resident),
            pl.BlockSpec((G, SEQ, F_PAD), lambda h, i: (h, 0, 0), pipeline_mode=resident),
            pl.BlockSpec((G, N_TILES, HEAD_DIM, T), lambda h, i: (h, 0, 0, 0),
                         pipeline_mode=resident),
        ],
        out_specs=pl.BlockSpec((G, T, HEAD_DIM), lambda h, i: (h, i, 0)),
        out_shape=jax.ShapeDtypeStruct((N_HEADS, SEQ, HEAD_DIM), BF16),
        scratch_shapes=[
            pltpu.VMEM((G, 1, T), F32),
            pltpu.VMEM((G, 1, T), F32),
            pltpu.VMEM((G, HEAD_DIM, T), F32),
        ],
        compiler_params=pltpu.CompilerParams(
            dimension_semantics=("arbitrary", "arbitrary"), vmem_limit_bytes=VMEM_LIMIT),
        name="fox",
    )(z4, eq, z4, ek, vt)


def _mix_kernel(a_ref, b_ref, ga_ref, gb_ref, x_ref, wpa_ref, wpb_ref, wo_ref, g_ref,
                x1_ref, hm_ref):
    def heads(ref):
        return jnp.concatenate([ref[h] for h in range(N_HEADS)], axis=1)

    def gate(ref):
        cols = [ref[t, h] for t in range(D_MODEL // WIDTH) for h in range(N_HEADS)]
        return jax.nn.sigmoid(jnp.concatenate(cols, axis=1).astype(F32))

    pa = jnp.dot(heads(a_ref), wpa_ref[...], preferred_element_type=F32)
    pb = jnp.dot(heads(b_ref), wpb_ref[...], preferred_element_type=F32)
    merged = gate(ga_ref) * pa + gate(gb_ref) * pb
    x1 = x_ref[...] + jnp.dot(merged.astype(BF16), wo_ref[...], preferred_element_type=F32)
    x1_ref[...] = x1
    hm_ref[...] = _rms_rows(x1, g_ref[...]).astype(BF16)


def _mix(a, b, z4, x2, wpa, wpb, wo, g):
    tm = MIX_TM
    gate_tiles = D_MODEL // WIDTH
    const = lambda i: (0, 0)
    return pl.pallas_call(
        _mix_kernel,
        grid=(SEQ // tm,),
        in_specs=[
            pl.BlockSpec((N_HEADS, tm, HEAD_DIM), lambda i: (0, i, 0)),
            pl.BlockSpec((N_HEADS, tm, HEAD_DIM), lambda i: (0, i, 0)),
            pl.BlockSpec((gate_tiles, N_HEADS, tm, HEAD_DIM), lambda i: (3, 0, i, 0)),
            pl.BlockSpec((gate_tiles, N_HEADS, tm, HEAD_DIM), lambda i: (4, 0, i, 0)),
            pl.BlockSpec((tm, D_MODEL), lambda i: (i, 0)),
            pl.BlockSpec((WIDTH, D_MODEL), const),
            pl.BlockSpec((WIDTH, D_MODEL), const),
            pl.BlockSpec((D_MODEL, D_MODEL), const),
            pl.BlockSpec((1, D_MODEL), const),
        ],
        out_specs=[
            pl.BlockSpec((tm, D_MODEL), lambda i: (i, 0)),
            pl.BlockSpec((tm, D_MODEL), lambda i: (i, 0)),
        ],
        out_shape=[
            jax.ShapeDtypeStruct((SEQ, D_MODEL), F32),
            jax.ShapeDtypeStruct((SEQ, D_MODEL), BF16),
        ],
        compiler_params=pltpu.CompilerParams(
            dimension_semantics=("arbitrary",), vmem_limit_bytes=VMEM_LIMIT),
        name="mix",
    )(a, b, z4, z4, x2, wpa, wpb, wo, g)


def _mlp_kernel(hm_ref, x1_ref, wu_ref, wd_ref, g_ref, o_ref, acc_ref):
    c = pl.program_id(1)

    @pl.when(c == 0)
    def _():
        acc_ref[...] = x1_ref[...]

    hcol = jnp.dot(hm_ref[...], wu_ref[...], preferred_element_type=F32)
    hcol = jnp.square(jnp.maximum(hcol, 0.0)).astype(BF16)
    acc_ref[...] += jnp.dot(hcol, wd_ref[...], preferred_element_type=F32)

    @pl.when(c == pl.num_programs(1) - 1)
    def _():
        o_ref[...] = _rms_rows(acc_ref[...], g_ref[...])


def _mlp(hm, x1, wu, wd, g):
    tm, fc = MLP_TM, MLP_FC
    return pl.pallas_call(
        _mlp_kernel,
        grid=(SEQ // tm, D_FF // fc),
        in_specs=[
            pl.BlockSpec((tm, D_MODEL), lambda i, c: (i, 0)),
            pl.BlockSpec((tm, D_MODEL), lambda i, c: (i, 0)),
            pl.BlockSpec((D_MODEL, fc), lambda i, c: (0, c)),
            pl.BlockSpec((fc, D_MODEL), lambda i, c: (c, 0)),
            pl.BlockSpec((1, D_MODEL), lambda i, c: (0, 0)),
        ],
        out_specs=pl.BlockSpec((tm, D_MODEL), lambda i, c: (i, 0)),
        out_shape=jax.ShapeDtypeStruct((SEQ, D_MODEL), F32),
        scratch_shapes=[pltpu.VMEM((tm, D_MODEL), F32)],
        compiler_params=pltpu.CompilerParams(
            dimension_semantics=("arbitrary", "arbitrary"), vmem_limit_bytes=VMEM_LIMIT),
        name="mlp",
    )(hm, x1, wu, wd, g)


def _tiles_transposed(v):
    return v.reshape(N_HEADS, N_TILES, ATT_T, HEAD_DIM).transpose(0, 1, 3, 2)


def kernel(x, norm_mix_g, w_in, b_forget, w_proj_a, w_proj_b, w_out, norm_mlp_g, w_up, w_down,
           norm_final_g):
    f_lo = 6 * WIDTH
    f_hi = f_lo + N_HEADS
    w_main = jnp.concatenate([w_in[:, :f_lo], w_in[:, f_hi:]], axis=1).astype(BF16)
    w_f = jnp.pad(w_in[:, f_lo:f_hi], ((0, 0), (0, F_PAD - N_HEADS))).astype(BF16)
    row = lambda g: g.reshape(1, D_MODEL).astype(F32)

    x2 = x.reshape(SEQ, D_MODEL)
    z4, f = _inproj(x2, row(norm_mix_g), w_main, w_f)

    b_pad = jnp.pad(b_forget.astype(F32), (0, F_PAD - N_HEADS)).reshape(1, F_PAD)
    ek, eq = _fox_prep(f, b_pad)

    slopes = jnp.exp2(-8.0 * jnp.arange(1, N_HEADS + 1, dtype=F32) / N_HEADS)
    a = _moba(slopes, z4, _tiles_transposed(z4[V_A_TILE]))
    b = _fox(z4, _tiles_transposed(z4[V_B_TILE]), ek, eq)

    x1, hm = _mix(a, b, z4, x2, w_proj_a.astype(BF16), w_proj_b.astype(BF16),
                  w_out.astype(BF16), row(norm_mlp_g))
    out = _mlp(hm, x1, w_up.astype(BF16), w_down.astype(BF16), row(norm_final_g))
    return out.reshape(x.shape)
```

```python
import math
from typing import Any, NamedTuple

import jax
import jax.numpy as jnp
from jax import lax
from jax.experimental import pallas as pl
from jax.experimental.pallas import tpu as pltpu

D_MODEL = 2048
SEQ = 8192
HEAD_DIM = 128
N_HEADS = 8
WIDTH = N_HEADS * HEAD_DIM
MOBA_BLOCK = 256
MOBA_TOPK = 3
N_BLOCKS = SEQ // MOBA_BLOCK
D_FF = 4 * D_MODEL
RMS_EPS = 1e-6
LOG2E = math.log2(math.e)
SCALE2 = HEAD_DIM ** -0.5 * LOG2E
N_MAIN = 6 * WIDTH + 2 * D_MODEL
F_PAD = 128

BF16 = jnp.bfloat16
F32 = jnp.float32
NEG_INF = float("-inf")
POS_INF = float("inf")

VMEM_LIMIT = 56 * 1024 * 1024

IN_TM, IN_TN = 1024, 1024
Q_A_TILE, K_A_TILE, V_A_TILE, Q_B_TILE, K_B_TILE, V_B_TILE = range(6)
ATT_T = 2 * MOBA_BLOCK
ATT_G = 4
N_SPLIT = 3
N_TILES = SEQ // ATT_T
MIX_TM = 512
MLP_TM, MLP_FC = 512, 512
ROW_CHUNK = 128


def _nt_dot(a, b):
    return lax.dot_general(a, b, (((1,), (1,)), ((), ())), preferred_element_type=F32)


def _rms_rows(x, g):
    ms = jnp.mean(x * x, axis=-1, keepdims=True)
    return x * lax.rsqrt(ms + RMS_EPS) * g


def _inproj_kernel(x_ref, g_ref, w_ref, wf_ref, z_ref, f_ref, hn_ref):
    j = pl.program_id(1)

    @pl.when(j == 0)
    def _():
        def body(r, _):
            rows = pl.ds(pl.multiple_of(r * ROW_CHUNK, ROW_CHUNK), ROW_CHUNK)
            hn = _rms_rows(x_ref[rows, :], g_ref[...]).astype(BF16)
            hn_ref[rows, :] = hn
            f_ref[rows, :] = jnp.dot(hn, wf_ref[...], preferred_element_type=F32)
            return 0
        lax.fori_loop(0, IN_TM // ROW_CHUNK, body, 0)

    scale = jnp.where((j == Q_A_TILE) | (j == Q_B_TILE), SCALE2, 1.0)
    z = (jnp.dot(hn_ref[...], w_ref[...], preferred_element_type=F32) * scale).astype(BF16)
    for h in range(IN_TN // HEAD_DIM):
        z_ref[h] = z[:, h * HEAD_DIM:(h + 1) * HEAD_DIM]


def _inproj(x2, g, w_main, w_f):
    n_i, n_j = SEQ // IN_TM, N_MAIN // IN_TN
    slabs = IN_TN // HEAD_DIM
    return pl.pallas_call(
        _inproj_kernel,
        grid=(n_i, n_j),
        in_specs=[
            pl.BlockSpec((IN_TM, D_MODEL), lambda i, j: (i, 0)),
            pl.BlockSpec((1, D_MODEL), lambda i, j: (0, 0)),
            pl.BlockSpec((D_MODEL, IN_TN), lambda i, j: (0, j)),
            pl.BlockSpec((D_MODEL, F_PAD), lambda i, j: (0, 0)),
        ],
        out_specs=[
            pl.BlockSpec((None, slabs, IN_TM, HEAD_DIM), lambda i, j: (j, 0, i, 0)),
            pl.BlockSpec((IN_TM, F_PAD), lambda i, j: (i, 0)),
        ],
        out_shape=[
            jax.ShapeDtypeStruct((n_j, slabs, SEQ, HEAD_DIM), BF16),
            jax.ShapeDtypeStruct((SEQ, F_PAD), F32),
        ],
        scratch_shapes=[pltpu.VMEM((IN_TM, D_MODEL), BF16)],
        compiler_params=pltpu.CompilerParams(
            dimension_semantics=("arbitrary", "arbitrary"), vmem_limit_bytes=VMEM_LIMIT),
        name="inproj",
    )(x2, g, w_main, w_f)


def _split(v):
    pieces = []
    for _ in range(N_SPLIT):
        p = v.astype(BF16).astype(F32)
        pieces.append(p)
        v = v - p
    return pieces


def _bias_features(v, lane, pieces_at, const_at, const):
    f = jnp.zeros_like(v)
    for n, p in enumerate(_split(v)):
        f = jnp.where(lane == pieces_at + n, p, f)
    f = jnp.where((lane >= const_at) & (lane < const_at + N_SPLIT), const, f)
    return f.astype(BF16)


def _fox_prep_kernel(f_ref, b_ref, ek_ref, eq_ref):
    n = ROW_CHUNK
    tri = (lax.broadcasted_iota(jnp.int32, (n, n), 0)
           >= lax.broadcasted_iota(jnp.int32, (n, n), 1)).astype(BF16)
    lane = lax.broadcasted_iota(jnp.int32, (n, F_PAD), 1)

    def body(ci, carry):
        rows = pl.ds(pl.multiple_of(ci * n, n), n)
        lf = jax.nn.log_sigmoid(f_ref[rows, :] + b_ref[...])
        cs = carry
        for p in _split(lf):
            cs = cs + jnp.dot(tri, p.astype(BF16), preferred_element_type=F32)
        c2 = cs * LOG2E
        for h in range(N_HEADS):
            col = jnp.broadcast_to(c2[:, h:h + 1], (n, F_PAD))
            ek_ref[h, rows, :] = _bias_features(col, lane, 0, N_SPLIT, 1.0)
            eq_ref[h, rows, :] = _bias_features(col, lane, N_SPLIT, 0, -1.0)
        return cs[n - 1:n, :]

    lax.fori_loop(0, SEQ // n, body, jnp.zeros((1, F_PAD), F32))


def _fox_prep(f, b_pad):
    feat = jax.ShapeDtypeStruct((N_HEADS, SEQ, F_PAD), BF16)
    return pl.pallas_call(
        _fox_prep_kernel,
        out_shape=[feat, feat],
        compiler_params=pltpu.CompilerParams(vmem_limit_bytes=VMEM_LIMIT),
        name="fox_prep",
    )(f, b_pad)


def _tile_iotas():
    ko = lax.broadcasted_iota(jnp.int32, (ATT_T, ATT_T), 0)
    qo = lax.broadcasted_iota(jnp.int32, (ATT_T, ATT_T), 1)
    return ko, qo


def _tile_rows(j):
    return pl.ds(pl.multiple_of(j * ATT_T, ATT_T), ATT_T)


def _scores(k_ref, ek, q_ref, eq, g, j):
    keys = jnp.concatenate([k_ref[g, _tile_rows(j), :], ek], axis=1)
    queries = jnp.concatenate([q_ref[g], eq], axis=1)
    return _nt_dot(keys, queries)


class _Stream(NamedTuple):
    s: Any
    p: Any
    alpha: Any
    m: Any
    l: Any
    acc: Any


def _stream_scratch():
    G, T = ATT_G, ATT_T
    row = pltpu.VMEM((G, 1, T), F32)
    return _Stream(s=pltpu.VMEM((G, T, T), F32), p=pltpu.VMEM((G, T, T), BF16),
                   alpha=row, m=row, l=row, acc=pltpu.VMEM((G, HEAD_DIM, T), F32))


def _fold_first(st, g, s):
    m0 = jnp.max(s, axis=0, keepdims=True)
    p = jnp.exp2(s - m0)
    st.m[g] = m0
    st.l[g] = jnp.sum(p, axis=0, keepdims=True)
    st.p[g] = p.astype(BF16)
    st.alpha[g] = jnp.zeros_like(m0)
    st.acc[g] = jnp.zeros(st.acc.shape[1:], F32)


def _fold(st, g, p, m_new):
    alpha = jnp.exp2(st.m[g] - m_new)
    st.alpha[g] = alpha
    st.l[g] = alpha * st.l[g] + jnp.sum(p, axis=0, keepdims=True)
    st.p[g] = p.astype(BF16)
    st.m[g] = m_new


def _apply(st, g, vt_tile):
    st.acc[g] = st.alpha[g] * st.acc[g] + jnp.dot(vt_tile, st.p[g], preferred_element_type=F32)


def _stream_past_tiles(st, i, vt_ref, score, fold_past, o_ref):
    chains = range(ATT_G)
    for g in chains:
        st.s[g] = score(g, 0)

    def body(r, _):
        prev = jnp.where(r == 1, i, r - 2)
        for g in chains:
            _apply(st, g, vt_ref[g, prev])
            fold_past(g, r - 1)
            st.s[g] = score(g, r)
        return 0
    lax.fori_loop(1, i, body, 0)

    @pl.when(i >= 1)
    def _():
        prev = jnp.where(i == 1, i, i - 2)
        for g in chains:
            _apply(st, g, vt_ref[g, prev])
            fold_past(g, i - 1)

    last = jnp.maximum(i - 1, 0)
    for g in chains:
        _apply(st, g, vt_ref[g, last])
        out = st.acc[g] / st.l[g]
        o_ref[g] = out.T.astype(o_ref.dtype)


def _moba_kernel(slope_ref, q_ref, k_ref, vt_ref, o_ref,
                 kmean_ref, ek_ref, eq_ref, selb_ref, *stream_refs):
    st = _Stream(*stream_refs)
    hg = pl.program_id(0)
    i = pl.program_id(1)
    ko, qo = _tile_iotas()
    half = MOBA_BLOCK
    slope2 = [slope_ref[hg * ATT_G + g] * LOG2E for g in range(ATT_G)]

    @pl.when(i == 0)
    def _():
        off = lax.broadcasted_iota(jnp.int32, (ATT_T, HEAD_DIM), 0).astype(F32)
        lane = lax.broadcasted_iota(jnp.int32, (ATT_T, HEAD_DIM), 1)
        for g in range(ATT_G):
            def body(b, _):
                rows = pl.ds(pl.multiple_of(b * MOBA_BLOCK, MOBA_BLOCK), MOBA_BLOCK)
                kb = k_ref[g, rows, :].astype(F32)
                kmean_ref[g, pl.ds(b, 1), :] = jnp.mean(kb, axis=0, keepdims=True)
                return 0
            lax.fori_loop(0, N_BLOCKS, body, 0)
            ek_ref[g] = _bias_features(off * slope2[g], lane, 0, N_SPLIT, 1.0)
            eq_ref[g] = _bias_features(off * (-slope2[g]), lane, N_SPLIT, 0, 1.0)

    jj = lax.broadcasted_iota(jnp.int32, (N_BLOCKS, ATT_T), 0)
    lane = lax.broadcasted_iota(jnp.int32, (N_BLOCKS, ATT_T), 1)
    own = 2 * i + jnp.where(lane >= half, 1, 0)
    past = jj < own
    for g in range(ATT_G):
        q = q_ref[g]
        km = kmean_ref[g]
        km_hi = km.astype(BF16)
        km_lo = (km - km_hi.astype(F32)).astype(BF16)
        gate = _nt_dot(km_hi, q) + _nt_dot(km_lo, q)
        gm = jnp.where(past, gate, NEG_INF)
        cnt = jnp.zeros((N_BLOCKS, ATT_T), F32)
        for jp in range(N_BLOCKS):
            row = gm[jp:jp + 1, :]
            tie = jnp.where(jj > jp, 1.0, 0.0)
            cnt = cnt + jnp.where(row > gm, 1.0, jnp.where(row == gm, tie, 0.0))
        chosen = jnp.where(past, cnt, float(MOBA_TOPK)) < float(MOBA_TOPK)
        selb_ref[g] = jnp.where(chosen, 0.0, NEG_INF)

    def score(g, tile):
        return _scores(k_ref, ek_ref[g], q_ref, eq_ref[g], g, tile)

    diag = [score(g, i) for g in range(ATT_G)]
    for g in range(ATT_G):
        sel0 = selb_ref[g, pl.ds(2 * i, 1), :]
        vis0 = jnp.where(qo[:1, :] < half, 0.0, sel0)
        s = diag[g] + jnp.where(ko < half, vis0, 0.0)
        _fold_first(st, g, jnp.where(ko <= qo, s, NEG_INF))

    def fold_past(g, t):
        cj = -slope2[g] * (ATT_T * (i - t)).astype(F32)
        r0 = selb_ref[g, pl.ds(2 * t, 1), :]
        r1 = selb_ref[g, pl.ds(2 * t + 1, 1), :]
        b0, b1 = st.s[g, :half, :], st.s[g, half:, :]
        bm = jnp.maximum(jnp.max(b0, axis=0, keepdims=True) + r0,
                         jnp.max(b1, axis=0, keepdims=True) + r1) + cj
        m_new = jnp.maximum(st.m[g], bm)
        t0 = jnp.where(r0 == 0.0, m_new - cj, POS_INF)
        t1 = jnp.where(r1 == 0.0, m_new - cj, POS_INF)
        p = jnp.concatenate([jnp.exp2(b0 - t0), jnp.exp2(b1 - t1)], axis=0)
        _fold(st, g, p, m_new)

    _stream_past_tiles(st, i, vt_ref, score, fold_past, o_ref)


def _moba(slopes, z4, vt):
    G, T = ATT_G, ATT_T
    return pl.pallas_call(
        _moba_kernel,
        grid=(N_HEADS // G, N_TILES),
        in_specs=[
            pl.BlockSpec(memory_space=pltpu.SMEM),
            pl.BlockSpec((None, G, T, HEAD_DIM), lambda h, i: (Q_A_TILE, h, i, 0)),
            pl.BlockSpec((None, G, SEQ, HEAD_DIM), lambda h, i: (K_A_TILE, h, 0, 0),
                         pipeline_mode=pl.Buffered(1)),
            pl.BlockSpec((G, N_TILES, HEAD_DIM, T), lambda h, i: (h, 0, 0, 0),
                         pipeline_mode=pl.Buffered(1)),
        ],
        out_specs=pl.BlockSpec((G, T, HEAD_DIM), lambda h, i: (h, i, 0)),
        out_shape=jax.ShapeDtypeStruct((N_HEADS, SEQ, HEAD_DIM), BF16),
        scratch_shapes=[
            pltpu.VMEM((G, N_BLOCKS, HEAD_DIM), F32),
            pltpu.VMEM((G, T, HEAD_DIM), BF16),
            pltpu.VMEM((G, T, HEAD_DIM), BF16),
            pltpu.VMEM((G, N_BLOCKS, T), F32),
            *_stream_scratch(),
        ],
        compiler_params=pltpu.CompilerParams(
            dimension_semantics=("arbitrary", "arbitrary"), vmem_limit_bytes=VMEM_LIMIT),
        name="moba",
    )(slopes, z4, z4, vt)


def _fox_kernel(q_ref, eq_ref, k_ref, ek_ref, vt_ref, o_ref, *stream_refs):
    st = _Stream(*stream_refs)
    i = pl.program_id(1)
    ko, qo = _tile_iotas()

    def score(g, tile):
        return _scores(k_ref, ek_ref[g, _tile_rows(tile), :], q_ref, eq_ref[g], g, tile)

    diag = [score(g, i) for g in range(ATT_G)]
    for g in range(ATT_G):
        _fold_first(st, g, jnp.where(ko <= qo, diag[g], NEG_INF))

    def fold_past(g, tile):
        s = st.s[g]
        m_new = jnp.maximum(st.m[g], jnp.max(s, axis=0, keepdims=True))
        _fold(st, g, jnp.exp2(s - m_new), m_new)

    _stream_past_tiles(st, i, vt_ref, score, fold_past, o_ref)


def _fox(z4, vt, ek, eq):
    G, T = ATT_G, ATT_T
    resident = pl.Buffered(1)
    return pl.pallas_call(
        _fox_kernel,
        grid=(N_HEADS // G, N_TILES),
        in_specs=[
            pl.BlockSpec((None, G, T, HEAD_DIM), lambda h, i: (Q_B_TILE, h, i, 0)),
            pl.BlockSpec((G, T, F_PAD), lambda h, i: (h, i, 0)),
            pl.BlockSpec((None, G, SEQ, HEAD_DIM), lambda h, i: (K_B_TILE, h, 0, 0),
                         pipeline_mode=resident),
            pl.BlockSpec((G, SEQ, F_PAD), lambda h, i: (h, 0, 0), pipeline_mode=resident),
            pl.BlockSpec((G, N_TILES, HEAD_DIM, T), lambda h, i: (h, 0, 0, 0),
                         pipeline_mode=resident),
        ],
        out_specs=pl.BlockSpec((G, T, HEAD_DIM), lambda h, i: (h, i, 0)),
        out_shape=jax.ShapeDtypeStruct((N_HEADS, SEQ, HEAD_DIM), BF16),
        scratch_shapes=list(_stream_scratch()),
        compiler_params=pltpu.CompilerParams(
            dimension_semantics=("arbitrary", "arbitrary"), vmem_limit_bytes=VMEM_LIMIT),
        name="fox",
    )(z4, eq, z4, ek, vt)


def _mix_kernel(a_ref, b_ref, ga_ref, gb_ref, x_ref, wpa_ref, wpb_ref, wo_ref, g_ref,
                x1_ref, hm_ref):
    def heads(ref):
        return jnp.concatenate([ref[h] for h in range(N_HEADS)], axis=1)

    def gate(ref):
        cols = [ref[t, h] for t in range(D_MODEL // WIDTH) for h in range(N_HEADS)]
        return jax.nn.sigmoid(jnp.concatenate(cols, axis=1).astype(F32))

    pa = jnp.dot(heads(a_ref), wpa_ref[...], preferred_element_type=F32)
    pb = jnp.dot(heads(b_ref), wpb_ref[...], preferred_element_type=F32)
    merged = gate(ga_ref) * pa + gate(gb_ref) * pb
    x1 = x_ref[...] + jnp.dot(merged.astype(BF16), wo_ref[...], preferred_element_type=F32)
    x1_ref[...] = x1
    hm_ref[...] = _rms_rows(x1, g_ref[...]).astype(BF16)


def _mix(a, b, z4, x2, wpa, wpb, wo, g):
    tm = MIX_TM
    gate_tiles = D_MODEL // WIDTH
    const = lambda i: (0, 0)
    return pl.pallas_call(
        _mix_kernel,
        grid=(SEQ // tm,),
        in_specs=[
            pl.BlockSpec((N_HEADS, tm, HEAD_DIM), lambda i: (0, i, 0)),
            pl.BlockSpec((N_HEADS, tm, HEAD_DIM), lambda i: (0, i, 0)),
            pl.BlockSpec((gate_tiles, N_HEADS, tm, HEAD_DIM), lambda i: (3, 0, i, 0)),
            pl.BlockSpec((gate_tiles, N_HEADS, tm, HEAD_DIM), lambda i: (4, 0, i, 0)),
            pl.BlockSpec((tm, D_MODEL), lambda i: (i, 0)),
            pl.BlockSpec((WIDTH, D_MODEL), const),
            pl.BlockSpec((WIDTH, D_MODEL), const),
            pl.BlockSpec((D_MODEL, D_MODEL), const),
            pl.BlockSpec((1, D_MODEL), const),
        ],
        out_specs=[
            pl.BlockSpec((tm, D_MODEL), lambda i: (i, 0)),
            pl.BlockSpec((tm, D_MODEL), lambda i: (i, 0)),
        ],
        out_shape=[
            jax.ShapeDtypeStruct((SEQ, D_MODEL), F32),
            jax.ShapeDtypeStruct((SEQ, D_MODEL), BF16),
        ],
        compiler_params=pltpu.CompilerParams(
            dimension_semantics=("arbitrary",), vmem_limit_bytes=VMEM_LIMIT),
        name="mix",
    )(a, b, z4, z4, x2, wpa, wpb, wo, g)


def _mlp_kernel(hm_ref, x1_ref, wu_ref, wd_ref, g_ref, o_ref, acc_ref):
    c = pl.program_id(1)

    @pl.when(c == 0)
    def _():
        acc_ref[...] = x1_ref[...]

    hcol = jnp.dot(hm_ref[...], wu_ref[...], preferred_element_type=F32)
    hcol = jnp.square(jnp.maximum(hcol, 0.0)).astype(BF16)
    acc_ref[...] += jnp.dot(hcol, wd_ref[...], preferred_element_type=F32)

    @pl.when(c == pl.num_programs(1) - 1)
    def _():
        o_ref[...] = _rms_rows(acc_ref[...], g_ref[...])


def _mlp(hm, x1, wu, wd, g):
    tm, fc = MLP_TM, MLP_FC
    return pl.pallas_call(
        _mlp_kernel,
        grid=(SEQ // tm, D_FF // fc),
        in_specs=[
            pl.BlockSpec((tm, D_MODEL), lambda i, c: (i, 0)),
            pl.BlockSpec((tm, D_MODEL), lambda i, c: (i, 0)),
            pl.BlockSpec((D_MODEL, fc), lambda i, c: (0, c)),
            pl.BlockSpec((fc, D_MODEL), lambda i, c: (c, 0)),
            pl.BlockSpec((1, D_MODEL), lambda i, c: (0, 0)),
        ],
        out_specs=pl.BlockSpec((tm, D_MODEL), lambda i, c: (i, 0)),
        out_shape=jax.ShapeDtypeStruct((SEQ, D_MODEL), F32),
        scratch_shapes=[pltpu.VMEM((tm, D_MODEL), F32)],
        compiler_params=pltpu.CompilerParams(
            dimension_semantics=("arbitrary", "arbitrary"), vmem_limit_bytes=VMEM_LIMIT),
        name="mlp",
    )(hm, x1, wu, wd, g)


def _tiles_transposed(v):
    return v.reshape(N_HEADS, N_TILES, ATT_T, HEAD_DIM).transpose(0, 1, 3, 2)


def kernel(x, norm_mix_g, w_in, b_forget, w_proj_a, w_proj_b, w_out, norm_mlp_g, w_up, w_down,
           norm_final_g):
    f_lo = 6 * WIDTH
    f_hi = f_lo + N_HEADS
    w_main = jnp.concatenate([w_in[:, :f_lo], w_in[:, f_hi:]], axis=1).astype(BF16)
    w_f = jnp.pad(w_in[:, f_lo:f_hi], ((0, 0), (0, F_PAD - N_HEADS))).astype(BF16)
    row = lambda g: g.reshape(1, D_MODEL).astype(F32)

    x2 = x.reshape(SEQ, D_MODEL)
    z4, f = _inproj(x2, row(norm_mix_g), w_main, w_f)

    b_pad = jnp.pad(b_forget.astype(F32), (0, F_PAD - N_HEADS)).reshape(1, F_PAD)
    ek, eq = _fox_prep(f, b_pad)

    slopes = jnp.exp2(-8.0 * jnp.arange(1, N_HEADS + 1, dtype=F32) / N_HEADS)
    a = _moba(slopes, z4, _tiles_transposed(z4[V_A_TILE]))
    b = _fox(z4, _tiles_transposed(z4[V_B_TILE]), ek, eq)

    x1, hm = _mix(a, b, z4, x2, w_proj_a.astype(BF16), w_proj_b.astype(BF16),
                  w_out.astype(BF16), row(norm_mlp_g))
    out = _mlp(hm, x1, w_up.astype(BF16), w_down.astype(BF16), row(norm_final_g))
    return out.reshape(x.shape)
```

```python
import math
from typing import Any, NamedTuple

import jax
import jax.numpy as jnp
from jax import lax
from jax.experimental import pallas as pl
from jax.experimental.pallas import tpu as pltpu

D_MODEL = 2048
SEQ = 8192
HEAD_DIM = 128
N_HEADS = 8
WIDTH = N_HEADS * HEAD_DIM
MOBA_BLOCK = 256
MOBA_TOPK = 3
N_BLOCKS = SEQ // MOBA_BLOCK
D_FF = 4 * D_MODEL
RMS_EPS = 1e-6
LOG2E = math.log2(math.e)
SCALE2 = HEAD_DIM ** -0.5 * LOG2E
N_MAIN = 6 * WIDTH + 2 * D_MODEL
F_PAD = 128

BF16 = jnp.bfloat16
F32 = jnp.float32
NEG_INF = float("-inf")
POS_INF = float("inf")

VMEM_LIMIT = 56 * 1024 * 1024

IN_TM, IN_TN = 1024, 1024
Q_A_TILE, K_A_TILE, V_A_TILE, Q_B_TILE, K_B_TILE, V_B_TILE = range(6)
ATT_T = 2 * MOBA_BLOCK
ATT_G = 4
N_SPLIT = 3
N_TILES = SEQ // ATT_T
MIX_TM = 512
MLP_TM, MLP_FC = 512, 1024
ROW_CHUNK = 128


def _nt_dot(a, b):
    return lax.dot_general(a, b, (((1,), (1,)), ((), ())), preferred_element_type=F32)


def _rms_rows(x, g):
    ms = jnp.mean(x * x, axis=-1, keepdims=True)
    return x * lax.rsqrt(ms + RMS_EPS) * g


def _inproj_kernel(x_ref, g_ref, w_ref, wf_ref, z_ref, f_ref, hn_ref):
    j = pl.program_id(1)

    @pl.when(j == 0)
    def _():
        def body(r, _):
            rows = pl.ds(pl.multiple_of(r * ROW_CHUNK, ROW_CHUNK), ROW_CHUNK)
            hn = _rms_rows(x_ref[rows, :], g_ref[...]).astype(BF16)
            hn_ref[rows, :] = hn
            f_ref[rows, :] = jnp.dot(hn, wf_ref[...], preferred_element_type=F32)
            return 0
        lax.fori_loop(0, IN_TM // ROW_CHUNK, body, 0)

    scale = jnp.where((j == Q_A_TILE) | (j == Q_B_TILE), SCALE2, 1.0)
    z = (jnp.dot(hn_ref[...], w_ref[...], preferred_element_type=F32) * scale).astype(BF16)
    for h in range(IN_TN // HEAD_DIM):
        z_ref[h] = z[:, h * HEAD_DIM:(h + 1) * HEAD_DIM]


def _inproj(x2, g, w_main, w_f):
    n_i, n_j = SEQ // IN_TM, N_MAIN // IN_TN
    slabs = IN_TN // HEAD_DIM
    return pl.pallas_call(
        _inproj_kernel,
        grid=(n_i, n_j),
        in_specs=[
            pl.BlockSpec((IN_TM, D_MODEL), lambda i, j: (i, 0)),
            pl.BlockSpec((1, D_MODEL), lambda i, j: (0, 0)),
            pl.BlockSpec((D_MODEL, IN_TN), lambda i, j: (0, j)),
            pl.BlockSpec((D_MODEL, F_PAD), lambda i, j: (0, 0)),
        ],
        out_specs=[
            pl.BlockSpec((None, slabs, IN_TM, HEAD_DIM), lambda i, j: (j, 0, i, 0)),
            pl.BlockSpec((IN_TM, F_PAD), lambda i, j: (i, 0)),
        ],
        out_shape=[
            jax.ShapeDtypeStruct((n_j, slabs, SEQ, HEAD_DIM), BF16),
            jax.ShapeDtypeStruct((SEQ, F_PAD), F32),
        ],
        scratch_shapes=[pltpu.VMEM((IN_TM, D_MODEL), BF16)],
        compiler_params=pltpu.CompilerParams(
            dimension_semantics=("arbitrary", "arbitrary"), vmem_limit_bytes=VMEM_LIMIT),
        name="inproj",
    )(x2, g, w_main, w_f)


def _split(v):
    pieces = []
    for _ in range(N_SPLIT):
        p = v.astype(BF16).astype(F32)
        pieces.append(p)
        v = v - p
    return pieces


def _bias_features(v, lane, pieces_at, const_at, const):
    f = jnp.zeros_like(v)
    for n, p in enumerate(_split(v)):
        f = jnp.where(lane == pieces_at + n, p, f)
    f = jnp.where((lane >= const_at) & (lane < const_at + N_SPLIT), const, f)
    return f.astype(BF16)


def _fox_prep_kernel(f_ref, b_ref, ek_ref, eq_ref):
    n = ROW_CHUNK
    tri = (lax.broadcasted_iota(jnp.int32, (n, n), 0)
           >= lax.broadcasted_iota(jnp.int32, (n, n), 1)).astype(BF16)
    lane = lax.broadcasted_iota(jnp.int32, (n, F_PAD), 1)

    def body(ci, carry):
        rows = pl.ds(pl.multiple_of(ci * n, n), n)
        lf = jax.nn.log_sigmoid(f_ref[rows, :] + b_ref[...])
        cs = carry
        for p in _split(lf):
            cs = cs + jnp.dot(tri, p.astype(BF16), preferred_element_type=F32)
        c2 = cs * LOG2E
        for h in range(N_HEADS):
            col = jnp.broadcast_to(c2[:, h:h + 1], (n, F_PAD))
            ek_ref[h, rows, :] = _bias_features(col, lane, 0, N_SPLIT, 1.0)
            eq_ref[h, rows, :] = _bias_features(col, lane, N_SPLIT, 0, -1.0)
        return cs[n - 1:n, :]

    lax.fori_loop(0, SEQ // n, body, jnp.zeros((1, F_PAD), F32))


def _fox_prep(f, b_pad):
    feat = jax.ShapeDtypeStruct((N_HEADS, SEQ, F_PAD), BF16)
    return pl.pallas_call(
        _fox_prep_kernel,
        out_shape=[feat, feat],
        compiler_params=pltpu.CompilerParams(vmem_limit_bytes=VMEM_LIMIT),
        name="fox_prep",
    )(f, b_pad)


def _tile_iotas():
    ko = lax.broadcasted_iota(jnp.int32, (ATT_T, ATT_T), 0)
    qo = lax.broadcasted_iota(jnp.int32, (ATT_T, ATT_T), 1)
    return ko, qo


def _tile_rows(j):
    return pl.ds(pl.multiple_of(j * ATT_T, ATT_T), ATT_T)


def _scores(k_ref, ek, q_ref, eq, g, j):
    keys = jnp.concatenate([k_ref[g, _tile_rows(j), :], ek], axis=1)
    queries = jnp.concatenate([q_ref[g], eq], axis=1)
    return _nt_dot(keys, queries)


class _Stream(NamedTuple):
    s: Any
    p: Any
    alpha: Any
    m: Any
    l: Any
    acc: Any


def _stream_scratch():
    G, T = ATT_G, ATT_T
    row = pltpu.VMEM((G, 1, T), F32)
    return _Stream(s=pltpu.VMEM((G, T, T), F32), p=pltpu.VMEM((G, T, T), BF16),
                   alpha=row, m=row, l=row, acc=pltpu.VMEM((G, HEAD_DIM, T), F32))


def _fold_first(st, g, s):
    m0 = jnp.max(s, axis=0, keepdims=True)
    p = jnp.exp2(s - m0)
    st.m[g] = m0
    st.l[g] = jnp.sum(p, axis=0, keepdims=True)
    st.p[g] = p.astype(BF16)
    st.alpha[g] = jnp.zeros_like(m0)
    st.acc[g] = jnp.zeros(st.acc.shape[1:], F32)


def _fold(st, g, p, m_new):
    alpha = jnp.exp2(st.m[g] - m_new)
    st.alpha[g] = alpha
    st.l[g] = alpha * st.l[g] + jnp.sum(p, axis=0, keepdims=True)
    st.p[g] = p.astype(BF16)
    st.m[g] = m_new


def _apply(st, g, vt_tile):
    st.acc[g] = st.alpha[g] * st.acc[g] + jnp.dot(vt_tile, st.p[g], preferred_element_type=F32)


def _stream_past_tiles(st, i, vt_ref, score, fold_past, o_ref):
    chains = range(ATT_G)
    for g in chains:
        st.s[g] = score(g, 0)

    def step(r, score_next):
        prev = jnp.where(r == 1, i, r - 2)
        for g in chains:
            _apply(st, g, vt_ref[g, prev])
        for g in chains:
            fold_past(g, r - 1)
        if score_next:
            for g in chains:
                st.s[g] = score(g, r)

    def body(r, _):
        step(r, True)
        return 0
    lax.fori_loop(1, i, body, 0)

    pl.when(i >= 1)(lambda: step(i, False))

    last = jnp.maximum(i - 1, 0)
    for g in chains:
        _apply(st, g, vt_ref[g, last])
        out = st.acc[g] / st.l[g]
        o_ref[g] = out.T.astype(o_ref.dtype)


def _moba_kernel(slope_ref, q_ref, k_ref, vt_ref, o_ref,
                 kmean_ref, ek_ref, eq_ref, selb_ref, *stream_refs):
    st = _Stream(*stream_refs)
    hg = pl.program_id(0)
    i = pl.program_id(1)
    ko, qo = _tile_iotas()
    half = MOBA_BLOCK
    slope2 = [slope_ref[hg * ATT_G + g] * LOG2E for g in range(ATT_G)]

    @pl.when(i == 0)
    def _():
        off = lax.broadcasted_iota(jnp.int32, (ATT_T, HEAD_DIM), 0).astype(F32)
        lane = lax.broadcasted_iota(jnp.int32, (ATT_T, HEAD_DIM), 1)
        for g in range(ATT_G):
            def body(b, _):
                rows = pl.ds(pl.multiple_of(b * MOBA_BLOCK, MOBA_BLOCK), MOBA_BLOCK)
                kb = k_ref[g, rows, :].astype(F32)
                kmean_ref[g, pl.ds(b, 1), :] = jnp.mean(kb, axis=0, keepdims=True)
                return 0
            lax.fori_loop(0, N_BLOCKS, body, 0)
            ek_ref[g] = _bias_features(off * slope2[g], lane, 0, N_SPLIT, 1.0)
            eq_ref[g] = _bias_features(off * (-slope2[g]), lane, N_SPLIT, 0, 1.0)

    jj = lax.broadcasted_iota(jnp.int32, (N_BLOCKS, ATT_T), 0)
    lane = lax.broadcasted_iota(jnp.int32, (N_BLOCKS, ATT_T), 1)
    own = 2 * i + jnp.where(lane >= half, 1, 0)
    past = jj < own
    for g in range(ATT_G):
        q = q_ref[g]
        km = kmean_ref[g]
        km_hi = km.astype(BF16)
        km_lo = (km - km_hi.astype(F32)).astype(BF16)
        gate = _nt_dot(km_hi, q) + _nt_dot(km_lo, q)
        gm = jnp.where(past, gate, NEG_INF)
        cnt = jnp.zeros((N_BLOCKS, ATT_T), F32)
        for jp in range(N_BLOCKS):
            row = gm[jp:jp + 1, :]
            tie = jnp.where(jj > jp, 1.0, 0.0)
            cnt = cnt + jnp.where(row > gm, 1.0, jnp.where(row == gm, tie, 0.0))
        chosen = jnp.where(past, cnt, float(MOBA_TOPK)) < float(MOBA_TOPK)
        selb_ref[g] = jnp.where(chosen, 0.0, NEG_INF)

    def score(g, tile):
        return _scores(k_ref, ek_ref[g], q_ref, eq_ref[g], g, tile)

    diag = [score(g, i) for g in range(ATT_G)]
    for g in range(ATT_G):
        sel0 = selb_ref[g, pl.ds(2 * i, 1), :]
        vis0 = jnp.where(qo[:1, :] < half, 0.0, sel0)
        s = diag[g] + jnp.where(ko < half, vis0, 0.0)
        _fold_first(st, g, jnp.where(ko <= qo, s, NEG_INF))

    def fold_past(g, t):
        cj = -slope2[g] * (ATT_T * (i - t)).astype(F32)
        r0 = selb_ref[g, pl.ds(2 * t, 1), :]
        r1 = selb_ref[g, pl.ds(2 * t + 1, 1), :]
        b0, b1 = st.s[g, :half, :], st.s[g, half:, :]
        bm = jnp.maximum(jnp.max(b0, axis=0, keepdims=True) + r0,
                         jnp.max(b1, axis=0, keepdims=True) + r1) + cj
        m_new = jnp.maximum(st.m[g], bm)
        t0 = jnp.where(r0 == 0.0, m_new - cj, POS_INF)
        t1 = jnp.where(r1 == 0.0, m_new - cj, POS_INF)
        p = jnp.concatenate([jnp.exp2(b0 - t0), jnp.exp2(b1 - t1)], axis=0)
        _fold(st, g, p, m_new)

    _stream_past_tiles(st, i, vt_ref, score, fold_past, o_ref)


def _moba(slopes, z4, vt):
    G, T = ATT_G, ATT_T
    return pl.pallas_call(
        _moba_kernel,
        grid=(N_HEADS // G, N_TILES),
        in_specs=[
            pl.BlockSpec(memory_space=pltpu.SMEM),
            pl.BlockSpec((None, G, T, HEAD_DIM), lambda h, i: (Q_A_TILE, h, i, 0)),
            pl.BlockSpec((None, G, SEQ, HEAD_DIM), lambda h, i: (K_A_TILE, h, 0, 0),
                         pipeline_mode=pl.Buffered(1)),
            pl.BlockSpec((G, N_TILES, HEAD_DIM, T), lambda h, i: (h, 0, 0, 0),
                         pipeline_mode=pl.Buffered(1)),
        ],
        out_specs=pl.BlockSpec((G, T, HEAD_DIM), lambda h, i: (h, i, 0)),
        out_shape=jax.ShapeDtypeStruct((N_HEADS, SEQ, HEAD_DIM), BF16),
        scratch_shapes=[
            pltpu.VMEM((G, N_BLOCKS, HEAD_DIM), F32),
            pltpu.VMEM((G, T, HEAD_DIM), BF16),
            pltpu.VMEM((G, T, HEAD_DIM), BF16),
            pltpu.VMEM((G, N_BLOCKS, T), F32),
            *_stream_scratch(),
        ],
        compiler_params=pltpu.CompilerParams(
            dimension_semantics=("arbitrary", "arbitrary"), vmem_limit_bytes=VMEM_LIMIT),
        name="moba",
    )(slopes, z4, z4, vt)


def _fox_kernel(q_ref, eq_ref, k_ref, ek_ref, vt_ref, o_ref, *stream_refs):
    st = _Stream(*stream_refs)
    i = pl.program_id(1)
    ko, qo = _tile_iotas()

    def score(g, tile):
        return _scores(k_ref, ek_ref[g, _tile_rows(tile), :], q_ref, eq_ref[g], g, tile)

    diag = [score(g, i) for g in range(ATT_G)]
    for g in range(ATT_G):
        _fold_first(st, g, jnp.where(ko <= qo, diag[g], NEG_INF))

    def fold_past(g, tile):
        s = st.s[g]
        m_new = jnp.maximum(st.m[g], jnp.max(s, axis=0, keepdims=True))
        _fold(st, g, jnp.exp2(s - m_new), m_new)

    _stream_past_tiles(st, i, vt_ref, score, fold_past, o_ref)


def _fox(z4, vt, ek, eq):
    G, T = ATT_G, ATT_T
    resident = pl.Buffered(1)
    return pl.pallas_call(
        _fox_kernel,
        grid=(N_HEADS // G, N_TILES),
        in_specs=[
            pl.BlockSpec((None, G, T, HEAD_DIM), lambda h, i: (Q_B_TILE, h, i, 0)),
            pl.BlockSpec((G, T, F_PAD), lambda h, i: (h, i, 0)),
            pl.BlockSpec((None, G, SEQ, HEAD_DIM), lambda h, i: (K_B_TILE, h, 0, 0),
                         pipeline_mode=resident),
            pl.BlockSpec((G, SEQ, F_PAD), lambda h, i: (h, 0, 0), pipeline_mode=resident),
            pl.BlockSpec((G, N_TILES, HEAD_DIM, T), lambda h, i: (h, 0, 0, 0),
                         pipeline_mode=resident),
        ],
        out_specs=pl.BlockSpec((G, T, HEAD_DIM), lambda h, i: (h, i, 0)),
        out_shape=jax.ShapeDtypeStruct((N_HEADS, SEQ, HEAD_DIM), BF16),
        scratch_shapes=list(_stream_scratch()),
        compiler_params=pltpu.CompilerParams(
            dimension_semantics=("arbitrary", "arbitrary"), vmem_limit_bytes=VMEM_LIMIT),
        name="fox",
    )(z4, eq, z4, ek, vt)


def _mix_kernel(a_ref, b_ref, ga_ref, gb_ref, x_ref, wpa_ref, wpb_ref, wo_ref, g_ref,
                x1_ref, hm_ref):
    def heads(ref):
        return jnp.concatenate([ref[h] for h in range(N_HEADS)], axis=1)

    def gate(ref):
        cols = [ref[t, h] for t in range(D_MODEL // WIDTH) for h in range(N_HEADS)]
        return jax.nn.sigmoid(jnp.concatenate(cols, axis=1).astype(F32))

    pa = jnp.dot(heads(a_ref), wpa_ref[...], preferred_element_type=F32)
    pb = jnp.dot(heads(b_ref), wpb_ref[...], preferred_element_type=F32)
    merged = gate(ga_ref) * pa + gate(gb_ref) * pb
    x1 = x_ref[...] + jnp.dot(merged.astype(BF16), wo_ref[...], preferred_element_type=F32)
    x1_ref[...] = x1
    hm_ref[...] = _rms_rows(x1, g_ref[...]).astype(BF16)


def _mix(a, b, z4, x2, wpa, wpb, wo, g):
    tm = MIX_TM
    gate_tiles = D_MODEL // WIDTH
    const = lambda i: (0, 0)
    return pl.pallas_call(
        _mix_kernel,
        grid=(SEQ // tm,),
        in_specs=[
            pl.BlockSpec((N_HEADS, tm, HEAD_DIM), lambda i: (0, i, 0)),
            pl.BlockSpec((N_HEADS, tm, HEAD_DIM), lambda i: (0, i, 0)),
            pl.BlockSpec((gate_tiles, N_HEADS, tm, HEAD_DIM), lambda i: (3, 0, i, 0)),
            pl.BlockSpec((gate_tiles, N_HEADS, tm, HEAD_DIM), lambda i: (4, 0, i, 0)),
            pl.BlockSpec((tm, D_MODEL), lambda i: (i, 0)),
            pl.BlockSpec((WIDTH, D_MODEL), const),
            pl.BlockSpec((WIDTH, D_MODEL), const),
            pl.BlockSpec((D_MODEL, D_MODEL), const),
            pl.BlockSpec((1, D_MODEL), const),
        ],
        out_specs=[
            pl.BlockSpec((tm, D_MODEL), lambda i: (i, 0)),
            pl.BlockSpec((tm, D_MODEL), lambda i: (i, 0)),
        ],
        out_shape=[
            jax.ShapeDtypeStruct((SEQ, D_MODEL), F32),
            jax.ShapeDtypeStruct((SEQ, D_MODEL), BF16),
        ],
        compiler_params=pltpu.CompilerParams(
            dimension_semantics=("arbitrary",), vmem_limit_bytes=VMEM_LIMIT),
        name="mix",
    )(a, b, z4, z4, x2, wpa, wpb, wo, g)


def _mlp_up(hm, wu):
    h = jnp.dot(hm, wu, preferred_element_type=F32)
    return jnp.square(jnp.maximum(h, 0.0)).astype(BF16)


def _mlp_head_kernel(hm_ref, wu_ref, h_ref):
    h_ref[...] = _mlp_up(hm_ref[...], wu_ref[...])


def _mlp_kernel(h0_ref, hm_next_ref, x1_ref, wu_next_ref, wd_ref, g_ref, o_ref, acc_ref, h_ref):
    i, c = pl.program_id(0), pl.program_id(1)

    @pl.when((i == 0) & (c == 0))
    def _():
        h_ref[...] = h0_ref[...]

    @pl.when(c == 0)
    def _():
        acc_ref[...] = x1_ref[...]

    acc_ref[...] += jnp.dot(h_ref[...], wd_ref[...], preferred_element_type=F32)
    h_ref[...] = _mlp_up(hm_next_ref[...], wu_next_ref[...])

    @pl.when(c == pl.num_programs(1) - 1)
    def _():
        o_ref[...] = _rms_rows(acc_ref[...], g_ref[...])


def _mlp(hm, x1, wu, wd, g):
    tm, fc = MLP_TM, MLP_FC
    n_i, n_c = SEQ // tm, D_FF // fc
    params = dict(vmem_limit_bytes=VMEM_LIMIT)
    h0 = pl.pallas_call(
        _mlp_head_kernel,
        grid=(1,),
        in_specs=[pl.BlockSpec((tm, D_MODEL), lambda s: (0, 0)),
                  pl.BlockSpec((D_MODEL, fc), lambda s: (0, 0))],
        out_specs=pl.BlockSpec((tm, fc), lambda s: (0, 0)),
        out_shape=jax.ShapeDtypeStruct((tm, fc), BF16),
        compiler_params=pltpu.CompilerParams(dimension_semantics=("arbitrary",), **params),
        name="mlp_head",
    )(hm, wu)

    def next_rows(i, c):
        return jnp.minimum(i + (c + 1) // n_c, n_i - 1)

    return pl.pallas_call(
        _mlp_kernel,
        grid=(n_i, n_c),
        in_specs=[
            pl.BlockSpec((tm, fc), lambda i, c: (0, 0)),
            pl.BlockSpec((tm, D_MODEL), lambda i, c: (next_rows(i, c), 0)),
            pl.BlockSpec((tm, D_MODEL), lambda i, c: (i, 0)),
            pl.BlockSpec((D_MODEL, fc), lambda i, c: (0, (c + 1) % n_c)),
            pl.BlockSpec((fc, D_MODEL), lambda i, c: (c, 0)),
            pl.BlockSpec((1, D_MODEL), lambda i, c: (0, 0)),
        ],
        out_specs=pl.BlockSpec((tm, D_MODEL), lambda i, c: (i, 0)),
        out_shape=jax.ShapeDtypeStruct((SEQ, D_MODEL), F32),
        scratch_shapes=[pltpu.VMEM((tm, D_MODEL), F32), pltpu.VMEM((tm, fc), BF16)],
        compiler_params=pltpu.CompilerParams(
            dimension_semantics=("arbitrary", "arbitrary"), **params),
        name="mlp",
    )(h0, hm, x1, wu, wd, g)


def _tiles_transposed(v):
    return v.reshape(N_HEADS, N_TILES, ATT_T, HEAD_DIM).transpose(0, 1, 3, 2)


def kernel(x, norm_mix_g, w_in, b_forget, w_proj_a, w_proj_b, w_out, norm_mlp_g, w_up, w_down,
           norm_final_g):
    f_lo = 6 * WIDTH
    f_hi = f_lo + N_HEADS
    w_main = jnp.concatenate([w_in[:, :f_lo], w_in[:, f_hi:]], axis=1).astype(BF16)
    w_f = jnp.pad(w_in[:, f_lo:f_hi], ((0, 0), (0, F_PAD - N_HEADS))).astype(BF16)
    row = lambda g: g.reshape(1, D_MODEL).astype(F32)

    x2 = x.reshape(SEQ, D_MODEL)
    z4, f = _inproj(x2, row(norm_mix_g), w_main, w_f)

    b_pad = jnp.pad(b_forget.astype(F32), (0, F_PAD - N_HEADS)).reshape(1, F_PAD)
    ek, eq = _fox_prep(f, b_pad)

    slopes = jnp.exp2(-8.0 * jnp.arange(1, N_HEADS + 1, dtype=F32) / N_HEADS)
    a = _moba(slopes, z4, _tiles_transposed(z4[V_A_TILE]))
    b = _fox(z4, _tiles_transposed(z4[V_B_TILE]), ek, eq)

    x1, hm = _mix(a, b, z4, x2, w_proj_a.astype(BF16), w_proj_b.astype(BF16),
                  w_out.astype(BF16), row(norm_mlp_g))
    out = _mlp(hm, x1, w_up.astype(BF16), w_down.astype(BF16), row(norm_final_g))
    return out.reshape(x.shape)
```

```python
import math
from typing import Any, NamedTuple

import jax
import jax.numpy as jnp
from jax import lax
from jax.experimental import pallas as pl
from jax.experimental.pallas import tpu as pltpu

D_MODEL = 2048
SEQ = 8192
HEAD_DIM = 128
N_HEADS = 8
WIDTH = N_HEADS * HEAD_DIM
MOBA_BLOCK = 256
MOBA_TOPK = 3
N_BLOCKS = SEQ // MOBA_BLOCK
D_FF = 4 * D_MODEL
RMS_EPS = 1e-6
LOG2E = math.log2(math.e)
SCALE2 = HEAD_DIM ** -0.5 * LOG2E
N_MAIN = 6 * WIDTH + 2 * D_MODEL
F_PAD = 128

BF16 = jnp.bfloat16
F32 = jnp.float32
NEG_INF = float("-inf")
POS_INF = float("inf")

VMEM_LIMIT = 56 * 1024 * 1024

IN_TM, IN_TN = 1024, 1024
Q_A_TILE, K_A_TILE, V_A_TILE, Q_B_TILE, K_B_TILE, V_B_TILE = range(6)
ATT_T = 2 * MOBA_BLOCK
ATT_G = 4
N_SPLIT = 3
N_TILES = SEQ // ATT_T
MIX_TM = 512
MLP_TM, MLP_FC = 512, 1024
ROW_CHUNK = 128


def _nt_dot(a, b):
    return lax.dot_general(a, b, (((1,), (1,)), ((), ())), preferred_element_type=F32)


def _rms_rows(x, g):
    ms = jnp.mean(x * x, axis=-1, keepdims=True)
    return x * lax.rsqrt(ms + RMS_EPS) * g


def _inproj_kernel(x_ref, g_ref, w_ref, wf_ref, z_ref, f_ref, vta_ref, vtb_ref, hn_ref):
    j = pl.program_id(1)

    @pl.when(j == 0)
    def _():
        def body(r, _):
            rows = pl.ds(pl.multiple_of(r * ROW_CHUNK, ROW_CHUNK), ROW_CHUNK)
            hn = _rms_rows(x_ref[rows, :], g_ref[...]).astype(BF16)
            hn_ref[rows, :] = hn
            f_ref[rows, :] = jnp.dot(hn, wf_ref[...], preferred_element_type=F32)
            return 0
        lax.fori_loop(0, IN_TM // ROW_CHUNK, body, 0)

    scale = jnp.where((j == Q_A_TILE) | (j == Q_B_TILE), SCALE2, 1.0)
    zf = jnp.dot(hn_ref[...], w_ref[...], preferred_element_type=F32) * scale
    z = zf.astype(BF16)
    for h in range(IN_TN // HEAD_DIM):
        z_ref[h] = z[:, h * HEAD_DIM:(h + 1) * HEAD_DIM]

    def store_transposed(vt_ref):
        for h in range(N_HEADS):
            for t in range(IN_TM // ATT_T):
                blk = zf[t * ATT_T:(t + 1) * ATT_T, h * HEAD_DIM:(h + 1) * HEAD_DIM]
                vt_ref[h, t] = blk.T.astype(BF16)

    pl.when(j == V_A_TILE)(lambda: store_transposed(vta_ref))
    pl.when(j == V_B_TILE)(lambda: store_transposed(vtb_ref))


def _inproj(x2, g, w_main, w_f):
    n_i, n_j = SEQ // IN_TM, N_MAIN // IN_TN
    slabs = IN_TN // HEAD_DIM
    vt_shape = jax.ShapeDtypeStruct((N_HEADS, N_TILES, HEAD_DIM, ATT_T), BF16)
    vt_spec = pl.BlockSpec((N_HEADS, IN_TM // ATT_T, HEAD_DIM, ATT_T), lambda i, j: (0, i, 0, 0))
    return pl.pallas_call(
        _inproj_kernel,
        grid=(n_i, n_j),
        in_specs=[
            pl.BlockSpec((IN_TM, D_MODEL), lambda i, j: (i, 0)),
            pl.BlockSpec((1, D_MODEL), lambda i, j: (0, 0)),
            pl.BlockSpec((D_MODEL, IN_TN), lambda i, j: (0, j)),
            pl.BlockSpec((D_MODEL, F_PAD), lambda i, j: (0, 0)),
        ],
        out_specs=[
            pl.BlockSpec((None, slabs, IN_TM, HEAD_DIM), lambda i, j: (j, 0, i, 0)),
            pl.BlockSpec((IN_TM, F_PAD), lambda i, j: (i, 0)),
            vt_spec,
            vt_spec,
        ],
        out_shape=[
            jax.ShapeDtypeStruct((n_j, slabs, SEQ, HEAD_DIM), BF16),
            jax.ShapeDtypeStruct((SEQ, F_PAD), F32),
            vt_shape,
            vt_shape,
        ],
        scratch_shapes=[pltpu.VMEM((IN_TM, D_MODEL), BF16)],
        compiler_params=pltpu.CompilerParams(
            dimension_semantics=("arbitrary", "arbitrary"), vmem_limit_bytes=VMEM_LIMIT),
        name="inproj",
    )(x2, g, w_main, w_f)


def _split(v):
    pieces = []
    for _ in range(N_SPLIT):
        p = v.astype(BF16).astype(F32)
        pieces.append(p)
        v = v - p
    return pieces


def _bias_features(v, lane, pieces_at, const_at, const):
    f = jnp.zeros_like(v)
    for n, p in enumerate(_split(v)):
        f = jnp.where(lane == pieces_at + n, p, f)
    f = jnp.where((lane >= const_at) & (lane < const_at + N_SPLIT), const, f)
    return f.astype(BF16)


def _fox_prep_kernel(f_ref, b_ref, ek_ref, eq_ref):
    n = ROW_CHUNK
    tri = (lax.broadcasted_iota(jnp.int32, (n, n), 0)
           >= lax.broadcasted_iota(jnp.int32, (n, n), 1)).astype(BF16)
    lane = lax.broadcasted_iota(jnp.int32, (n, F_PAD), 1)

    def body(ci, carry):
        rows = pl.ds(pl.multiple_of(ci * n, n), n)
        lf = jax.nn.log_sigmoid(f_ref[rows, :] + b_ref[...])
        cs = carry
        for p in _split(lf):
            cs = cs + jnp.dot(tri, p.astype(BF16), preferred_element_type=F32)
        c2 = cs * LOG2E
        for h in range(N_HEADS):
            col = jnp.broadcast_to(c2[:, h:h + 1], (n, F_PAD))
            ek_ref[h, rows, :] = _bias_features(col, lane, 0, N_SPLIT, 1.0)
            eq_ref[h, rows, :] = _bias_features(col, lane, N_SPLIT, 0, -1.0)
        return cs[n - 1:n, :]

    lax.fori_loop(0, SEQ // n, body, jnp.zeros((1, F_PAD), F32))


def _fox_prep(f, b_pad):
    feat = jax.ShapeDtypeStruct((N_HEADS, SEQ, F_PAD), BF16)
    return pl.pallas_call(
        _fox_prep_kernel,
        out_shape=[feat, feat],
        compiler_params=pltpu.CompilerParams(vmem_limit_bytes=VMEM_LIMIT),
        name="fox_prep",
    )(f, b_pad)


def _tile_iotas():
    ko = lax.broadcasted_iota(jnp.int32, (ATT_T, ATT_T), 0)
    qo = lax.broadcasted_iota(jnp.int32, (ATT_T, ATT_T), 1)
    return ko, qo


def _tile_rows(j):
    return pl.ds(pl.multiple_of(j * ATT_T, ATT_T), ATT_T)


def _scores(k_ref, ek, q_ref, eq, g, j):
    keys = jnp.concatenate([k_ref[g, _tile_rows(j), :], ek], axis=1)
    queries = jnp.concatenate([q_ref[g], eq], axis=1)
    return _nt_dot(keys, queries)


class _Stream(NamedTuple):
    s: Any
    p: Any
    alpha: Any
    m: Any
    l: Any
    acc: Any


def _stream_scratch():
    G, T = ATT_G, ATT_T
    row = pltpu.VMEM((G, 1, T), F32)
    return _Stream(s=pltpu.VMEM((G, T, T), F32), p=pltpu.VMEM((G, T, T), BF16),
                   alpha=row, m=row, l=row, acc=pltpu.VMEM((G, HEAD_DIM, T), F32))


def _fold_first(st, g, s):
    m0 = jnp.max(s, axis=0, keepdims=True)
    p = jnp.exp2(s - m0)
    st.m[g] = m0
    st.l[g] = jnp.sum(p, axis=0, keepdims=True)
    st.p[g] = p.astype(BF16)
    st.alpha[g] = jnp.zeros_like(m0)
    st.acc[g] = jnp.zeros(st.acc.shape[1:], F32)


def _fold(st, g, p, m_new):
    alpha = jnp.exp2(st.m[g] - m_new)
    st.alpha[g] = alpha
    st.l[g] = alpha * st.l[g] + jnp.sum(p, axis=0, keepdims=True)
    st.p[g] = p.astype(BF16)
    st.m[g] = m_new


def _apply(st, g, vt_tile):
    st.acc[g] = st.alpha[g] * st.acc[g] + jnp.dot(vt_tile, st.p[g], preferred_element_type=F32)


def _stream_past_tiles(st, i, vt_ref, score, fold_past, o_ref):
    chains = range(ATT_G)
    for g in chains:
        st.s[g] = score(g, 0)

    def step(r, score_next):
        prev = jnp.where(r == 1, i, r - 2)
        for g in chains:
            _apply(st, g, vt_ref[g, prev])
        for g in chains:
            fold_past(g, r - 1)
        if score_next:
            for g in chains:
                st.s[g] = score(g, r)

    def body(r, _):
        step(r, True)
        return 0
    lax.fori_loop(1, i, body, 0)

    pl.when(i >= 1)(lambda: step(i, False))

    last = jnp.maximum(i - 1, 0)
    for g in chains:
        _apply(st, g, vt_ref[g, last])
        out = st.acc[g] / st.l[g]
        o_ref[g] = out.T.astype(o_ref.dtype)


def _moba_kernel(slope_ref, q_ref, k_ref, vt_ref, w32_ref, o_ref, w16_ref,
                 kmean_ref, ek_ref, eq_ref, selb_ref, *stream_refs):
    st = _Stream(*stream_refs)
    hg = pl.program_id(0)
    i = pl.program_id(1)
    w16_ref[...] = w32_ref[...].astype(BF16)
    ko, qo = _tile_iotas()
    half = MOBA_BLOCK
    slope2 = [slope_ref[hg * ATT_G + g] * LOG2E for g in range(ATT_G)]

    @pl.when(i == 0)
    def _():
        off = lax.broadcasted_iota(jnp.int32, (ATT_T, HEAD_DIM), 0).astype(F32)
        lane = lax.broadcasted_iota(jnp.int32, (ATT_T, HEAD_DIM), 1)
        for g in range(ATT_G):
            def body(b, _):
                rows = pl.ds(pl.multiple_of(b * MOBA_BLOCK, MOBA_BLOCK), MOBA_BLOCK)
                kb = k_ref[g, rows, :].astype(F32)
                kmean_ref[g, pl.ds(b, 1), :] = jnp.mean(kb, axis=0, keepdims=True)
                return 0
            lax.fori_loop(0, N_BLOCKS, body, 0)
            ek_ref[g] = _bias_features(off * slope2[g], lane, 0, N_SPLIT, 1.0)
            eq_ref[g] = _bias_features(off * (-slope2[g]), lane, N_SPLIT, 0, 1.0)

    jj = lax.broadcasted_iota(jnp.int32, (N_BLOCKS, ATT_T), 0)
    lane = lax.broadcasted_iota(jnp.int32, (N_BLOCKS, ATT_T), 1)
    own = 2 * i + jnp.where(lane >= half, 1, 0)
    past = jj < own
    for g in range(ATT_G):
        q = q_ref[g]
        km = kmean_ref[g]
        km_hi = km.astype(BF16)
        km_lo = (km - km_hi.astype(F32)).astype(BF16)
        gate = _nt_dot(km_hi, q) + _nt_dot(km_lo, q)
        gm = jnp.where(past, gate, NEG_INF)
        cnt = jnp.zeros((N_BLOCKS, ATT_T), F32)
        for jp in range(N_BLOCKS):
            row = gm[jp:jp + 1, :]
            tie = jnp.where(jj > jp, 1.0, 0.0)
            cnt = cnt + jnp.where(row > gm, 1.0, jnp.where(row == gm, tie, 0.0))
        chosen = jnp.where(past, cnt, float(MOBA_TOPK)) < float(MOBA_TOPK)
        selb_ref[g] = jnp.where(chosen, 0.0, NEG_INF)

    def score(g, tile):
        return _scores(k_ref, ek_ref[g], q_ref, eq_ref[g], g, tile)

    diag = [score(g, i) for g in range(ATT_G)]
    for g in range(ATT_G):
        sel0 = selb_ref[g, pl.ds(2 * i, 1), :]
        vis0 = jnp.where(qo[:1, :] < half, 0.0, sel0)
        s = diag[g] + jnp.where(ko < half, vis0, 0.0)
        _fold_first(st, g, jnp.where(ko <= qo, s, NEG_INF))

    def fold_past(g, t):
        cj = -slope2[g] * (ATT_T * (i - t)).astype(F32)
        r0 = selb_ref[g, pl.ds(2 * t, 1), :]
        r1 = selb_ref[g, pl.ds(2 * t + 1, 1), :]
        b0, b1 = st.s[g, :half, :], st.s[g, half:, :]
        bm = jnp.maximum(jnp.max(b0, axis=0, keepdims=True) + r0,
                         jnp.max(b1, axis=0, keepdims=True) + r1) + cj
        m_new = jnp.maximum(st.m[g], bm)
        t0 = jnp.where(r0 == 0.0, m_new - cj, POS_INF)
        t1 = jnp.where(r1 == 0.0, m_new - cj, POS_INF)
        p = jnp.concatenate([jnp.exp2(b0 - t0), jnp.exp2(b1 - t1)], axis=0)
        _fold(st, g, p, m_new)

    _stream_past_tiles(st, i, vt_ref, score, fold_past, o_ref)


def _cast_slab_spec(w):
    steps = (N_HEADS // ATT_G) * N_TILES
    slab = (w.shape[0] // steps, w.shape[1])
    return pl.BlockSpec(slab, lambda h, i: (h * N_TILES + i, 0)), jax.ShapeDtypeStruct(w.shape, BF16)


def _moba(slopes, z4, vt, w32):
    G, T = ATT_G, ATT_T
    w_spec, w_shape = _cast_slab_spec(w32)
    return pl.pallas_call(
        _moba_kernel,
        grid=(N_HEADS // G, N_TILES),
        in_specs=[
            pl.BlockSpec(memory_space=pltpu.SMEM),
            pl.BlockSpec((None, G, T, HEAD_DIM), lambda h, i: (Q_A_TILE, h, i, 0)),
            pl.BlockSpec((None, G, SEQ, HEAD_DIM), lambda h, i: (K_A_TILE, h, 0, 0),
                         pipeline_mode=pl.Buffered(1)),
            pl.BlockSpec((G, N_TILES, HEAD_DIM, T), lambda h, i: (h, 0, 0, 0),
                         pipeline_mode=pl.Buffered(1)),
            w_spec,
        ],
        out_specs=[pl.BlockSpec((G, T, HEAD_DIM), lambda h, i: (h, i, 0)), w_spec],
        out_shape=[jax.ShapeDtypeStruct((N_HEADS, SEQ, HEAD_DIM), BF16), w_shape],
        scratch_shapes=[
            pltpu.VMEM((G, N_BLOCKS, HEAD_DIM), F32),
            pltpu.VMEM((G, T, HEAD_DIM), BF16),
            pltpu.VMEM((G, T, HEAD_DIM), BF16),
            pltpu.VMEM((G, N_BLOCKS, T), F32),
            *_stream_scratch(),
        ],
        compiler_params=pltpu.CompilerParams(
            dimension_semantics=("arbitrary", "arbitrary"), vmem_limit_bytes=VMEM_LIMIT),
        name="moba",
    )(slopes, z4, z4, vt, w32)


def _fox_kernel(q_ref, eq_ref, k_ref, ek_ref, vt_ref, w32_ref, o_ref, w16_ref, *stream_refs):
    st = _Stream(*stream_refs)
    i = pl.program_id(1)
    w16_ref[...] = w32_ref[...].astype(BF16)
    ko, qo = _tile_iotas()

    def score(g, tile):
        return _scores(k_ref, ek_ref[g, _tile_rows(tile), :], q_ref, eq_ref[g], g, tile)

    diag = [score(g, i) for g in range(ATT_G)]
    for g in range(ATT_G):
        _fold_first(st, g, jnp.where(ko <= qo, diag[g], NEG_INF))

    def fold_past(g, tile):
        s = st.s[g]
        m_new = jnp.maximum(st.m[g], jnp.max(s, axis=0, keepdims=True))
        _fold(st, g, jnp.exp2(s - m_new), m_new)

    _stream_past_tiles(st, i, vt_ref, score, fold_past, o_ref)


def _fox(z4, vt, ek, eq, w32):
    G, T = ATT_G, ATT_T
    resident = pl.Buffered(1)
    w_spec, w_shape = _cast_slab_spec(w32)
    return pl.pallas_call(
        _fox_kernel,
        grid=(N_HEADS // G, N_TILES),
        in_specs=[
            pl.BlockSpec((None, G, T, HEAD_DIM), lambda h, i: (Q_B_TILE, h, i, 0)),
            pl.BlockSpec((G, T, F_PAD), lambda h, i: (h, i, 0)),
            pl.BlockSpec((None, G, SEQ, HEAD_DIM), lambda h, i: (K_B_TILE, h, 0, 0),
                         pipeline_mode=resident),
            pl.BlockSpec((G, SEQ, F_PAD), lambda h, i: (h, 0, 0), pipeline_mode=resident),
            pl.BlockSpec((G, N_TILES, HEAD_DIM, T), lambda h, i: (h, 0, 0, 0),
                         pipeline_mode=resident),
            w_spec,
        ],
        out_specs=[pl.BlockSpec((G, T, HEAD_DIM), lambda h, i: (h, i, 0)), w_spec],
        out_shape=[jax.ShapeDtypeStruct((N_HEADS, SEQ, HEAD_DIM), BF16), w_shape],
        scratch_shapes=list(_stream_scratch()),
        compiler_params=pltpu.CompilerParams(
            dimension_semantics=("arbitrary", "arbitrary"), vmem_limit_bytes=VMEM_LIMIT),
        name="fox",
    )(z4, eq, z4, ek, vt, w32)


def _mix_kernel(a_ref, b_ref, ga_ref, gb_ref, x_ref, wpa_ref, wpb_ref, wo_ref, g_ref,
                x1_ref, hm_ref):
    def heads(ref):
        return jnp.concatenate([ref[h] for h in range(N_HEADS)], axis=1)

    def gate(ref):
        cols = [ref[t, h] for t in range(D_MODEL // WIDTH) for h in range(N_HEADS)]
        return jax.nn.sigmoid(jnp.concatenate(cols, axis=1).astype(F32))

    pa = jnp.dot(heads(a_ref), wpa_ref[...], preferred_element_type=F32)
    pb = jnp.dot(heads(b_ref), wpb_ref[...], preferred_element_type=F32)
    merged = gate(ga_ref) * pa + gate(gb_ref) * pb
    x1 = x_ref[...] + jnp.dot(merged.astype(BF16), wo_ref[...], preferred_element_type=F32)
    x1_ref[...] = x1
    hm_ref[...] = _rms_rows(x1, g_ref[...]).astype(BF16)


def _mix(a, b, z4, x2, wpa, wpb, wo, g):
    tm = MIX_TM
    gate_tiles = D_MODEL // WIDTH
    const = lambda i: (0, 0)
    return pl.pallas_call(
        _mix_kernel,
        grid=(SEQ // tm,),
        in_specs=[
            pl.BlockSpec((N_HEADS, tm, HEAD_DIM), lambda i: (0, i, 0)),
            pl.BlockSpec((N_HEADS, tm, HEAD_DIM), lambda i: (0, i, 0)),
            pl.BlockSpec((gate_tiles, N_HEADS, tm, HEAD_DIM), lambda i: (3, 0, i, 0)),
            pl.BlockSpec((gate_tiles, N_HEADS, tm, HEAD_DIM), lambda i: (4, 0, i, 0)),
            pl.BlockSpec((tm, D_MODEL), lambda i: (i, 0)),
            pl.BlockSpec((WIDTH, D_MODEL), const),
            pl.BlockSpec((WIDTH, D_MODEL), const),
            pl.BlockSpec((D_MODEL, D_MODEL), const),
            pl.BlockSpec((1, D_MODEL), const),
        ],
        out_specs=[
            pl.BlockSpec((tm, D_MODEL), lambda i: (i, 0)),
            pl.BlockSpec((tm, D_MODEL), lambda i: (i, 0)),
        ],
        out_shape=[
            jax.ShapeDtypeStruct((SEQ, D_MODEL), F32),
            jax.ShapeDtypeStruct((SEQ, D_MODEL), BF16),
        ],
        compiler_params=pltpu.CompilerParams(
            dimension_semantics=("arbitrary",), vmem_limit_bytes=VMEM_LIMIT),
        name="mix",
    )(a, b, z4, z4, x2, wpa, wpb, wo, g)


def _mlp_up(hm, wu):
    h = jnp.dot(hm, wu, preferred_element_type=F32)
    return jnp.square(jnp.maximum(h, 0.0)).astype(BF16)


def _mlp_head_kernel(hm_ref, wu_ref, h_ref):
    h_ref[...] = _mlp_up(hm_ref[...], wu_ref[...])


def _mlp_kernel(h0_ref, hm_next_ref, x1_ref, wu_next_ref, wd_ref, g_ref, o_ref, acc_ref, h_ref):
    i, c = pl.program_id(0), pl.program_id(1)

    @pl.when((i == 0) & (c == 0))
    def _():
        h_ref[...] = h0_ref[...]

    @pl.when(c == 0)
    def _():
        acc_ref[...] = x1_ref[...]

    acc_ref[...] += jnp.dot(h_ref[...], wd_ref[...], preferred_element_type=F32)
    h_ref[...] = _mlp_up(hm_next_ref[...], wu_next_ref[...])

    @pl.when(c == pl.num_programs(1) - 1)
    def _():
        o_ref[...] = _rms_rows(acc_ref[...], g_ref[...])


def _mlp(hm, x1, wu, wd, g):
    tm, fc = MLP_TM, MLP_FC
    n_i, n_c = SEQ // tm, D_FF // fc
    params = dict(vmem_limit_bytes=VMEM_LIMIT)
    h0 = pl.pallas_call(
        _mlp_head_kernel,
        grid=(1,),
        in_specs=[pl.BlockSpec((tm, D_MODEL), lambda s: (0, 0)),
                  pl.BlockSpec((D_MODEL, fc), lambda s: (0, 0))],
        out_specs=pl.BlockSpec((tm, fc), lambda s: (0, 0)),
        out_shape=jax.ShapeDtypeStruct((tm, fc), BF16),
        compiler_params=pltpu.CompilerParams(dimension_semantics=("arbitrary",), **params),
        name="mlp_head",
    )(hm, wu)

    def next_rows(i, c):
        return jnp.minimum(i + (c + 1) // n_c, n_i - 1)

    return pl.pallas_call(
        _mlp_kernel,
        grid=(n_i, n_c),
        in_specs=[
            pl.BlockSpec((tm, fc), lambda i, c: (0, 0)),
            pl.BlockSpec((tm, D_MODEL), lambda i, c: (next_rows(i, c), 0)),
            pl.BlockSpec((tm, D_MODEL), lambda i, c: (i, 0)),
            pl.BlockSpec((D_MODEL, fc), lambda i, c: (0, (c + 1) % n_c)),
            pl.BlockSpec((fc, D_MODEL), lambda i, c: (c, 0)),
            pl.BlockSpec((1, D_MODEL), lambda i, c: (0, 0)),
        ],
        out_specs=pl.BlockSpec((tm, D_MODEL), lambda i, c: (i, 0)),
        out_shape=jax.ShapeDtypeStruct((SEQ, D_MODEL), F32),
        scratch_shapes=[pltpu.VMEM((tm, D_MODEL), F32), pltpu.VMEM((tm, fc), BF16)],
        compiler_params=pltpu.CompilerParams(
            dimension_semantics=("arbitrary", "arbitrary"), **params),
        name="mlp",
    )(h0, hm, x1, wu, wd, g)


def kernel(x, norm_mix_g, w_in, b_forget, w_proj_a, w_proj_b, w_out, norm_mlp_g, w_up, w_down,
           norm_final_g):
    f_lo = 6 * WIDTH
    f_hi = f_lo + N_HEADS
    w_main = jnp.concatenate([w_in[:, :f_lo], w_in[:, f_hi:]], axis=1).astype(BF16)
    w_f = jnp.pad(w_in[:, f_lo:f_hi], ((0, 0), (0, F_PAD - N_HEADS))).astype(BF16)
    row = lambda g: g.reshape(1, D_MODEL).astype(F32)

    x2 = x.reshape(SEQ, D_MODEL)
    z4, f, vt_a, vt_b = _inproj(x2, row(norm_mix_g), w_main, w_f)

    b_pad = jnp.pad(b_forget.astype(F32), (0, F_PAD - N_HEADS)).reshape(1, F_PAD)
    ek, eq = _fox_prep(f, b_pad)

    slopes = jnp.exp2(-8.0 * jnp.arange(1, N_HEADS + 1, dtype=F32) / N_HEADS)
    a, w_up16 = _moba(slopes, z4, vt_a, w_up)
    b, w_down16 = _fox(z4, vt_b, ek, eq, w_down)

    x1, hm = _mix(a, b, z4, x2, w_proj_a.astype(BF16), w_proj_b.astype(BF16),
                  w_out.astype(BF16), row(norm_mlp_g))
    out = _mlp(hm, x1, w_up16, w_down16, row(norm_final_g))
    return out.reshape(x.shape)
```

```python
import functools
import math
from typing import Any, NamedTuple

import jax
import jax.numpy as jnp
from jax import lax
from jax.experimental import pallas as pl
from jax.experimental.pallas import tpu as pltpu

D_MODEL = 2048
SEQ = 8192
HEAD_DIM = 128
N_HEADS = 8
WIDTH = N_HEADS * HEAD_DIM
MOBA_BLOCK = 256
MOBA_TOPK = 3
N_BLOCKS = SEQ // MOBA_BLOCK
D_FF = 4 * D_MODEL
RMS_EPS = 1e-6
LOG2E = math.log2(math.e)
SCALE2 = HEAD_DIM ** -0.5 * LOG2E
F_PAD = 128

BF16 = jnp.bfloat16
F32 = jnp.float32
NEG_INF = float("-inf")
POS_INF = float("inf")

VMEM_LIMIT = 56 * 1024 * 1024

IN_TM, IN_TN = 1024, 1024
Q_A_TILE, K_A_TILE, V_A_TILE, Q_B_TILE, K_B_TILE, V_B_TILE = range(6)
N_QKV_TILES = 6 * WIDTH // IN_TN
N_GATE_TILES = 2 * D_MODEL // IN_TN
ATT_T = 2 * MOBA_BLOCK
ATT_G = 4
N_SPLIT = 3
N_TILES = SEQ // ATT_T
MIX_TM = 512
MLP_TM, MLP_FC = 512, 1024
ROW_CHUNK = 128


def _nt_dot(a, b):
    return lax.dot_general(a, b, (((1,), (1,)), ((), ())), preferred_element_type=F32)


def _rms_rows(x, g):
    ms = jnp.mean(x * x, axis=-1, keepdims=True)
    return x * lax.rsqrt(ms + RMS_EPS) * g


def _inproj_kernel(x_ref, g_ref, wqkv_ref, wg_ref, wf_ref, z_ref, f_ref, vt_ref, hn_ref):
    j = pl.program_id(1)

    @pl.when(j == 0)
    def _():
        def body(r, _):
            rows = pl.ds(pl.multiple_of(r * ROW_CHUNK, ROW_CHUNK), ROW_CHUNK)
            hn = _rms_rows(x_ref[rows, :], g_ref[...]).astype(BF16)
            hn_ref[rows, :] = hn
            f_ref[rows, :] = jnp.dot(hn, wf_ref[...], preferred_element_type=F32)
            return 0
        lax.fori_loop(0, IN_TM // ROW_CHUNK, body, 0)

    def project(w_ref, scale):
        zf = jnp.dot(hn_ref[...], w_ref[...], preferred_element_type=F32)
        if scale is not None:
            zf = zf * scale
        z = zf.astype(BF16)
        for h in range(IN_TN // HEAD_DIM):
            z_ref[h] = z[:, h * HEAD_DIM:(h + 1) * HEAD_DIM]
        return zf

    @pl.when(j < N_QKV_TILES)
    def _():
        zf = project(wqkv_ref, jnp.where((j == Q_A_TILE) | (j == Q_B_TILE), SCALE2, 1.0))

        @pl.when((j == V_A_TILE) | (j == V_B_TILE))
        def _():
            for h in range(N_HEADS):
                for t in range(IN_TM // ATT_T):
                    blk = zf[t * ATT_T:(t + 1) * ATT_T, h * HEAD_DIM:(h + 1) * HEAD_DIM]
                    vt_ref[h, t] = blk.T.astype(BF16)

    @pl.when(j >= N_QKV_TILES)
    def _():
        project(wg_ref, None)


def _inproj(x2, g, w_qkv, w_gates, w_f):
    n_i = SEQ // IN_TM
    n_j = N_QKV_TILES + N_GATE_TILES
    slabs = IN_TN // HEAD_DIM
    tiles = IN_TM // ATT_T
    return pl.pallas_call(
        _inproj_kernel,
        grid=(n_i, n_j),
        in_specs=[
            pl.BlockSpec((IN_TM, D_MODEL), lambda i, j: (i, 0)),
            pl.BlockSpec((1, D_MODEL), lambda i, j: (0, 0)),
            pl.BlockSpec((D_MODEL, IN_TN), lambda i, j: (0, jnp.minimum(j, N_QKV_TILES - 1))),
            pl.BlockSpec((D_MODEL, IN_TN), lambda i, j: (0, jnp.maximum(j - N_QKV_TILES, 0))),
            pl.BlockSpec((D_MODEL, F_PAD), lambda i, j: (0, 0)),
        ],
        out_specs=[
            pl.BlockSpec((None, slabs, IN_TM, HEAD_DIM), lambda i, j: (j, 0, i, 0)),
            pl.BlockSpec((IN_TM, F_PAD), lambda i, j: (i, 0)),
            pl.BlockSpec((None, N_HEADS, tiles, HEAD_DIM, ATT_T),
                         lambda i, j: (jnp.where(j > V_A_TILE, 1, 0), 0, i, 0, 0)),
        ],
        out_shape=[
            jax.ShapeDtypeStruct((n_j, slabs, SEQ, HEAD_DIM), BF16),
            jax.ShapeDtypeStruct((SEQ, F_PAD), F32),
            jax.ShapeDtypeStruct((2, N_HEADS, N_TILES, HEAD_DIM, ATT_T), BF16),
        ],
        scratch_shapes=[pltpu.VMEM((IN_TM, D_MODEL), BF16)],
        compiler_params=pltpu.CompilerParams(
            dimension_semantics=("arbitrary", "arbitrary"), vmem_limit_bytes=VMEM_LIMIT),
        name="inproj",
    )(x2, g, w_qkv, w_gates, w_f)


def _split(v):
    pieces = []
    for _ in range(N_SPLIT):
        p = v.astype(BF16).astype(F32)
        pieces.append(p)
        v = v - p
    return pieces


def _bias_features(v, lane, pieces_at, const_at, const):
    f = jnp.zeros_like(v)
    for n, p in enumerate(_split(v)):
        f = jnp.where(lane == pieces_at + n, p, f)
    f = jnp.where((lane >= const_at) & (lane < const_at + N_SPLIT), const, f)
    return f.astype(BF16)


def _fox_prep_kernel(f_ref, b_ref, ek_ref, eq_ref):
    n = ROW_CHUNK
    tri = (lax.broadcasted_iota(jnp.int32, (n, n), 0)
           >= lax.broadcasted_iota(jnp.int32, (n, n), 1)).astype(BF16)
    lane = lax.broadcasted_iota(jnp.int32, (n, F_PAD), 1)

    def body(ci, carry):
        rows = pl.ds(pl.multiple_of(ci * n, n), n)
        lf = jax.nn.log_sigmoid(f_ref[rows, :] + b_ref[...])
        cs = carry
        for p in _split(lf):
            cs = cs + jnp.dot(tri, p.astype(BF16), preferred_element_type=F32)
        c2 = cs * LOG2E
        for h in range(N_HEADS):
            col = jnp.broadcast_to(c2[:, h:h + 1], (n, F_PAD))
            ek_ref[h, rows, :] = _bias_features(col, lane, 0, N_SPLIT, 1.0)
            eq_ref[h, rows, :] = _bias_features(col, lane, N_SPLIT, 0, -1.0)
        return cs[n - 1:n, :]

    lax.fori_loop(0, SEQ // n, body, jnp.zeros((1, F_PAD), F32))


def _fox_prep(f, b_pad):
    feat = jax.ShapeDtypeStruct((N_HEADS, SEQ, F_PAD), BF16)
    return pl.pallas_call(
        _fox_prep_kernel,
        out_shape=[feat, feat],
        compiler_params=pltpu.CompilerParams(vmem_limit_bytes=VMEM_LIMIT),
        name="fox_prep",
    )(f, b_pad)


def _tile_iotas():
    ko = lax.broadcasted_iota(jnp.int32, (ATT_T, ATT_T), 0)
    qo = lax.broadcasted_iota(jnp.int32, (ATT_T, ATT_T), 1)
    return ko, qo


def _tile_rows(j):
    return pl.ds(pl.multiple_of(j * ATT_T, ATT_T), ATT_T)


def _scores(k_ref, ek, q_ref, eq, g, j):
    keys = jnp.concatenate([k_ref[g, _tile_rows(j), :], ek], axis=1)
    queries = jnp.concatenate([q_ref[g], eq], axis=1)
    return _nt_dot(keys, queries)


class _Stream(NamedTuple):
    s: Any
    p: Any
    alpha: Any
    m: Any
    l: Any
    acc: Any


def _stream_scratch():
    G, T = ATT_G, ATT_T
    row = pltpu.VMEM((G, 1, T), F32)
    return _Stream(s=pltpu.VMEM((G, T, T), F32), p=pltpu.VMEM((G, T, T), BF16),
                   alpha=row, m=row, l=row, acc=pltpu.VMEM((G, HEAD_DIM, T), F32))


def _fold_first(st, g, s):
    m0 = jnp.max(s, axis=0, keepdims=True)
    p = jnp.exp2(s - m0)
    st.m[g] = m0
    st.l[g] = jnp.sum(p, axis=0, keepdims=True)
    st.p[g] = p.astype(BF16)
    st.alpha[g] = jnp.zeros_like(m0)
    st.acc[g] = jnp.zeros(st.acc.shape[1:], F32)


def _fold(st, g, p, m_new):
    alpha = jnp.exp2(st.m[g] - m_new)
    st.alpha[g] = alpha
    st.l[g] = alpha * st.l[g] + jnp.sum(p, axis=0, keepdims=True)
    st.p[g] = p.astype(BF16)
    st.m[g] = m_new


def _apply(st, g, vt_tile):
    st.acc[g] = st.alpha[g] * st.acc[g] + jnp.dot(vt_tile, st.p[g], preferred_element_type=F32)


def _stream_past_tiles(st, i, vt_ref, score, fold_past, o_ref):
    chains = range(ATT_G)
    for g in chains:
        st.s[g] = score(g, 0)

    def step(r, score_next):
        prev = jnp.where(r == 1, i, r - 2)
        for g in chains:
            _apply(st, g, vt_ref[g, prev])
        for g in chains:
            fold_past(g, r - 1)
        if score_next:
            for g in chains:
                st.s[g] = score(g, r)

    def body(r, _):
        step(r, True)
        return 0
    lax.fori_loop(1, i, body, 0)

    pl.when(i >= 1)(lambda: step(i, False))

    last = jnp.maximum(i - 1, 0)
    for g in chains:
        _apply(st, g, vt_ref[g, last])
        out = st.acc[g] / st.l[g]
        o_ref[g] = out.T.astype(o_ref.dtype)


def _moba_kernel(n_cast, slope_ref, q_ref, k_ref, vt_ref, *refs):
    w32_refs, (o_ref, *w16_refs), rest = refs[:n_cast], refs[n_cast:2 * n_cast + 1], refs[2 * n_cast + 1:]
    kmean_ref, ek_ref, eq_ref, selb_ref, *stream_refs = rest
    st = _Stream(*stream_refs)
    hg = pl.program_id(0)
    i = pl.program_id(1)
    _cast_slabs(w32_refs, w16_refs)
    ko, qo = _tile_iotas()
    half = MOBA_BLOCK
    slope2 = [slope_ref[hg * ATT_G + g] * LOG2E for g in range(ATT_G)]

    @pl.when(i == 0)
    def _():
        off = lax.broadcasted_iota(jnp.int32, (ATT_T, HEAD_DIM), 0).astype(F32)
        lane = lax.broadcasted_iota(jnp.int32, (ATT_T, HEAD_DIM), 1)
        for g in range(ATT_G):
            def body(b, _):
                rows = pl.ds(pl.multiple_of(b * MOBA_BLOCK, MOBA_BLOCK), MOBA_BLOCK)
                kb = k_ref[g, rows, :].astype(F32)
                kmean_ref[g, pl.ds(b, 1), :] = jnp.mean(kb, axis=0, keepdims=True)
                return 0
            lax.fori_loop(0, N_BLOCKS, body, 0)
            ek_ref[g] = _bias_features(off * slope2[g], lane, 0, N_SPLIT, 1.0)
            eq_ref[g] = _bias_features(off * (-slope2[g]), lane, N_SPLIT, 0, 1.0)

    jj = lax.broadcasted_iota(jnp.int32, (N_BLOCKS, ATT_T), 0)
    lane = lax.broadcasted_iota(jnp.int32, (N_BLOCKS, ATT_T), 1)
    own = 2 * i + jnp.where(lane >= half, 1, 0)
    past = jj < own
    for g in range(ATT_G):
        q = q_ref[g]
        km = kmean_ref[g]
        km_hi = km.astype(BF16)
        km_lo = (km - km_hi.astype(F32)).astype(BF16)
        gate = _nt_dot(km_hi, q) + _nt_dot(km_lo, q)
        gm = jnp.where(past, gate, NEG_INF)
        cnt = jnp.zeros((N_BLOCKS, ATT_T), F32)
        for jp in range(N_BLOCKS):
            row = gm[jp:jp + 1, :]
            tie = jnp.where(jj > jp, 1.0, 0.0)
            cnt = cnt + jnp.where(row > gm, 1.0, jnp.where(row == gm, tie, 0.0))
        chosen = jnp.where(past, cnt, float(MOBA_TOPK)) < float(MOBA_TOPK)
        selb_ref[g] = jnp.where(chosen, 0.0, NEG_INF)

    def score(g, tile):
        return _scores(k_ref, ek_ref[g], q_ref, eq_ref[g], g, tile)

    diag = [score(g, i) for g in range(ATT_G)]
    for g in range(ATT_G):
        sel0 = selb_ref[g, pl.ds(2 * i, 1), :]
        vis0 = jnp.where(qo[:1, :] < half, 0.0, sel0)
        s = diag[g] + jnp.where(ko < half, vis0, 0.0)
        _fold_first(st, g, jnp.where(ko <= qo, s, NEG_INF))

    def fold_past(g, t):
        cj = -slope2[g] * (ATT_T * (i - t)).astype(F32)
        r0 = selb_ref[g, pl.ds(2 * t, 1), :]
        r1 = selb_ref[g, pl.ds(2 * t + 1, 1), :]
        b0, b1 = st.s[g, :half, :], st.s[g, half:, :]
        bm = jnp.maximum(jnp.max(b0, axis=0, keepdims=True) + r0,
                         jnp.max(b1, axis=0, keepdims=True) + r1) + cj
        m_new = jnp.maximum(st.m[g], bm)
        t0 = jnp.where(r0 == 0.0, m_new - cj, POS_INF)
        t1 = jnp.where(r1 == 0.0, m_new - cj, POS_INF)
        p = jnp.concatenate([jnp.exp2(b0 - t0), jnp.exp2(b1 - t1)], axis=0)
        _fold(st, g, p, m_new)

    _stream_past_tiles(st, i, vt_ref, score, fold_past, o_ref)


def _cast_slab_specs(ws):
    steps = (N_HEADS // ATT_G) * N_TILES
    specs = [pl.BlockSpec((w.shape[0] // steps, w.shape[1]), lambda h, i: (h * N_TILES + i, 0))
             for w in ws]
    return specs, [jax.ShapeDtypeStruct(w.shape, BF16) for w in ws]


def _cast_slabs(w32_refs, w16_refs):
    for src, dst in zip(w32_refs, w16_refs):
        dst[...] = src[...].astype(BF16)


def _moba(slopes, z4, vt, ws32):
    G, T = ATT_G, ATT_T
    w_specs, w_shapes = _cast_slab_specs(ws32)
    return pl.pallas_call(
        functools.partial(_moba_kernel, len(ws32)),
        grid=(N_HEADS // G, N_TILES),
        in_specs=[
            pl.BlockSpec(memory_space=pltpu.SMEM),
            pl.BlockSpec((None, G, T, HEAD_DIM), lambda h, i: (Q_A_TILE, h, i, 0)),
            pl.BlockSpec((None, G, SEQ, HEAD_DIM), lambda h, i: (K_A_TILE, h, 0, 0),
                         pipeline_mode=pl.Buffered(1)),
            pl.BlockSpec((None, G, N_TILES, HEAD_DIM, T), lambda h, i: (0, h, 0, 0, 0),
                         pipeline_mode=pl.Buffered(1)),
            *w_specs,
        ],
        out_specs=[pl.BlockSpec((G, T, HEAD_DIM), lambda h, i: (h, i, 0)), *w_specs],
        out_shape=[jax.ShapeDtypeStruct((N_HEADS, SEQ, HEAD_DIM), BF16), *w_shapes],
        scratch_shapes=[
            pltpu.VMEM((G, N_BLOCKS, HEAD_DIM), F32),
            pltpu.VMEM((G, T, HEAD_DIM), BF16),
            pltpu.VMEM((G, T, HEAD_DIM), BF16),
            pltpu.VMEM((G, N_BLOCKS, T), F32),
            *_stream_scratch(),
        ],
        compiler_params=pltpu.CompilerParams(
            dimension_semantics=("arbitrary", "arbitrary"), vmem_limit_bytes=VMEM_LIMIT),
        name="moba",
    )(slopes, z4, z4, vt, *ws32)


def _fox_kernel(n_cast, q_ref, eq_ref, k_ref, ek_ref, vt_ref, *refs):
    w32_refs, (o_ref, *w16_refs), stream_refs = (
        refs[:n_cast], refs[n_cast:2 * n_cast + 1], refs[2 * n_cast + 1:])
    st = _Stream(*stream_refs)
    i = pl.program_id(1)
    _cast_slabs(w32_refs, w16_refs)
    ko, qo = _tile_iotas()

    def score(g, tile):
        return _scores(k_ref, ek_ref[g, _tile_rows(tile), :], q_ref, eq_ref[g], g, tile)

    diag = [score(g, i) for g in range(ATT_G)]
    for g in range(ATT_G):
        _fold_first(st, g, jnp.where(ko <= qo, diag[g], NEG_INF))

    def fold_past(g, tile):
        s = st.s[g]
        m_new = jnp.maximum(st.m[g], jnp.max(s, axis=0, keepdims=True))
        _fold(st, g, jnp.exp2(s - m_new), m_new)

    _stream_past_tiles(st, i, vt_ref, score, fold_past, o_ref)


def _fox(z4, vt, ek, eq, ws32):
    G, T = ATT_G, ATT_T
    resident = pl.Buffered(1)
    w_specs, w_shapes = _cast_slab_specs(ws32)
    return pl.pallas_call(
        functools.partial(_fox_kernel, len(ws32)),
        grid=(N_HEADS // G, N_TILES),
        in_specs=[
            pl.BlockSpec((None, G, T, HEAD_DIM), lambda h, i: (Q_B_TILE, h, i, 0)),
            pl.BlockSpec((G, T, F_PAD), lambda h, i: (h, i, 0)),
            pl.BlockSpec((None, G, SEQ, HEAD_DIM), lambda h, i: (K_B_TILE, h, 0, 0),
                         pipeline_mode=resident),
            pl.BlockSpec((G, SEQ, F_PAD), lambda h, i: (h, 0, 0), pipeline_mode=resident),
            pl.BlockSpec((None, G, N_TILES, HEAD_DIM, T), lambda h, i: (1, h, 0, 0, 0),
                         pipeline_mode=resident),
            *w_specs,
        ],
        out_specs=[pl.BlockSpec((G, T, HEAD_DIM), lambda h, i: (h, i, 0)), *w_specs],
        out_shape=[jax.ShapeDtypeStruct((N_HEADS, SEQ, HEAD_DIM), BF16), *w_shapes],
        scratch_shapes=list(_stream_scratch()),
        compiler_params=pltpu.CompilerParams(
            dimension_semantics=("arbitrary", "arbitrary"), vmem_limit_bytes=VMEM_LIMIT),
        name="fox",
    )(z4, eq, z4, ek, vt, *ws32)


def _mix_kernel(a_ref, b_ref, ga_ref, gb_ref, x_ref, wpa_ref, wpb_ref, wo_ref, g_ref,
                x1_ref, hm_ref):
    def heads(ref):
        return jnp.concatenate([ref[h] for h in range(N_HEADS)], axis=1)

    def gate(ref):
        cols = [ref[t, h] for t in range(D_MODEL // WIDTH) for h in range(N_HEADS)]
        return jax.nn.sigmoid(jnp.concatenate(cols, axis=1).astype(F32))

    pa = jnp.dot(heads(a_ref), wpa_ref[...], preferred_element_type=F32)
    pb = jnp.dot(heads(b_ref), wpb_ref[...], preferred_element_type=F32)
    merged = gate(ga_ref) * pa + gate(gb_ref) * pb
    x1 = x_ref[...] + jnp.dot(merged.astype(BF16), wo_ref[...], preferred_element_type=F32)
    x1_ref[...] = x1
    hm_ref[...] = _rms_rows(x1, g_ref[...]).astype(BF16)


def _mix(a, b, z4, x2, wpa, wpb, wo, g):
    tm = MIX_TM
    gate_tiles = D_MODEL // WIDTH
    const = lambda i: (0, 0)
    return pl.pallas_call(
        _mix_kernel,
        grid=(SEQ // tm,),
        in_specs=[
            pl.BlockSpec((N_HEADS, tm, HEAD_DIM), lambda i: (0, i, 0)),
            pl.BlockSpec((N_HEADS, tm, HEAD_DIM), lambda i: (0, i, 0)),
            pl.BlockSpec((gate_tiles, N_HEADS, tm, HEAD_DIM), lambda i: (3, 0, i, 0)),
            pl.BlockSpec((gate_tiles, N_HEADS, tm, HEAD_DIM), lambda i: (4, 0, i, 0)),
            pl.BlockSpec((tm, D_MODEL), lambda i: (i, 0)),
            pl.BlockSpec((WIDTH, D_MODEL), const),
            pl.BlockSpec((WIDTH, D_MODEL), const),
            pl.BlockSpec((D_MODEL, D_MODEL), const),
            pl.BlockSpec((1, D_MODEL), const),
        ],
        out_specs=[
            pl.BlockSpec((tm, D_MODEL), lambda i: (i, 0)),
            pl.BlockSpec((tm, D_MODEL), lambda i: (i, 0)),
        ],
        out_shape=[
            jax.ShapeDtypeStruct((SEQ, D_MODEL), F32),
            jax.ShapeDtypeStruct((SEQ, D_MODEL), BF16),
        ],
        compiler_params=pltpu.CompilerParams(
            dimension_semantics=("arbitrary",), vmem_limit_bytes=VMEM_LIMIT),
        name="mix",
    )(a, b, z4, z4, x2, wpa, wpb, wo, g)


def _mlp_up(hm, wu):
    h = jnp.dot(hm, wu, preferred_element_type=F32)
    return jnp.square(jnp.maximum(h, 0.0)).astype(BF16)


def _mlp_head_kernel(hm_ref, wu_ref, h_ref):
    h_ref[...] = _mlp_up(hm_ref[...], wu_ref[...])


def _mlp_kernel(h0_ref, hm_next_ref, x1_ref, wu_next_ref, wd_ref, g_ref, o_ref, acc_ref, h_ref):
    i, c = pl.program_id(0), pl.program_id(1)

    @pl.when((i == 0) & (c == 0))
    def _():
        h_ref[...] = h0_ref[...]

    @pl.when(c == 0)
    def _():
        acc_ref[...] = x1_ref[...]

    acc_ref[...] += jnp.dot(h_ref[...], wd_ref[...], preferred_element_type=F32)
    h_ref[...] = _mlp_up(hm_next_ref[...], wu_next_ref[...])

    @pl.when(c == pl.num_programs(1) - 1)
    def _():
        o_ref[...] = _rms_rows(acc_ref[...], g_ref[...])


def _mlp(hm, x1, wu, wd, g):
    tm, fc = MLP_TM, MLP_FC
    n_i, n_c = SEQ // tm, D_FF // fc
    params = dict(vmem_limit_bytes=VMEM_LIMIT)
    h0 = pl.pallas_call(
        _mlp_head_kernel,
        grid=(1,),
        in_specs=[pl.BlockSpec((tm, D_MODEL), lambda s: (0, 0)),
                  pl.BlockSpec((D_MODEL, fc), lambda s: (0, 0))],
        out_specs=pl.BlockSpec((tm, fc), lambda s: (0, 0)),
        out_shape=jax.ShapeDtypeStruct((tm, fc), BF16),
        compiler_params=pltpu.CompilerParams(dimension_semantics=("arbitrary",), **params),
        name="mlp_head",
    )(hm, wu)

    def next_rows(i, c):
        return jnp.minimum(i + (c + 1) // n_c, n_i - 1)

    return pl.pallas_call(
        _mlp_kernel,
        grid=(n_i, n_c),
        in_specs=[
            pl.BlockSpec((tm, fc), lambda i, c: (0, 0)),
            pl.BlockSpec((tm, D_MODEL), lambda i, c: (next_rows(i, c), 0)),
            pl.BlockSpec((tm, D_MODEL), lambda i, c: (i, 0)),
            pl.BlockSpec((D_MODEL, fc), lambda i, c: (0, (c + 1) % n_c)),
            pl.BlockSpec((fc, D_MODEL), lambda i, c: (c, 0)),
            pl.BlockSpec((1, D_MODEL), lambda i, c: (0, 0)),
        ],
        out_specs=pl.BlockSpec((tm, D_MODEL), lambda i, c: (i, 0)),
        out_shape=jax.ShapeDtypeStruct((SEQ, D_MODEL), F32),
        scratch_shapes=[pltpu.VMEM((tm, D_MODEL), F32), pltpu.VMEM((tm, fc), BF16)],
        compiler_params=pltpu.CompilerParams(
            dimension_semantics=("arbitrary", "arbitrary"), **params),
        name="mlp",
    )(h0, hm, x1, wu, wd, g)


def kernel(x, norm_mix_g, w_in, b_forget, w_proj_a, w_proj_b, w_out, norm_mlp_g, w_up, w_down,
           norm_final_g):
    f_lo = 6 * WIDTH
    f_hi = f_lo + N_HEADS
    w_qkv = w_in[:, :f_lo].astype(BF16)
    w_gates = w_in[:, f_hi:].astype(BF16)
    w_f = jnp.pad(w_in[:, f_lo:f_hi], ((0, 0), (0, F_PAD - N_HEADS))).astype(BF16)
    row = lambda g: g.reshape(1, D_MODEL).astype(F32)

    x2 = x.reshape(SEQ, D_MODEL)
    z4, f, vt = _inproj(x2, row(norm_mix_g), w_qkv, w_gates, w_f)

    b_pad = jnp.pad(b_forget.astype(F32), (0, F_PAD - N_HEADS)).reshape(1, F_PAD)
    ek, eq = _fox_prep(f, b_pad)

    slopes = jnp.exp2(-8.0 * jnp.arange(1, N_HEADS + 1, dtype=F32) / N_HEADS)
    a, w_up16, w_pa16, w_pb16 = _moba(slopes, z4, vt, (w_up, w_proj_a, w_proj_b))
    b, w_down16, w_out16 = _fox(z4, vt, ek, eq, (w_down, w_out))

    x1, hm = _mix(a, b, z4, x2, w_pa16, w_pb16, w_out16, row(norm_mlp_g))
    out = _mlp(hm, x1, w_up16, w_down16, row(norm_final_g))
    return out.reshape(x.shape)
```

```python
import functools
import math
from typing import Any, NamedTuple

import jax
import jax.numpy as jnp
from jax import lax
from jax.experimental import pallas as pl
from jax.experimental.pallas import tpu as pltpu

D_MODEL = 2048
SEQ = 8192
HEAD_DIM = 128
N_HEADS = 8
WIDTH = N_HEADS * HEAD_DIM
MOBA_BLOCK = 256
MOBA_TOPK = 3
N_BLOCKS = SEQ // MOBA_BLOCK
D_FF = 4 * D_MODEL
RMS_EPS = 1e-6
LOG2E = math.log2(math.e)
SCALE2 = HEAD_DIM ** -0.5 * LOG2E
F_PAD = 128

BF16 = jnp.bfloat16
F32 = jnp.float32
NEG_INF = float("-inf")
POS_INF = float("inf")

VMEM_LIMIT = 56 * 1024 * 1024

IN_TM, IN_TN = 1024, 1024
Q_A_TILE, K_A_TILE, V_A_TILE, Q_B_TILE, K_B_TILE, V_B_TILE = range(6)
N_QKV_TILES = 6 * WIDTH // IN_TN
N_GATE_TILES = 2 * D_MODEL // IN_TN
ATT_T = 2 * MOBA_BLOCK
ATT_G = 4
N_SPLIT = 3
N_TILES = SEQ // ATT_T
MIX_TM = 512
MLP_TM, MLP_FC = 512, 1024
ROW_CHUNK = 128


def _nt_dot(a, b):
    return lax.dot_general(a, b, (((1,), (1,)), ((), ())), preferred_element_type=F32)


def _rms_rows(x, g):
    ms = jnp.mean(x * x, axis=-1, keepdims=True)
    return x * lax.rsqrt(ms + RMS_EPS) * g


def _inproj_kernel(x_ref, g_ref, w_ref, wf_ref, wt_ref, z_ref, f_ref, t_ref, vt_ref, hn_ref):
    j = pl.program_id(1)

    @pl.when(j == 0)
    def _():
        def body(r, _):
            rows = pl.ds(pl.multiple_of(r * ROW_CHUNK, ROW_CHUNK), ROW_CHUNK)
            hn = _rms_rows(x_ref[rows, :], g_ref[...]).astype(BF16)
            hn_ref[rows, :] = hn
            f_ref[rows, :] = jnp.dot(hn, wf_ref[...], preferred_element_type=F32)
            t_ref[rows, :] = jnp.dot(hn, wt_ref[...], preferred_element_type=F32)
            return 0
        lax.fori_loop(0, IN_TM // ROW_CHUNK, body, 0)

    scale = jnp.where((j == Q_A_TILE) | (j == Q_B_TILE), SCALE2, 1.0)
    zf = jnp.dot(hn_ref[...], w_ref[...].astype(BF16), preferred_element_type=F32) * scale
    z = zf.astype(BF16)
    for h in range(IN_TN // HEAD_DIM):
        z_ref[h] = z[:, h * HEAD_DIM:(h + 1) * HEAD_DIM]

    @pl.when((j == V_A_TILE) | (j == V_B_TILE))
    def _():
        for h in range(N_HEADS):
            for t in range(IN_TM // ATT_T):
                blk = zf[t * ATT_T:(t + 1) * ATT_T, h * HEAD_DIM:(h + 1) * HEAD_DIM]
                vt_ref[h, t] = blk.T.astype(BF16)


def _inproj(x2, g, w_in, w_f, w_tail):
    n_i = SEQ // IN_TM
    n_j = N_QKV_TILES + N_GATE_TILES
    slabs = IN_TN // HEAD_DIM
    tiles = IN_TM // ATT_T
    side = pl.BlockSpec((IN_TM, F_PAD), lambda i, j: (i, 0))
    return pl.pallas_call(
        _inproj_kernel,
        grid=(n_i, n_j),
        in_specs=[
            pl.BlockSpec((IN_TM, D_MODEL), lambda i, j: (i, 0)),
            pl.BlockSpec((1, D_MODEL), lambda i, j: (0, 0)),
            pl.BlockSpec((D_MODEL, IN_TN), lambda i, j: (0, j)),
            pl.BlockSpec((D_MODEL, F_PAD), lambda i, j: (0, 0)),
            pl.BlockSpec((D_MODEL, F_PAD), lambda i, j: (0, 0)),
        ],
        out_specs=[
            pl.BlockSpec((None, slabs, IN_TM, HEAD_DIM), lambda i, j: (j, 0, i, 0)),
            side,
            side,
            pl.BlockSpec((None, N_HEADS, tiles, HEAD_DIM, ATT_T),
                         lambda i, j: (jnp.where(j > V_A_TILE, 1, 0), 0, i, 0, 0)),
        ],
        out_shape=[
            jax.ShapeDtypeStruct((n_j, slabs, SEQ, HEAD_DIM), BF16),
            jax.ShapeDtypeStruct((SEQ, F_PAD), F32),
            jax.ShapeDtypeStruct((SEQ, F_PAD), F32),
            jax.ShapeDtypeStruct((2, N_HEADS, N_TILES, HEAD_DIM, ATT_T), BF16),
        ],
        scratch_shapes=[pltpu.VMEM((IN_TM, D_MODEL), BF16)],
        compiler_params=pltpu.CompilerParams(
            dimension_semantics=("arbitrary", "arbitrary"), vmem_limit_bytes=VMEM_LIMIT),
        name="inproj",
    )(x2, g, w_in, w_f, w_tail)


def _split(v):
    pieces = []
    for _ in range(N_SPLIT):
        p = v.astype(BF16).astype(F32)
        pieces.append(p)
        v = v - p
    return pieces


def _bias_features(v, lane, pieces_at, const_at, const):
    f = jnp.zeros_like(v)
    for n, p in enumerate(_split(v)):
        f = jnp.where(lane == pieces_at + n, p, f)
    f = jnp.where((lane >= const_at) & (lane < const_at + N_SPLIT), const, f)
    return f.astype(BF16)


def _fox_prep_kernel(f_ref, b_ref, ek_ref, eq_ref):
    n = ROW_CHUNK
    tri = (lax.broadcasted_iota(jnp.int32, (n, n), 0)
           >= lax.broadcasted_iota(jnp.int32, (n, n), 1)).astype(BF16)
    lane = lax.broadcasted_iota(jnp.int32, (n, F_PAD), 1)

    def body(ci, carry):
        rows = pl.ds(pl.multiple_of(ci * n, n), n)
        lf = jax.nn.log_sigmoid(f_ref[rows, :] + b_ref[...])
        cs = carry
        for p in _split(lf):
            cs = cs + jnp.dot(tri, p.astype(BF16), preferred_element_type=F32)
        c2 = cs * LOG2E
        for h in range(N_HEADS):
            col = jnp.broadcast_to(c2[:, h:h + 1], (n, F_PAD))
            ek_ref[h, rows, :] = _bias_features(col, lane, 0, N_SPLIT, 1.0)
            eq_ref[h, rows, :] = _bias_features(col, lane, N_SPLIT, 0, -1.0)
        return cs[n - 1:n, :]

    lax.fori_loop(0, SEQ // n, body, jnp.zeros((1, F_PAD), F32))


def _fox_prep(f, b_pad):
    feat = jax.ShapeDtypeStruct((N_HEADS, SEQ, F_PAD), BF16)
    return pl.pallas_call(
        _fox_prep_kernel,
        out_shape=[feat, feat],
        compiler_params=pltpu.CompilerParams(vmem_limit_bytes=VMEM_LIMIT),
        name="fox_prep",
    )(f, b_pad)


def _tile_iotas():
    ko = lax.broadcasted_iota(jnp.int32, (ATT_T, ATT_T), 0)
    qo = lax.broadcasted_iota(jnp.int32, (ATT_T, ATT_T), 1)
    return ko, qo


def _tile_rows(j):
    return pl.ds(pl.multiple_of(j * ATT_T, ATT_T), ATT_T)


def _scores(k_ref, ek, q_ref, eq, g, j):
    keys = jnp.concatenate([k_ref[g, _tile_rows(j), :], ek], axis=1)
    queries = jnp.concatenate([q_ref[g], eq], axis=1)
    return _nt_dot(keys, queries)


class _Stream(NamedTuple):
    s: Any
    p: Any
    alpha: Any
    m: Any
    l: Any
    acc: Any


def _stream_scratch():
    G, T = ATT_G, ATT_T
    row = pltpu.VMEM((G, 1, T), F32)
    return _Stream(s=pltpu.VMEM((G, T, T), F32), p=pltpu.VMEM((G, T, T), BF16),
                   alpha=row, m=row, l=row, acc=pltpu.VMEM((G, HEAD_DIM, T), F32))


def _fold_first(st, g, s):
    m0 = jnp.max(s, axis=0, keepdims=True)
    p = jnp.exp2(s - m0)
    st.m[g] = m0
    st.l[g] = jnp.sum(p, axis=0, keepdims=True)
    st.p[g] = p.astype(BF16)
    st.alpha[g] = jnp.zeros_like(m0)
    st.acc[g] = jnp.zeros(st.acc.shape[1:], F32)


def _fold(st, g, p, m_new):
    alpha = jnp.exp2(st.m[g] - m_new)
    st.alpha[g] = alpha
    st.l[g] = alpha * st.l[g] + jnp.sum(p, axis=0, keepdims=True)
    st.p[g] = p.astype(BF16)
    st.m[g] = m_new


def _apply(st, g, vt_tile):
    st.acc[g] = st.alpha[g] * st.acc[g] + jnp.dot(vt_tile, st.p[g], preferred_element_type=F32)


def _stream_past_tiles(st, i, vt_ref, score, fold_past, o_ref):
    chains = range(ATT_G)
    for g in chains:
        st.s[g] = score(g, 0)

    def step(r, score_next):
        prev = jnp.where(r == 1, i, r - 2)
        for g in chains:
            _apply(st, g, vt_ref[g, prev])
        for g in chains:
            fold_past(g, r - 1)
        if score_next:
            for g in chains:
                st.s[g] = score(g, r)

    def body(r, _):
        step(r, True)
        return 0
    lax.fori_loop(1, i, body, 0)

    pl.when(i >= 1)(lambda: step(i, False))

    last = jnp.maximum(i - 1, 0)
    for g in chains:
        _apply(st, g, vt_ref[g, last])
        out = st.acc[g] / st.l[g]
        o_ref[g] = out.T.astype(o_ref.dtype)


def _moba_kernel(n_cast, slope_ref, q_ref, k_ref, vt_ref, *refs):
    w32_refs, (o_ref, *w16_refs), rest = refs[:n_cast], refs[n_cast:2 * n_cast + 1], refs[2 * n_cast + 1:]
    kmean_ref, ek_ref, eq_ref, selb_ref, *stream_refs = rest
    st = _Stream(*stream_refs)
    hg = pl.program_id(0)
    i = pl.program_id(1)
    _cast_slabs(w32_refs, w16_refs)
    ko, qo = _tile_iotas()
    half = MOBA_BLOCK
    slope2 = [slope_ref[hg * ATT_G + g] * LOG2E for g in range(ATT_G)]

    @pl.when(i == 0)
    def _():
        off = lax.broadcasted_iota(jnp.int32, (ATT_T, HEAD_DIM), 0).astype(F32)
        lane = lax.broadcasted_iota(jnp.int32, (ATT_T, HEAD_DIM), 1)
        for g in range(ATT_G):
            def body(b, _):
                rows = pl.ds(pl.multiple_of(b * MOBA_BLOCK, MOBA_BLOCK), MOBA_BLOCK)
                kb = k_ref[g, rows, :].astype(F32)
                kmean_ref[g, pl.ds(b, 1), :] = jnp.mean(kb, axis=0, keepdims=True)
                return 0
            lax.fori_loop(0, N_BLOCKS, body, 0)
            ek_ref[g] = _bias_features(off * slope2[g], lane, 0, N_SPLIT, 1.0)
            eq_ref[g] = _bias_features(off * (-slope2[g]), lane, N_SPLIT, 0, 1.0)

    jj = lax.broadcasted_iota(jnp.int32, (N_BLOCKS, ATT_T), 0)
    lane = lax.broadcasted_iota(jnp.int32, (N_BLOCKS, ATT_T), 1)
    own = 2 * i + jnp.where(lane >= half, 1, 0)
    past = jj < own
    for g in range(ATT_G):
        q = q_ref[g]
        km = kmean_ref[g]
        km_hi = km.astype(BF16)
        km_lo = (km - km_hi.astype(F32)).astype(BF16)
        gate = _nt_dot(km_hi, q) + _nt_dot(km_lo, q)
        gm = jnp.where(past, gate, NEG_INF)
        cnt = jnp.zeros((N_BLOCKS, ATT_T), F32)
        for jp in range(N_BLOCKS):
            row = gm[jp:jp + 1, :]
            tie = jnp.where(jj > jp, 1.0, 0.0)
            cnt = cnt + jnp.where(row > gm, 1.0, jnp.where(row == gm, tie, 0.0))
        chosen = jnp.where(past, cnt, float(MOBA_TOPK)) < float(MOBA_TOPK)
        selb_ref[g] = jnp.where(chosen, 0.0, NEG_INF)

    def score(g, tile):
        return _scores(k_ref, ek_ref[g], q_ref, eq_ref[g], g, tile)

    diag = [score(g, i) for g in range(ATT_G)]
    for g in range(ATT_G):
        sel0 = selb_ref[g, pl.ds(2 * i, 1), :]
        vis0 = jnp.where(qo[:1, :] < half, 0.0, sel0)
        s = diag[g] + jnp.where(ko < half, vis0, 0.0)
        _fold_first(st, g, jnp.where(ko <= qo, s, NEG_INF))

    def fold_past(g, t):
        cj = -slope2[g] * (ATT_T * (i - t)).astype(F32)
        r0 = selb_ref[g, pl.ds(2 * t, 1), :]
        r1 = selb_ref[g, pl.ds(2 * t + 1, 1), :]
        b0, b1 = st.s[g, :half, :], st.s[g, half:, :]
        bm = jnp.maximum(jnp.max(b0, axis=0, keepdims=True) + r0,
                         jnp.max(b1, axis=0, keepdims=True) + r1) + cj
        m_new = jnp.maximum(st.m[g], bm)
        t0 = jnp.where(r0 == 0.0, m_new - cj, POS_INF)
        t1 = jnp.where(r1 == 0.0, m_new - cj, POS_INF)
        p = jnp.concatenate([jnp.exp2(b0 - t0), jnp.exp2(b1 - t1)], axis=0)
        _fold(st, g, p, m_new)

    _stream_past_tiles(st, i, vt_ref, score, fold_past, o_ref)


def _cast_slab_specs(ws):
    steps = (N_HEADS // ATT_G) * N_TILES
    specs = [pl.BlockSpec((w.shape[0] // steps, w.shape[1]), lambda h, i: (h * N_TILES + i, 0))
             for w in ws]
    return specs, [jax.ShapeDtypeStruct(w.shape, BF16) for w in ws]


def _cast_slabs(w32_refs, w16_refs):
    for src, dst in zip(w32_refs, w16_refs):
        dst[...] = src[...].astype(BF16)


def _moba(slopes, z4, vt, ws32):
    G, T = ATT_G, ATT_T
    w_specs, w_shapes = _cast_slab_specs(ws32)
    return pl.pallas_call(
        functools.partial(_moba_kernel, len(ws32)),
        grid=(N_HEADS // G, N_TILES),
        in_specs=[
            pl.BlockSpec(memory_space=pltpu.SMEM),
            pl.BlockSpec((None, G, T, HEAD_DIM), lambda h, i: (Q_A_TILE, h, i, 0)),
            pl.BlockSpec((None, G, SEQ, HEAD_DIM), lambda h, i: (K_A_TILE, h, 0, 0),
                         pipeline_mode=pl.Buffered(1)),
            pl.BlockSpec((None, G, N_TILES, HEAD_DIM, T), lambda h, i: (0, h, 0, 0, 0),
                         pipeline_mode=pl.Buffered(1)),
            *w_specs,
        ],
        out_specs=[pl.BlockSpec((G, T, HEAD_DIM), lambda h, i: (h, i, 0)), *w_specs],
        out_shape=[jax.ShapeDtypeStruct((N_HEADS, SEQ, HEAD_DIM), BF16), *w_shapes],
        scratch_shapes=[
            pltpu.VMEM((G, N_BLOCKS, HEAD_DIM), F32),
            pltpu.VMEM((G, T, HEAD_DIM), BF16),
            pltpu.VMEM((G, T, HEAD_DIM), BF16),
            pltpu.VMEM((G, N_BLOCKS, T), F32),
            *_stream_scratch(),
        ],
        compiler_params=pltpu.CompilerParams(
            dimension_semantics=("arbitrary", "arbitrary"), vmem_limit_bytes=VMEM_LIMIT),
        name="moba",
    )(slopes, z4, z4, vt, *ws32)


def _fox_kernel(n_cast, q_ref, eq_ref, k_ref, ek_ref, vt_ref, *refs):
    w32_refs, (o_ref, *w16_refs), stream_refs = (
        refs[:n_cast], refs[n_cast:2 * n_cast + 1], refs[2 * n_cast + 1:])
    st = _Stream(*stream_refs)
    i = pl.program_id(1)
    _cast_slabs(w32_refs, w16_refs)
    ko, qo = _tile_iotas()

    def score(g, tile):
        return _scores(k_ref, ek_ref[g, _tile_rows(tile), :], q_ref, eq_ref[g], g, tile)

    diag = [score(g, i) for g in range(ATT_G)]
    for g in range(ATT_G):
        _fold_first(st, g, jnp.where(ko <= qo, diag[g], NEG_INF))

    def fold_past(g, tile):
        s = st.s[g]
        m_new = jnp.maximum(st.m[g], jnp.max(s, axis=0, keepdims=True))
        _fold(st, g, jnp.exp2(s - m_new), m_new)

    _stream_past_tiles(st, i, vt_ref, score, fold_past, o_ref)


def _fox(z4, vt, ek, eq, ws32):
    G, T = ATT_G, ATT_T
    resident = pl.Buffered(1)
    w_specs, w_shapes = _cast_slab_specs(ws32)
    return pl.pallas_call(
        functools.partial(_fox_kernel, len(ws32)),
        grid=(N_HEADS // G, N_TILES),
        in_specs=[
            pl.BlockSpec((None, G, T, HEAD_DIM), lambda h, i: (Q_B_TILE, h, i, 0)),
            pl.BlockSpec((G, T, F_PAD), lambda h, i: (h, i, 0)),
            pl.BlockSpec((None, G, SEQ, HEAD_DIM), lambda h, i: (K_B_TILE, h, 0, 0),
                         pipeline_mode=resident),
            pl.BlockSpec((G, SEQ, F_PAD), lambda h, i: (h, 0, 0), pipeline_mode=resident),
            pl.BlockSpec((None, G, N_TILES, HEAD_DIM, T), lambda h, i: (1, h, 0, 0, 0),
                         pipeline_mode=resident),
            *w_specs,
        ],
        out_specs=[pl.BlockSpec((G, T, HEAD_DIM), lambda h, i: (h, i, 0)), *w_specs],
        out_shape=[jax.ShapeDtypeStruct((N_HEADS, SEQ, HEAD_DIM), BF16), *w_shapes],
        scratch_shapes=list(_stream_scratch()),
        compiler_params=pltpu.CompilerParams(
            dimension_semantics=("arbitrary", "arbitrary"), vmem_limit_bytes=VMEM_LIMIT),
        name="fox",
    )(z4, eq, z4, ek, vt, *ws32)


def _mix_kernel(a_ref, b_ref, g67_ref, g89_ref, gtail_ref, x_ref, wpa_ref, wpb_ref, wo_ref, g_ref,
                x1_ref, hm_ref):
    def heads(ref):
        return jnp.concatenate([ref[h] for h in range(N_HEADS)], axis=1)

    def slabs(ref):
        return [ref[t, h].astype(F32) for t in range(ref.shape[0]) for h in range(N_HEADS)]

    def gate(raw_slabs):
        raw = jnp.concatenate(raw_slabs, axis=1)
        return jax.nn.sigmoid(raw[:, N_HEADS:N_HEADS + D_MODEL])

    gate_a = gate(slabs(g67_ref) + [g89_ref[0, 0].astype(F32)])
    gate_b = gate(slabs(g89_ref) + [gtail_ref[...]])
    pa = jnp.dot(heads(a_ref), wpa_ref[...], preferred_element_type=F32)
    pb = jnp.dot(heads(b_ref), wpb_ref[...], preferred_element_type=F32)
    merged = gate_a * pa + gate_b * pb
    x1 = x_ref[...] + jnp.dot(merged.astype(BF16), wo_ref[...], preferred_element_type=F32)
    x1_ref[...] = x1
    hm_ref[...] = _rms_rows(x1, g_ref[...]).astype(BF16)


def _mix(a, b, z4, gtail, x2, wpa, wpb, wo, g):
    tm = MIX_TM
    const = lambda i: (0, 0)
    return pl.pallas_call(
        _mix_kernel,
        grid=(SEQ // tm,),
        in_specs=[
            pl.BlockSpec((N_HEADS, tm, HEAD_DIM), lambda i: (0, i, 0)),
            pl.BlockSpec((N_HEADS, tm, HEAD_DIM), lambda i: (0, i, 0)),
            pl.BlockSpec((2, N_HEADS, tm, HEAD_DIM), lambda i: (3, 0, i, 0)),
            pl.BlockSpec((2, N_HEADS, tm, HEAD_DIM), lambda i: (4, 0, i, 0)),
            pl.BlockSpec((tm, F_PAD), lambda i: (i, 0)),
            pl.BlockSpec((tm, D_MODEL), lambda i: (i, 0)),
            pl.BlockSpec((WIDTH, D_MODEL), const),
            pl.BlockSpec((WIDTH, D_MODEL), const),
            pl.BlockSpec((D_MODEL, D_MODEL), const),
            pl.BlockSpec((1, D_MODEL), const),
        ],
        out_specs=[
            pl.BlockSpec((tm, D_MODEL), lambda i: (i, 0)),
            pl.BlockSpec((tm, D_MODEL), lambda i: (i, 0)),
        ],
        out_shape=[
            jax.ShapeDtypeStruct((SEQ, D_MODEL), F32),
            jax.ShapeDtypeStruct((SEQ, D_MODEL), BF16),
        ],
        compiler_params=pltpu.CompilerParams(
            dimension_semantics=("arbitrary",), vmem_limit_bytes=VMEM_LIMIT),
        name="mix",
    )(a, b, z4, z4, gtail, x2, wpa, wpb, wo, g)


def _mlp_up(hm, wu):
    h = jnp.dot(hm, wu, preferred_element_type=F32)
    return jnp.square(jnp.maximum(h, 0.0)).astype(BF16)


def _mlp_head_kernel(hm_ref, wu_ref, h_ref):
    h_ref[...] = _mlp_up(hm_ref[...], wu_ref[...])


def _mlp_kernel(h0_ref, hm_next_ref, x1_ref, wu_next_ref, wd_ref, g_ref, o_ref, acc_ref, h_ref):
    i, c = pl.program_id(0), pl.program_id(1)

    @pl.when((i == 0) & (c == 0))
    def _():
        h_ref[...] = h0_ref[...]

    @pl.when(c == 0)
    def _():
        acc_ref[...] = x1_ref[...]

    acc_ref[...] += jnp.dot(h_ref[...], wd_ref[...], preferred_element_type=F32)
    h_ref[...] = _mlp_up(hm_next_ref[...], wu_next_ref[...])

    @pl.when(c == pl.num_programs(1) - 1)
    def _():
        o_ref[...] = _rms_rows(acc_ref[...], g_ref[...])


def _mlp(hm, x1, wu, wd, g):
    tm, fc = MLP_TM, MLP_FC
    n_i, n_c = SEQ // tm, D_FF // fc
    params = dict(vmem_limit_bytes=VMEM_LIMIT)
    h0 = pl.pallas_call(
        _mlp_head_kernel,
        grid=(1,),
        in_specs=[pl.BlockSpec((tm, D_MODEL), lambda s: (0, 0)),
                  pl.BlockSpec((D_MODEL, fc), lambda s: (0, 0))],
        out_specs=pl.BlockSpec((tm, fc), lambda s: (0, 0)),
        out_shape=jax.ShapeDtypeStruct((tm, fc), BF16),
        compiler_params=pltpu.CompilerParams(dimension_semantics=("arbitrary",), **params),
        name="mlp_head",
    )(hm, wu)

    def next_rows(i, c):
        return jnp.minimum(i + (c + 1) // n_c, n_i - 1)

    return pl.pallas_call(
        _mlp_kernel,
        grid=(n_i, n_c),
        in_specs=[
            pl.BlockSpec((tm, fc), lambda i, c: (0, 0)),
            pl.BlockSpec((tm, D_MODEL), lambda i, c: (next_rows(i, c), 0)),
            pl.BlockSpec((tm, D_MODEL), lambda i, c: (i, 0)),
            pl.BlockSpec((D_MODEL, fc), lambda i, c: (0, (c + 1) % n_c)),
            pl.BlockSpec((fc, D_MODEL), lambda i, c: (c, 0)),
            pl.BlockSpec((1, D_MODEL), lambda i, c: (0, 0)),
        ],
        out_specs=pl.BlockSpec((tm, D_MODEL), lambda i, c: (i, 0)),
        out_shape=jax.ShapeDtypeStruct((SEQ, D_MODEL), F32),
        scratch_shapes=[pltpu.VMEM((tm, D_MODEL), F32), pltpu.VMEM((tm, fc), BF16)],
        compiler_params=pltpu.CompilerParams(
            dimension_semantics=("arbitrary", "arbitrary"), **params),
        name="mlp",
    )(h0, hm, x1, wu, wd, g)


def kernel(x, norm_mix_g, w_in, b_forget, w_proj_a, w_proj_b, w_out, norm_mlp_g, w_up, w_down,
           norm_final_g):
    f_lo = 6 * WIDTH
    tail_lo = (N_QKV_TILES + N_GATE_TILES) * IN_TN
    lane_tile = lambda w: jnp.pad(w, ((0, 0), (0, F_PAD - w.shape[1]))).astype(BF16)
    w_f = lane_tile(w_in[:, f_lo:f_lo + N_HEADS])
    w_tail = lane_tile(w_in[:, tail_lo:])
    row = lambda g: g.reshape(1, D_MODEL).astype(F32)

    x2 = x.reshape(SEQ, D_MODEL)
    z4, f, gtail, vt = _inproj(x2, row(norm_mix_g), w_in, w_f, w_tail)

    b_pad = jnp.pad(b_forget.astype(F32), (0, F_PAD - N_HEADS)).reshape(1, F_PAD)
    ek, eq = _fox_prep(f, b_pad)

    slopes = jnp.exp2(-8.0 * jnp.arange(1, N_HEADS + 1, dtype=F32) / N_HEADS)
    a, w_up16, w_pa16, w_pb16 = _moba(slopes, z4, vt, (w_up, w_proj_a, w_proj_b))
    b, w_down16, w_out16 = _fox(z4, vt, ek, eq, (w_down, w_out))

    x1, hm = _mix(a, b, z4, gtail, x2, w_pa16, w_pb16, w_out16, row(norm_mlp_g))
    out = _mlp(hm, x1, w_up16, w_down16, row(norm_final_g))
    return out.reshape(x.shape)
```

```python
import functools
import math
from typing import Any, NamedTuple

import jax
import jax.numpy as jnp
from jax import lax
from jax.experimental import pallas as pl
from jax.experimental.pallas import tpu as pltpu

D_MODEL = 2048
SEQ = 8192
HEAD_DIM = 128
N_HEADS = 8
WIDTH = N_HEADS * HEAD_DIM
MOBA_BLOCK = 256
MOBA_TOPK = 3
N_BLOCKS = SEQ // MOBA_BLOCK
D_FF = 4 * D_MODEL
RMS_EPS = 1e-6
LOG2E = math.log2(math.e)
SCALE2 = HEAD_DIM ** -0.5 * LOG2E
F_PAD = 128

BF16 = jnp.bfloat16
F32 = jnp.float32
NEG_INF = float("-inf")
POS_INF = float("inf")

VMEM_LIMIT = 56 * 1024 * 1024

IN_TM, IN_TN = 1024, 1024
Q_A_TILE, K_A_TILE, V_A_TILE, Q_B_TILE, K_B_TILE, V_B_TILE = range(6)
N_QKV_TILES = 6 * WIDTH // IN_TN
N_GATE_TILES = 2 * D_MODEL // IN_TN
ATT_T = 2 * MOBA_BLOCK
ATT_G = 4
N_SPLIT = 3
N_TILES = SEQ // ATT_T
MIX_TM = 512
MLP_TM, MLP_FC = 512, 1024
ROW_CHUNK = 128


def _nt_dot(a, b):
    return lax.dot_general(a, b, (((1,), (1,)), ((), ())), preferred_element_type=F32)


def _rms_rows(x, g):
    ms = jnp.mean(x * x, axis=-1, keepdims=True)
    return x * lax.rsqrt(ms + RMS_EPS) * g


def _inproj_kernel(x_ref, g_ref, wt_ref, wft_ref, z_ref, f_ref, vt_ref, hn_ref):
    j = pl.program_id(1)

    @pl.when(j == 0)
    def _():
        wft = wft_ref[...].astype(BF16)

        def body(r, _):
            rows = pl.ds(pl.multiple_of(r * ROW_CHUNK, ROW_CHUNK), ROW_CHUNK)
            hn = _rms_rows(x_ref[rows, :], g_ref[...]).astype(BF16)
            hn_ref[rows, :] = hn
            f_ref[rows, :] = _nt_dot(hn, wft)
            return 0
        lax.fori_loop(0, IN_TM // ROW_CHUNK, body, 0)

    scale = jnp.where((j == Q_A_TILE) | (j == Q_B_TILE), SCALE2, 1.0)
    zf = _nt_dot(hn_ref[...], wt_ref[...].astype(BF16)) * scale
    z = zf.astype(BF16)
    for h in range(IN_TN // HEAD_DIM):
        z_ref[h] = z[:, h * HEAD_DIM:(h + 1) * HEAD_DIM]

    @pl.when((j == V_A_TILE) | (j == V_B_TILE))
    def _():
        for h in range(N_HEADS):
            for t in range(IN_TM // ATT_T):
                blk = zf[t * ATT_T:(t + 1) * ATT_T, h * HEAD_DIM:(h + 1) * HEAD_DIM]
                vt_ref[h, t] = blk.T.astype(BF16)


def _inproj(x2, g, w_in_t):
    n_i = SEQ // IN_TM
    n_j = N_QKV_TILES + N_GATE_TILES
    slabs = IN_TN // HEAD_DIM
    tiles = IN_TM // ATT_T
    f_row = N_QKV_TILES * IN_TN

    def tile_row(i, j):
        return pl.multiple_of(j * IN_TN + jnp.where(j >= N_QKV_TILES, N_HEADS, 0), N_HEADS), 0

    return pl.pallas_call(
        _inproj_kernel,
        grid=(n_i, n_j),
        in_specs=[
            pl.BlockSpec((IN_TM, D_MODEL), lambda i, j: (i, 0)),
            pl.BlockSpec((1, D_MODEL), lambda i, j: (0, 0)),
            pl.BlockSpec((pl.Element(IN_TN), pl.Element(D_MODEL)), tile_row),
            pl.BlockSpec((F_PAD, D_MODEL), lambda i, j: (f_row // F_PAD, 0)),
        ],
        out_specs=[
            pl.BlockSpec((None, slabs, IN_TM, HEAD_DIM), lambda i, j: (j, 0, i, 0)),
            pl.BlockSpec((IN_TM, F_PAD), lambda i, j: (i, 0)),
            pl.BlockSpec((None, N_HEADS, tiles, HEAD_DIM, ATT_T),
                         lambda i, j: (jnp.where(j > V_A_TILE, 1, 0), 0, i, 0, 0)),
        ],
        out_shape=[
            jax.ShapeDtypeStruct((n_j, slabs, SEQ, HEAD_DIM), BF16),
            jax.ShapeDtypeStruct((SEQ, F_PAD), F32),
            jax.ShapeDtypeStruct((2, N_HEADS, N_TILES, HEAD_DIM, ATT_T), BF16),
        ],
        scratch_shapes=[pltpu.VMEM((IN_TM, D_MODEL), BF16)],
        compiler_params=pltpu.CompilerParams(
            dimension_semantics=("arbitrary", "arbitrary"), vmem_limit_bytes=VMEM_LIMIT),
        name="inproj",
    )(x2, g, w_in_t, w_in_t)


def _split(v):
    pieces = []
    for _ in range(N_SPLIT):
        p = v.astype(BF16).astype(F32)
        pieces.append(p)
        v = v - p
    return pieces


def _bias_features(v, lane, pieces_at, const_at, const):
    f = jnp.zeros_like(v)
    for n, p in enumerate(_split(v)):
        f = jnp.where(lane == pieces_at + n, p, f)
    f = jnp.where((lane >= const_at) & (lane < const_at + N_SPLIT), const, f)
    return f.astype(BF16)


def _fox_prep_kernel(f_ref, b_ref, ek_ref, eq_ref):
    n = ROW_CHUNK
    tri = (lax.broadcasted_iota(jnp.int32, (n, n), 0)
           >= lax.broadcasted_iota(jnp.int32, (n, n), 1)).astype(BF16)
    lane = lax.broadcasted_iota(jnp.int32, (n, F_PAD), 1)

    def body(ci, carry):
        rows = pl.ds(pl.multiple_of(ci * n, n), n)
        lf = jax.nn.log_sigmoid(f_ref[rows, :] + b_ref[...])
        cs = carry
        for p in _split(lf):
            cs = cs + jnp.dot(tri, p.astype(BF16), preferred_element_type=F32)
        c2 = cs * LOG2E
        for h in range(N_HEADS):
            col = jnp.broadcast_to(c2[:, h:h + 1], (n, F_PAD))
            ek_ref[h, rows, :] = _bias_features(col, lane, 0, N_SPLIT, 1.0)
            eq_ref[h, rows, :] = _bias_features(col, lane, N_SPLIT, 0, -1.0)
        return cs[n - 1:n, :]

    lax.fori_loop(0, SEQ // n, body, jnp.zeros((1, F_PAD), F32))


def _fox_prep(f, b_pad):
    feat = jax.ShapeDtypeStruct((N_HEADS, SEQ, F_PAD), BF16)
    return pl.pallas_call(
        _fox_prep_kernel,
        out_shape=[feat, feat],
        compiler_params=pltpu.CompilerParams(vmem_limit_bytes=VMEM_LIMIT),
        name="fox_prep",
    )(f, b_pad)


def _tile_iotas():
    ko = lax.broadcasted_iota(jnp.int32, (ATT_T, ATT_T), 0)
    qo = lax.broadcasted_iota(jnp.int32, (ATT_T, ATT_T), 1)
    return ko, qo


def _tile_rows(j):
    return pl.ds(pl.multiple_of(j * ATT_T, ATT_T), ATT_T)


def _scores(k_ref, ek, q_ref, eq, g, j):
    keys = jnp.concatenate([k_ref[g, _tile_rows(j), :], ek], axis=1)
    queries = jnp.concatenate([q_ref[g], eq], axis=1)
    return _nt_dot(keys, queries)


class _Stream(NamedTuple):
    s: Any
    p: Any
    alpha: Any
    m: Any
    l: Any
    acc: Any


def _stream_scratch():
    G, T = ATT_G, ATT_T
    row = pltpu.VMEM((G, 1, T), F32)
    return _Stream(s=pltpu.VMEM((G, T, T), F32), p=pltpu.VMEM((G, T, T), BF16),
                   alpha=row, m=row, l=row, acc=pltpu.VMEM((G, HEAD_DIM, T), F32))


def _fold_first(st, g, s):
    m0 = jnp.max(s, axis=0, keepdims=True)
    p = jnp.exp2(s - m0)
    st.m[g] = m0
    st.l[g] = jnp.sum(p, axis=0, keepdims=True)
    st.p[g] = p.astype(BF16)
    st.alpha[g] = jnp.zeros_like(m0)
    st.acc[g] = jnp.zeros(st.acc.shape[1:], F32)


def _fold(st, g, p, m_new):
    alpha = jnp.exp2(st.m[g] - m_new)
    st.alpha[g] = alpha
    st.l[g] = alpha * st.l[g] + jnp.sum(p, axis=0, keepdims=True)
    st.p[g] = p.astype(BF16)
    st.m[g] = m_new


def _apply(st, g, vt_tile):
    st.acc[g] = st.alpha[g] * st.acc[g] + jnp.dot(vt_tile, st.p[g], preferred_element_type=F32)


def _stream_past_tiles(st, i, vt_ref, score, fold_past, o_ref):
    chains = range(ATT_G)
    for g in chains:
        st.s[g] = score(g, 0)

    def step(r, score_next):
        prev = jnp.where(r == 1, i, r - 2)
        for g in chains:
            _apply(st, g, vt_ref[g, prev])
        for g in chains:
            fold_past(g, r - 1)
        if score_next:
            for g in chains:
                st.s[g] = score(g, r)

    def body(r, _):
        step(r, True)
        return 0
    lax.fori_loop(1, i, body, 0)

    pl.when(i >= 1)(lambda: step(i, False))

    last = jnp.maximum(i - 1, 0)
    for g in chains:
        _apply(st, g, vt_ref[g, last])
        out = st.acc[g] / st.l[g]
        o_ref[g] = out.T.astype(o_ref.dtype)


def _moba_kernel(n_cast, slope_ref, q_ref, k_ref, vt_ref, *refs):
    w32_refs, (o_ref, *w16_refs), rest = refs[:n_cast], refs[n_cast:2 * n_cast + 1], refs[2 * n_cast + 1:]
    kmean_ref, ek_ref, eq_ref, selb_ref, *stream_refs = rest
    st = _Stream(*stream_refs)
    hg = pl.program_id(0)
    i = pl.program_id(1)
    _cast_slabs(w32_refs, w16_refs)
    ko, qo = _tile_iotas()
    half = MOBA_BLOCK
    slope2 = [slope_ref[hg * ATT_G + g] * LOG2E for g in range(ATT_G)]

    @pl.when(i == 0)
    def _():
        off = lax.broadcasted_iota(jnp.int32, (ATT_T, HEAD_DIM), 0).astype(F32)
        lane = lax.broadcasted_iota(jnp.int32, (ATT_T, HEAD_DIM), 1)
        for g in range(ATT_G):
            def body(b, _):
                rows = pl.ds(pl.multiple_of(b * MOBA_BLOCK, MOBA_BLOCK), MOBA_BLOCK)
                kb = k_ref[g, rows, :].astype(F32)
                kmean_ref[g, pl.ds(b, 1), :] = jnp.mean(kb, axis=0, keepdims=True)
                return 0
            lax.fori_loop(0, N_BLOCKS, body, 0)
            ek_ref[g] = _bias_features(off * slope2[g], lane, 0, N_SPLIT, 1.0)
            eq_ref[g] = _bias_features(off * (-slope2[g]), lane, N_SPLIT, 0, 1.0)

    jj = lax.broadcasted_iota(jnp.int32, (N_BLOCKS, ATT_T), 0)
    lane = lax.broadcasted_iota(jnp.int32, (N_BLOCKS, ATT_T), 1)
    own = 2 * i + jnp.where(lane >= half, 1, 0)
    past = jj < own
    for g in range(ATT_G):
        q = q_ref[g]
        km = kmean_ref[g]
        km_hi = km.astype(BF16)
        km_lo = (km - km_hi.astype(F32)).astype(BF16)
        gate = _nt_dot(km_hi, q) + _nt_dot(km_lo, q)
        gm = jnp.where(past, gate, NEG_INF)
        cnt = jnp.zeros((N_BLOCKS, ATT_T), F32)
        for jp in range(N_BLOCKS):
            row = gm[jp:jp + 1, :]
            tie = jnp.where(jj > jp, 1.0, 0.0)
            cnt = cnt + jnp.where(row > gm, 1.0, jnp.where(row == gm, tie, 0.0))
        chosen = jnp.where(past, cnt, float(MOBA_TOPK)) < float(MOBA_TOPK)
        selb_ref[g] = jnp.where(chosen, 0.0, NEG_INF)

    def score(g, tile):
        return _scores(k_ref, ek_ref[g], q_ref, eq_ref[g], g, tile)

    diag = [score(g, i) for g in range(ATT_G)]
    for g in range(ATT_G):
        sel0 = selb_ref[g, pl.ds(2 * i, 1), :]
        vis0 = jnp.where(qo[:1, :] < half, 0.0, sel0)
        s = diag[g] + jnp.where(ko < half, vis0, 0.0)
        _fold_first(st, g, jnp.where(ko <= qo, s, NEG_INF))

    def fold_past(g, t):
        cj = -slope2[g] * (ATT_T * (i - t)).astype(F32)
        r0 = selb_ref[g, pl.ds(2 * t, 1), :]
        r1 = selb_ref[g, pl.ds(2 * t + 1, 1), :]
        b0, b1 = st.s[g, :half, :], st.s[g, half:, :]
        bm = jnp.maximum(jnp.max(b0, axis=0, keepdims=True) + r0,
                         jnp.max(b1, axis=0, keepdims=True) + r1) + cj
        m_new = jnp.maximum(st.m[g], bm)
        t0 = jnp.where(r0 == 0.0, m_new - cj, POS_INF)
        t1 = jnp.where(r1 == 0.0, m_new - cj, POS_INF)
        p = jnp.concatenate([jnp.exp2(b0 - t0), jnp.exp2(b1 - t1)], axis=0)
        _fold(st, g, p, m_new)

    _stream_past_tiles(st, i, vt_ref, score, fold_past, o_ref)


def _cast_slab_specs(ws):
    steps = (N_HEADS // ATT_G) * N_TILES
    specs = [pl.BlockSpec((w.shape[0] // steps, w.shape[1]), lambda h, i: (h * N_TILES + i, 0))
             for w in ws]
    return specs, [jax.ShapeDtypeStruct(w.shape, BF16) for w in ws]


def _cast_slabs(w32_refs, w16_refs):
    for src, dst in zip(w32_refs, w16_refs):
        dst[...] = src[...].astype(BF16)


def _moba(slopes, z4, vt, ws32):
    G, T = ATT_G, ATT_T
    w_specs, w_shapes = _cast_slab_specs(ws32)
    return pl.pallas_call(
        functools.partial(_moba_kernel, len(ws32)),
        grid=(N_HEADS // G, N_TILES),
        in_specs=[
            pl.BlockSpec(memory_space=pltpu.SMEM),
            pl.BlockSpec((None, G, T, HEAD_DIM), lambda h, i: (Q_A_TILE, h, i, 0)),
            pl.BlockSpec((None, G, SEQ, HEAD_DIM), lambda h, i: (K_A_TILE, h, 0, 0),
                         pipeline_mode=pl.Buffered(1)),
            pl.BlockSpec((None, G, N_TILES, HEAD_DIM, T), lambda h, i: (0, h, 0, 0, 0),
                         pipeline_mode=pl.Buffered(1)),
            *w_specs,
        ],
        out_specs=[pl.BlockSpec((G, T, HEAD_DIM), lambda h, i: (h, i, 0)), *w_specs],
        out_shape=[jax.ShapeDtypeStruct((N_HEADS, SEQ, HEAD_DIM), BF16), *w_shapes],
        scratch_shapes=[
            pltpu.VMEM((G, N_BLOCKS, HEAD_DIM), F32),
            pltpu.VMEM((G, T, HEAD_DIM), BF16),
            pltpu.VMEM((G, T, HEAD_DIM), BF16),
            pltpu.VMEM((G, N_BLOCKS, T), F32),
            *_stream_scratch(),
        ],
        compiler_params=pltpu.CompilerParams(
            dimension_semantics=("arbitrary", "arbitrary"), vmem_limit_bytes=VMEM_LIMIT),
        name="moba",
    )(slopes, z4, z4, vt, *ws32)


def _fox_kernel(n_cast, q_ref, eq_ref, k_ref, ek_ref, vt_ref, *refs):
    w32_refs, (o_ref, *w16_refs), stream_refs = (
        refs[:n_cast], refs[n_cast:2 * n_cast + 1], refs[2 * n_cast + 1:])
    st = _Stream(*stream_refs)
    i = pl.program_id(1)
    _cast_slabs(w32_refs, w16_refs)
    ko, qo = _tile_iotas()

    def score(g, tile):
        return _scores(k_ref, ek_ref[g, _tile_rows(tile), :], q_ref, eq_ref[g], g, tile)

    diag = [score(g, i) for g in range(ATT_G)]
    for g in range(ATT_G):
        _fold_first(st, g, jnp.where(ko <= qo, diag[g], NEG_INF))

    def fold_past(g, tile):
        s = st.s[g]
        m_new = jnp.maximum(st.m[g], jnp.max(s, axis=0, keepdims=True))
        _fold(st, g, jnp.exp2(s - m_new), m_new)

    _stream_past_tiles(st, i, vt_ref, score, fold_past, o_ref)


def _fox(z4, vt, ek, eq, ws32):
    G, T = ATT_G, ATT_T
    resident = pl.Buffered(1)
    w_specs, w_shapes = _cast_slab_specs(ws32)
    return pl.pallas_call(
        functools.partial(_fox_kernel, len(ws32)),
        grid=(N_HEADS // G, N_TILES),
        in_specs=[
            pl.BlockSpec((None, G, T, HEAD_DIM), lambda h, i: (Q_B_TILE, h, i, 0)),
            pl.BlockSpec((G, T, F_PAD), lambda h, i: (h, i, 0)),
            pl.BlockSpec((None, G, SEQ, HEAD_DIM), lambda h, i: (K_B_TILE, h, 0, 0),
                         pipeline_mode=resident),
            pl.BlockSpec((G, SEQ, F_PAD), lambda h, i: (h, 0, 0), pipeline_mode=resident),
            pl.BlockSpec((None, G, N_TILES, HEAD_DIM, T), lambda h, i: (1, h, 0, 0, 0),
                         pipeline_mode=resident),
            *w_specs,
        ],
        out_specs=[pl.BlockSpec((G, T, HEAD_DIM), lambda h, i: (h, i, 0)), *w_specs],
        out_shape=[jax.ShapeDtypeStruct((N_HEADS, SEQ, HEAD_DIM), BF16), *w_shapes],
        scratch_shapes=list(_stream_scratch()),
        compiler_params=pltpu.CompilerParams(
            dimension_semantics=("arbitrary", "arbitrary"), vmem_limit_bytes=VMEM_LIMIT),
        name="fox",
    )(z4, eq, z4, ek, vt, *ws32)


def _mix_kernel(a_ref, b_ref, ga_ref, gb_ref, x_ref, wpa_ref, wpb_ref, wo_ref, g_ref,
                x1_ref, hm_ref):
    def heads(ref):
        return jnp.concatenate([ref[h] for h in range(N_HEADS)], axis=1)

    def gate(ref):
        cols = [ref[t, h] for t in range(D_MODEL // WIDTH) for h in range(N_HEADS)]
        return jax.nn.sigmoid(jnp.concatenate(cols, axis=1).astype(F32))

    pa = jnp.dot(heads(a_ref), wpa_ref[...], preferred_element_type=F32)
    pb = jnp.dot(heads(b_ref), wpb_ref[...], preferred_element_type=F32)
    merged = gate(ga_ref) * pa + gate(gb_ref) * pb
    x1 = x_ref[...] + jnp.dot(merged.astype(BF16), wo_ref[...], preferred_element_type=F32)
    x1_ref[...] = x1
    hm_ref[...] = _rms_rows(x1, g_ref[...]).astype(BF16)


def _mix(a, b, z4, x2, wpa, wpb, wo, g):
    tm = MIX_TM
    gate_tiles = D_MODEL // IN_TN
    const = lambda i: (0, 0)
    return pl.pallas_call(
        _mix_kernel,
        grid=(SEQ // tm,),
        in_specs=[
            pl.BlockSpec((N_HEADS, tm, HEAD_DIM), lambda i: (0, i, 0)),
            pl.BlockSpec((N_HEADS, tm, HEAD_DIM), lambda i: (0, i, 0)),
            pl.BlockSpec((gate_tiles, N_HEADS, tm, HEAD_DIM),
                         lambda i: (N_QKV_TILES // gate_tiles, 0, i, 0)),
            pl.BlockSpec((gate_tiles, N_HEADS, tm, HEAD_DIM),
                         lambda i: (N_QKV_TILES // gate_tiles + 1, 0, i, 0)),
            pl.BlockSpec((tm, D_MODEL), lambda i: (i, 0)),
            pl.BlockSpec((WIDTH, D_MODEL), const),
            pl.BlockSpec((WIDTH, D_MODEL), const),
            pl.BlockSpec((D_MODEL, D_MODEL), const),
            pl.BlockSpec((1, D_MODEL), const),
        ],
        out_specs=[
            pl.BlockSpec((tm, D_MODEL), lambda i: (i, 0)),
            pl.BlockSpec((tm, D_MODEL), lambda i: (i, 0)),
        ],
        out_shape=[
            jax.ShapeDtypeStruct((SEQ, D_MODEL), F32),
            jax.ShapeDtypeStruct((SEQ, D_MODEL), BF16),
        ],
        compiler_params=pltpu.CompilerParams(
            dimension_semantics=("arbitrary",), vmem_limit_bytes=VMEM_LIMIT),
        name="mix",
    )(a, b, z4, z4, x2, wpa, wpb, wo, g)


def _mlp_up(hm, wu):
    h = jnp.dot(hm, wu, preferred_element_type=F32)
    return jnp.square(jnp.maximum(h, 0.0)).astype(BF16)


def _mlp_head_kernel(hm_ref, wu_ref, h_ref):
    h_ref[...] = _mlp_up(hm_ref[...], wu_ref[...])


def _mlp_kernel(h0_ref, hm_next_ref, x1_ref, wu_next_ref, wd_ref, g_ref, o_ref, acc_ref, h_ref):
    i, c = pl.program_id(0), pl.program_id(1)

    @pl.when((i == 0) & (c == 0))
    def _():
        h_ref[...] = h0_ref[...]

    @pl.when(c == 0)
    def _():
        acc_ref[...] = x1_ref[...]

    acc_ref[...] += jnp.dot(h_ref[...], wd_ref[...], preferred_element_type=F32)
    h_ref[...] = _mlp_up(hm_next_ref[...], wu_next_ref[...])

    @pl.when(c == pl.num_programs(1) - 1)
    def _():
        o_ref[...] = _rms_rows(acc_ref[...], g_ref[...])


def _mlp(hm, x1, wu, wd, g):
    tm, fc = MLP_TM, MLP_FC
    n_i, n_c = SEQ // tm, D_FF // fc
    params = dict(vmem_limit_bytes=VMEM_LIMIT)
    h0 = pl.pallas_call(
        _mlp_head_kernel,
        grid=(1,),
        in_specs=[pl.BlockSpec((tm, D_MODEL), lambda s: (0, 0)),
                  pl.BlockSpec((D_MODEL, fc), lambda s: (0, 0))],
        out_specs=pl.BlockSpec((tm, fc), lambda s: (0, 0)),
        out_shape=jax.ShapeDtypeStruct((tm, fc), BF16),
        compiler_params=pltpu.CompilerParams(dimension_semantics=("arbitrary",), **params),
        name="mlp_head",
    )(hm, wu)

    def next_rows(i, c):
        return jnp.minimum(i + (c + 1) // n_c, n_i - 1)

    return pl.pallas_call(
        _mlp_kernel,
        grid=(n_i, n_c),
        in_specs=[
            pl.BlockSpec((tm, fc), lambda i, c: (0, 0)),
            pl.BlockSpec((tm, D_MODEL), lambda i, c: (next_rows(i, c), 0)),
            pl.BlockSpec((tm, D_MODEL), lambda i, c: (i, 0)),
            pl.BlockSpec((D_MODEL, fc), lambda i, c: (0, (c + 1) % n_c)),
            pl.BlockSpec((fc, D_MODEL), lambda i, c: (c, 0)),
            pl.BlockSpec((1, D_MODEL), lambda i, c: (0, 0)),
        ],
        out_specs=pl.BlockSpec((tm, D_MODEL), lambda i, c: (i, 0)),
        out_shape=jax.ShapeDtypeStruct((SEQ, D_MODEL), F32),
        scratch_shapes=[pltpu.VMEM((tm, D_MODEL), F32), pltpu.VMEM((tm, fc), BF16)],
        compiler_params=pltpu.CompilerParams(
            dimension_semantics=("arbitrary", "arbitrary"), **params),
        name="mlp",
    )(h0, hm, x1, wu, wd, g)


def kernel(x, norm_mix_g, w_in, b_forget, w_proj_a, w_proj_b, w_out, norm_mlp_g, w_up, w_down,
           norm_final_g):
    row = lambda g: g.reshape(1, D_MODEL).astype(F32)

    x2 = x.reshape(SEQ, D_MODEL)
    z4, f, vt = _inproj(x2, row(norm_mix_g), w_in.T)

    b_pad = jnp.pad(b_forget.astype(F32), (0, F_PAD - N_HEADS)).reshape(1, F_PAD)
    ek, eq = _fox_prep(f, b_pad)

    slopes = jnp.exp2(-8.0 * jnp.arange(1, N_HEADS + 1, dtype=F32) / N_HEADS)
    a, w_up16, w_pa16, w_pb16 = _moba(slopes, z4, vt, (w_up, w_proj_a, w_proj_b))
    b, w_down16, w_out16 = _fox(z4, vt, ek, eq, (w_down, w_out))

    x1, hm = _mix(a, b, z4, x2, w_pa16, w_pb16, w_out16, row(norm_mlp_g))
    out = _mlp(hm, x1, w_up16, w_down16, row(norm_final_g))
    return out.reshape(x.shape)
```

```python
import functools
import math
from typing import Any, NamedTuple

import jax
import jax.numpy as jnp
from jax import lax
from jax.experimental import pallas as pl
from jax.experimental.pallas import tpu as pltpu

D_MODEL = 2048
SEQ = 8192
HEAD_DIM = 128
N_HEADS = 8
WIDTH = N_HEADS * HEAD_DIM
MOBA_BLOCK = 256
MOBA_TOPK = 3
N_BLOCKS = SEQ // MOBA_BLOCK
D_FF = 4 * D_MODEL
RMS_EPS = 1e-6
LOG2E = math.log2(math.e)
SCALE2 = HEAD_DIM ** -0.5 * LOG2E
F_PAD = 128

BF16 = jnp.bfloat16
F32 = jnp.float32
NEG_INF = float("-inf")
POS_INF = float("inf")

VMEM_LIMIT = 56 * 1024 * 1024

IN_TM, IN_TN = 1024, 1024
Q_A_TILE, K_A_TILE, V_A_TILE, Q_B_TILE, K_B_TILE, V_B_TILE = range(6)
N_QKV_TILES = 6 * WIDTH // IN_TN
N_GATE_TILES = 2 * D_MODEL // IN_TN
ATT_T = 2 * MOBA_BLOCK
ATT_G = 4
N_SPLIT = 3
N_TILES = SEQ // ATT_T
MIX_TM = 512
MLP_TM, MLP_FC = 512, 1024
ROW_CHUNK = 128
PREP_CHUNK = 256


def _nt_dot(a, b):
    return lax.dot_general(a, b, (((1,), (1,)), ((), ())), preferred_element_type=F32)


def _rms_rows(x, g):
    ms = jnp.mean(x * x, axis=-1, keepdims=True)
    return x * lax.rsqrt(ms + RMS_EPS) * g


def _inproj_kernel(x_ref, g_ref, wt_ref, wft_ref, z_ref, f_ref, vt_ref, hn_ref):
    j = pl.program_id(1)

    @pl.when(j == 0)
    def _():
        wft = wft_ref[...].astype(BF16)

        def body(r, _):
            rows = pl.ds(pl.multiple_of(r * ROW_CHUNK, ROW_CHUNK), ROW_CHUNK)
            hn = _rms_rows(x_ref[rows, :], g_ref[...]).astype(BF16)
            hn_ref[rows, :] = hn
            f_ref[rows, :] = _nt_dot(hn, wft)
            return 0
        lax.fori_loop(0, IN_TM // ROW_CHUNK, body, 0)

    scale = jnp.where((j == Q_A_TILE) | (j == Q_B_TILE), SCALE2, 1.0)
    zf = _nt_dot(hn_ref[...], wt_ref[...].astype(BF16)) * scale
    z = zf.astype(BF16)
    for h in range(IN_TN // HEAD_DIM):
        z_ref[h] = z[:, h * HEAD_DIM:(h + 1) * HEAD_DIM]

    @pl.when((j == V_A_TILE) | (j == V_B_TILE))
    def _():
        for h in range(N_HEADS):
            for t in range(IN_TM // ATT_T):
                blk = zf[t * ATT_T:(t + 1) * ATT_T, h * HEAD_DIM:(h + 1) * HEAD_DIM]
                vt_ref[h, t] = blk.T.astype(BF16)


def _inproj(x2, g, w_in_t):
    n_i = SEQ // IN_TM
    n_j = N_QKV_TILES + N_GATE_TILES
    slabs = IN_TN // HEAD_DIM
    tiles = IN_TM // ATT_T
    f_row = N_QKV_TILES * IN_TN

    def tile_row(i, j):
        return pl.multiple_of(j * IN_TN + jnp.where(j >= N_QKV_TILES, N_HEADS, 0), N_HEADS), 0

    return pl.pallas_call(
        _inproj_kernel,
        grid=(n_i, n_j),
        in_specs=[
            pl.BlockSpec((IN_TM, D_MODEL), lambda i, j: (i, 0)),
            pl.BlockSpec((1, D_MODEL), lambda i, j: (0, 0)),
            pl.BlockSpec((pl.Element(IN_TN), pl.Element(D_MODEL)), tile_row),
            pl.BlockSpec((F_PAD, D_MODEL), lambda i, j: (f_row // F_PAD, 0)),
        ],
        out_specs=[
            pl.BlockSpec((None, slabs, IN_TM, HEAD_DIM), lambda i, j: (j, 0, i, 0)),
            pl.BlockSpec((IN_TM, F_PAD), lambda i, j: (i, 0)),
            pl.BlockSpec((None, N_HEADS, tiles, HEAD_DIM, ATT_T),
                         lambda i, j: (jnp.where(j > V_A_TILE, 1, 0), 0, i, 0, 0)),
        ],
        out_shape=[
            jax.ShapeDtypeStruct((n_j, slabs, SEQ, HEAD_DIM), BF16),
            jax.ShapeDtypeStruct((SEQ, F_PAD), F32),
            jax.ShapeDtypeStruct((2, N_HEADS, N_TILES, HEAD_DIM, ATT_T), BF16),
        ],
        scratch_shapes=[pltpu.VMEM((IN_TM, D_MODEL), BF16)],
        compiler_params=pltpu.CompilerParams(
            dimension_semantics=("arbitrary", "arbitrary"), vmem_limit_bytes=VMEM_LIMIT),
        name="inproj",
    )(x2, g, w_in_t, w_in_t)


def _split(v):
    pieces = []
    for _ in range(N_SPLIT):
        p = v.astype(BF16).astype(F32)
        pieces.append(p)
        v = v - p
    return pieces


def _bias_features(v, lane, pieces_at, const_at, const):
    f = jnp.zeros_like(v)
    for n, p in enumerate(_split(v)):
        f = jnp.where(lane == pieces_at + n, p, f)
    f = jnp.where((lane >= const_at) & (lane < const_at + N_SPLIT), const, f)
    return f.astype(BF16)


def _fox_prep_kernel(f_ref, b_ref, ek_ref, eq_ref):
    n = PREP_CHUNK
    tri = (lax.broadcasted_iota(jnp.int32, (n, n), 0)
           >= lax.broadcasted_iota(jnp.int32, (n, n), 1)).astype(BF16)
    lane = lax.broadcasted_iota(jnp.int32, (n, F_PAD), 1)

    def body(ci, carry):
        rows = pl.ds(pl.multiple_of(ci * n, n), n)
        lf = jax.nn.log_sigmoid(f_ref[rows, :] + b_ref[...])
        cs = carry
        for p in _split(lf):
            cs = cs + jnp.dot(tri, p.astype(BF16), preferred_element_type=F32)
        c2 = cs * LOG2E
        for h in range(N_HEADS):
            col = jnp.broadcast_to(c2[:, h:h + 1], (n, F_PAD))
            ek_ref[h, rows, :] = _bias_features(col, lane, 0, N_SPLIT, 1.0)
            eq_ref[h, rows, :] = _bias_features(col, lane, N_SPLIT, 0, -1.0)
        return cs[n - 1:n, :]

    lax.fori_loop(0, SEQ // n, body, jnp.zeros((1, F_PAD), F32))


def _fox_prep(f, b_pad):
    feat = jax.ShapeDtypeStruct((N_HEADS, SEQ, F_PAD), BF16)
    return pl.pallas_call(
        _fox_prep_kernel,
        out_shape=[feat, feat],
        compiler_params=pltpu.CompilerParams(vmem_limit_bytes=VMEM_LIMIT),
        name="fox_prep",
    )(f, b_pad)


def _tile_iotas():
    ko = lax.broadcasted_iota(jnp.int32, (ATT_T, ATT_T), 0)
    qo = lax.broadcasted_iota(jnp.int32, (ATT_T, ATT_T), 1)
    return ko, qo


def _tile_rows(j):
    return pl.ds(pl.multiple_of(j * ATT_T, ATT_T), ATT_T)


def _scores(k_ref, ek, q_ref, eq, g, j):
    keys = jnp.concatenate([k_ref[g, _tile_rows(j), :], ek], axis=1)
    queries = jnp.concatenate([q_ref[g], eq], axis=1)
    return _nt_dot(keys, queries)


class _Stream(NamedTuple):
    s: Any
    p: Any
    alpha: Any
    m: Any
    l: Any
    acc: Any


def _stream_scratch():
    G, T = ATT_G, ATT_T
    row = pltpu.VMEM((G, 1, T), F32)
    return _Stream(s=pltpu.VMEM((G, T, T), F32), p=pltpu.VMEM((G, T, T), BF16),
                   alpha=row, m=row, l=row, acc=pltpu.VMEM((G, HEAD_DIM, T), F32))


def _fold_first(st, g, s):
    m0 = jnp.max(s, axis=0, keepdims=True)
    p = jnp.exp2(s - m0)
    st.m[g] = m0
    st.l[g] = jnp.sum(p, axis=0, keepdims=True)
    st.p[g] = p.astype(BF16)
    st.alpha[g] = jnp.zeros_like(m0)
    st.acc[g] = jnp.zeros(st.acc.shape[1:], F32)


def _fold(st, g, p, m_new):
    alpha = jnp.exp2(st.m[g] - m_new)
    st.alpha[g] = alpha
    st.l[g] = alpha * st.l[g] + jnp.sum(p, axis=0, keepdims=True)
    st.p[g] = p.astype(BF16)
    st.m[g] = m_new


def _apply(st, g, vt_tile):
    st.acc[g] = st.alpha[g] * st.acc[g] + jnp.dot(vt_tile, st.p[g], preferred_element_type=F32)


def _stream_past_tiles(st, i, vt_ref, score, fold_past, o_ref):
    chains = range(ATT_G)
    for g in chains:
        st.s[g] = score(g, 0)

    def step(r, score_next):
        prev = jnp.where(r == 1, i, r - 2)
        for g in chains:
            _apply(st, g, vt_ref[g, prev])
        for g in chains:
            fold_past(g, r - 1)
        if score_next:
            for g in chains:
                st.s[g] = score(g, r)

    def body(r, _):
        step(r, True)
        return 0
    lax.fori_loop(1, i, body, 0)

    pl.when(i >= 1)(lambda: step(i, False))

    last = jnp.maximum(i - 1, 0)
    for g in chains:
        _apply(st, g, vt_ref[g, last])
        out = st.acc[g] / st.l[g]
        o_ref[g] = out.T.astype(o_ref.dtype)


def _moba_kernel(n_cast, slope_ref, q_ref, k_ref, vt_ref, *refs):
    w32_refs, (o_ref, *w16_refs), rest = refs[:n_cast], refs[n_cast:2 * n_cast + 1], refs[2 * n_cast + 1:]
    kmean_ref, ek_ref, eq_ref, selb_ref, *stream_refs = rest
    st = _Stream(*stream_refs)
    hg = pl.program_id(0)
    i = pl.program_id(1)
    _cast_slabs(w32_refs, w16_refs)
    ko, qo = _tile_iotas()
    half = MOBA_BLOCK
    slope2 = [slope_ref[hg * ATT_G + g] * LOG2E for g in range(ATT_G)]

    @pl.when(i == 0)
    def _():
        off = lax.broadcasted_iota(jnp.int32, (ATT_T, HEAD_DIM), 0).astype(F32)
        lane = lax.broadcasted_iota(jnp.int32, (ATT_T, HEAD_DIM), 1)
        for g in range(ATT_G):
            def body(b, _):
                rows = pl.ds(pl.multiple_of(b * MOBA_BLOCK, MOBA_BLOCK), MOBA_BLOCK)
                kb = k_ref[g, rows, :].astype(F32)
                kmean_ref[g, pl.ds(b, 1), :] = jnp.mean(kb, axis=0, keepdims=True)
                return 0
            lax.fori_loop(0, N_BLOCKS, body, 0)
            ek_ref[g] = _bias_features(off * slope2[g], lane, 0, N_SPLIT, 1.0)
            eq_ref[g] = _bias_features(off * (-slope2[g]), lane, N_SPLIT, 0, 1.0)

    jj = lax.broadcasted_iota(jnp.int32, (N_BLOCKS, ATT_T), 0)
    jjf = jj.astype(F32)
    lane = lax.broadcasted_iota(jnp.int32, (N_BLOCKS, ATT_T), 1)
    own = 2 * i + jnp.where(lane >= half, 1, 0)
    past = jj < own
    for g in range(ATT_G):
        q = q_ref[g]
        km = kmean_ref[g]
        km_hi = km.astype(BF16)
        km_lo = (km - km_hi.astype(F32)).astype(BF16)
        gate = _nt_dot(km_hi, q) + _nt_dot(km_lo, q)
        gm = jnp.where(past, gate, NEG_INF)
        selb = jnp.full((N_BLOCKS, ATT_T), NEG_INF, F32)
        for _ in range(MOBA_TOPK):
            mx = jnp.max(gm, axis=0, keepdims=True)
            first = jnp.min(jnp.where(gm == mx, jjf, float(N_BLOCKS)), axis=0, keepdims=True)
            pick = (jjf == first) & ((mx > NEG_INF) & (mx < POS_INF))
            selb = jnp.where(pick, 0.0, selb)
            gm = jnp.where(pick, NEG_INF, gm)
        selb_ref[g] = selb

    def score(g, tile):
        return _scores(k_ref, ek_ref[g], q_ref, eq_ref[g], g, tile)

    diag = [score(g, i) for g in range(ATT_G)]
    for g in range(ATT_G):
        sel0 = selb_ref[g, pl.ds(2 * i, 1), :]
        vis0 = jnp.where(qo[:1, :] < half, 0.0, sel0)
        s = diag[g] + jnp.where(ko < half, vis0, 0.0)
        _fold_first(st, g, jnp.where(ko <= qo, s, NEG_INF))

    def fold_past(g, t):
        cj = -slope2[g] * (ATT_T * (i - t)).astype(F32)
        r0 = selb_ref[g, pl.ds(2 * t, 1), :]
        r1 = selb_ref[g, pl.ds(2 * t + 1, 1), :]
        b0, b1 = st.s[g, :half, :], st.s[g, half:, :]
        bm = jnp.maximum(jnp.max(b0, axis=0, keepdims=True) + r0,
                         jnp.max(b1, axis=0, keepdims=True) + r1) + cj
        m_new = jnp.maximum(st.m[g], bm)
        t0 = jnp.where(r0 == 0.0, m_new - cj, POS_INF)
        t1 = jnp.where(r1 == 0.0, m_new - cj, POS_INF)
        p = jnp.concatenate([jnp.exp2(b0 - t0), jnp.exp2(b1 - t1)], axis=0)
        _fold(st, g, p, m_new)

    _stream_past_tiles(st, i, vt_ref, score, fold_past, o_ref)


def _cast_slab_specs(ws):
    steps = (N_HEADS // ATT_G) * N_TILES
    specs = [pl.BlockSpec((w.shape[0] // steps, w.shape[1]), lambda h, i: (h * N_TILES + i, 0))
             for w in ws]
    return specs, [jax.ShapeDtypeStruct(w.shape, BF16) for w in ws]


def _cast_slabs(w32_refs, w16_refs):
    for src, dst in zip(w32_refs, w16_refs):
        dst[...] = src[...].astype(BF16)


def _moba(slopes, z4, vt, ws32):
    G, T = ATT_G, ATT_T
    w_specs, w_shapes = _cast_slab_specs(ws32)
    return pl.pallas_call(
        functools.partial(_moba_kernel, len(ws32)),
        grid=(N_HEADS // G, N_TILES),
        in_specs=[
            pl.BlockSpec(memory_space=pltpu.SMEM),
            pl.BlockSpec((None, G, T, HEAD_DIM), lambda h, i: (Q_A_TILE, h, i, 0)),
            pl.BlockSpec((None, G, SEQ, HEAD_DIM), lambda h, i: (K_A_TILE, h, 0, 0),
                         pipeline_mode=pl.Buffered(1)),
            pl.BlockSpec((None, G, N_TILES, HEAD_DIM, T), lambda h, i: (0, h, 0, 0, 0),
                         pipeline_mode=pl.Buffered(1)),
            *w_specs,
        ],
        out_specs=[pl.BlockSpec((G, T, HEAD_DIM), lambda h, i: (h, i, 0)), *w_specs],
        out_shape=[jax.ShapeDtypeStruct((N_HEADS, SEQ, HEAD_DIM), BF16), *w_shapes],
        scratch_shapes=[
            pltpu.VMEM((G, N_BLOCKS, HEAD_DIM), F32),
            pltpu.VMEM((G, T, HEAD_DIM), BF16),
            pltpu.VMEM((G, T, HEAD_DIM), BF16),
            pltpu.VMEM((G, N_BLOCKS, T), F32),
            *_stream_scratch(),
        ],
        compiler_params=pltpu.CompilerParams(
            dimension_semantics=("arbitrary", "arbitrary"), vmem_limit_bytes=VMEM_LIMIT),
        name="moba",
    )(slopes, z4, z4, vt, *ws32)


def _fox_kernel(n_cast, q_ref, eq_ref, k_ref, ek_ref, vt_ref, *refs):
    w32_refs, (o_ref, *w16_refs), stream_refs = (
        refs[:n_cast], refs[n_cast:2 * n_cast + 1], refs[2 * n_cast + 1:])
    st = _Stream(*stream_refs)
    i = pl.program_id(1)
    _cast_slabs(w32_refs, w16_refs)
    ko, qo = _tile_iotas()

    def score(g, tile):
        return _scores(k_ref, ek_ref[g, _tile_rows(tile), :], q_ref, eq_ref[g], g, tile)

    diag = [score(g, i) for g in range(ATT_G)]
    for g in range(ATT_G):
        _fold_first(st, g, jnp.where(ko <= qo, diag[g], NEG_INF))

    def fold_past(g, tile):
        s = st.s[g]
        m_new = jnp.maximum(st.m[g], jnp.max(s, axis=0, keepdims=True))
        _fold(st, g, jnp.exp2(s - m_new), m_new)

    _stream_past_tiles(st, i, vt_ref, score, fold_past, o_ref)


def _fox(z4, vt, ek, eq, ws32):
    G, T = ATT_G, ATT_T
    resident = pl.Buffered(1)
    w_specs, w_shapes = _cast_slab_specs(ws32)
    return pl.pallas_call(
        functools.partial(_fox_kernel, len(ws32)),
        grid=(N_HEADS // G, N_TILES),
        in_specs=[
            pl.BlockSpec((None, G, T, HEAD_DIM), lambda h, i: (Q_B_TILE, h, i, 0)),
            pl.BlockSpec((G, T, F_PAD), lambda h, i: (h, i, 0)),
            pl.BlockSpec((None, G, SEQ, HEAD_DIM), lambda h, i: (K_B_TILE, h, 0, 0),
                         pipeline_mode=resident),
            pl.BlockSpec((G, SEQ, F_PAD), lambda h, i: (h, 0, 0), pipeline_mode=resident),
            pl.BlockSpec((None, G, N_TILES, HEAD_DIM, T), lambda h, i: (1, h, 0, 0, 0),
                         pipeline_mode=resident),
            *w_specs,
        ],
        out_specs=[pl.BlockSpec((G, T, HEAD_DIM), lambda h, i: (h, i, 0)), *w_specs],
        out_shape=[jax.ShapeDtypeStruct((N_HEADS, SEQ, HEAD_DIM), BF16), *w_shapes],
        scratch_shapes=list(_stream_scratch()),
        compiler_params=pltpu.CompilerParams(
            dimension_semantics=("arbitrary", "arbitrary"), vmem_limit_bytes=VMEM_LIMIT),
        name="fox",
    )(z4, eq, z4, ek, vt, *ws32)


def _mix_kernel(a_ref, b_ref, ga_ref, gb_ref, x_ref, wpa_ref, wpb_ref, wo_ref, g_ref,
                x1_ref, hm_ref):
    def heads(ref):
        return jnp.concatenate([ref[h] for h in range(N_HEADS)], axis=1)

    def gate(ref):
        cols = [ref[t, h] for t in range(D_MODEL // WIDTH) for h in range(N_HEADS)]
        return jax.nn.sigmoid(jnp.concatenate(cols, axis=1).astype(F32))

    pa = jnp.dot(heads(a_ref), wpa_ref[...], preferred_element_type=F32)
    pb = jnp.dot(heads(b_ref), wpb_ref[...], preferred_element_type=F32)
    merged = gate(ga_ref) * pa + gate(gb_ref) * pb
    x1 = x_ref[...] + jnp.dot(merged.astype(BF16), wo_ref[...], preferred_element_type=F32)
    x1_ref[...] = x1
    hm_ref[...] = _rms_rows(x1, g_ref[...]).astype(BF16)


def _mix(a, b, z4, x2, wpa, wpb, wo, g):
    tm = MIX_TM
    gate_tiles = D_MODEL // IN_TN
    const = lambda i: (0, 0)
    return pl.pallas_call(
        _mix_kernel,
        grid=(SEQ // tm,),
        in_specs=[
            pl.BlockSpec((N_HEADS, tm, HEAD_DIM), lambda i: (0, i, 0)),
            pl.BlockSpec((N_HEADS, tm, HEAD_DIM), lambda i: (0, i, 0)),
            pl.BlockSpec((gate_tiles, N_HEADS, tm, HEAD_DIM),
                         lambda i: (N_QKV_TILES // gate_tiles, 0, i, 0)),
            pl.BlockSpec((gate_tiles, N_HEADS, tm, HEAD_DIM),
                         lambda i: (N_QKV_TILES // gate_tiles + 1, 0, i, 0)),
            pl.BlockSpec((tm, D_MODEL), lambda i: (i, 0)),
            pl.BlockSpec((WIDTH, D_MODEL), const),
            pl.BlockSpec((WIDTH, D_MODEL), const),
            pl.BlockSpec((D_MODEL, D_MODEL), const),
            pl.BlockSpec((1, D_MODEL), const),
        ],
        out_specs=[
            pl.BlockSpec((tm, D_MODEL), lambda i: (i, 0)),
            pl.BlockSpec((tm, D_MODEL), lambda i: (i, 0)),
        ],
        out_shape=[
            jax.ShapeDtypeStruct((SEQ, D_MODEL), F32),
            jax.ShapeDtypeStruct((SEQ, D_MODEL), BF16),
        ],
        compiler_params=pltpu.CompilerParams(
            dimension_semantics=("arbitrary",), vmem_limit_bytes=VMEM_LIMIT),
        name="mix",
    )(a, b, z4, z4, x2, wpa, wpb, wo, g)


def _mlp_up(hm, wu):
    h = jnp.dot(hm, wu, preferred_element_type=F32)
    return jnp.square(jnp.maximum(h, 0.0)).astype(BF16)


def _mlp_head_kernel(hm_ref, wu_ref, h_ref):
    h_ref[...] = _mlp_up(hm_ref[...], wu_ref[...])


def _mlp_kernel(h0_ref, hm_next_ref, x1_ref, wu_next_ref, wd_ref, g_ref, o_ref, acc_ref, h_ref):
    i, c = pl.program_id(0), pl.program_id(1)

    @pl.when((i == 0) & (c == 0))
    def _():
        h_ref[...] = h0_ref[...]

    @pl.when(c == 0)
    def _():
        acc_ref[...] = x1_ref[...]

    acc_ref[...] += jnp.dot(h_ref[...], wd_ref[...], preferred_element_type=F32)
    h_ref[...] = _mlp_up(hm_next_ref[...], wu_next_ref[...])

    @pl.when(c == pl.num_programs(1) - 1)
    def _():
        o_ref[...] = _rms_rows(acc_ref[...], g_ref[...])


def _mlp(hm, x1, wu, wd, g):
    tm, fc = MLP_TM, MLP_FC
    n_i, n_c = SEQ // tm, D_FF // fc
    params = dict(vmem_limit_bytes=VMEM_LIMIT)
    h0 = pl.pallas_call(
        _mlp_head_kernel,
        grid=(1,),
        in_specs=[pl.BlockSpec((tm, D_MODEL), lambda s: (0, 0)),
                  pl.BlockSpec((D_MODEL, fc), lambda s: (0, 0))],
        out_specs=pl.BlockSpec((tm, fc), lambda s: (0, 0)),
        out_shape=jax.ShapeDtypeStruct((tm, fc), BF16),
        compiler_params=pltpu.CompilerParams(dimension_semantics=("arbitrary",), **params),
        name="mlp_head",
    )(hm, wu)

    def next_rows(i, c):
        return jnp.minimum(i + (c + 1) // n_c, n_i - 1)

    return pl.pallas_call(
        _mlp_kernel,
        grid=(n_i, n_c),
        in_specs=[
            pl.BlockSpec((tm, fc), lambda i, c: (0, 0)),
            pl.BlockSpec((tm, D_MODEL), lambda i, c: (next_rows(i, c), 0)),
            pl.BlockSpec((tm, D_MODEL), lambda i, c: (i, 0)),
            pl.BlockSpec((D_MODEL, fc), lambda i, c: (0, (c + 1) % n_c)),
            pl.BlockSpec((fc, D_MODEL), lambda i, c: (c, 0)),
            pl.BlockSpec((1, D_MODEL), lambda i, c: (0, 0)),
        ],
        out_specs=pl.BlockSpec((tm, D_MODEL), lambda i, c: (i, 0)),
        out_shape=jax.ShapeDtypeStruct((SEQ, D_MODEL), F32),
        scratch_shapes=[pltpu.VMEM((tm, D_MODEL), F32), pltpu.VMEM((tm, fc), BF16)],
        compiler_params=pltpu.CompilerParams(
            dimension_semantics=("arbitrary", "arbitrary"), **params),
        name="mlp",
    )(h0, hm, x1, wu, wd, g)


def kernel(x, norm_mix_g, w_in, b_forget, w_proj_a, w_proj_b, w_out, norm_mlp_g, w_up, w_down,
           norm_final_g):
    row = lambda g: g.reshape(1, D_MODEL).astype(F32)

    x2 = x.reshape(SEQ, D_MODEL)
    z4, f, vt = _inproj(x2, row(norm_mix_g), w_in.T)

    b_pad = jnp.pad(b_forget.astype(F32), (0, F_PAD - N_HEADS)).reshape(1, F_PAD)
    ek, eq = _fox_prep(f, b_pad)

    slopes = jnp.exp2(-8.0 * jnp.arange(1, N_HEADS + 1, dtype=F32) / N_HEADS)
    a, w_up16, w_pa16, w_pb16 = _moba(slopes, z4, vt, (w_up, w_proj_a, w_proj_b))
    b, w_down16, w_out16 = _fox(z4, vt, ek, eq, (w_down, w_out))

    x1, hm = _mix(a, b, z4, x2, w_pa16, w_pb16, w_out16, row(norm_mlp_g))
    out = _mlp(hm, x1, w_up16, w_down16, row(norm_final_g))
    return out.reshape(x.shape)
```

```python
import functools
import math
from typing import Any, NamedTuple

import jax
import jax.numpy as jnp
from jax import lax
from jax.experimental import pallas as pl
from jax.experimental.pallas import tpu as pltpu

D_MODEL = 2048
SEQ = 8192
HEAD_DIM = 128
N_HEADS = 8
WIDTH = N_HEADS * HEAD_DIM
MOBA_BLOCK = 256
MOBA_TOPK = 3
N_BLOCKS = SEQ // MOBA_BLOCK
D_FF = 4 * D_MODEL
RMS_EPS = 1e-6
LOG2E = math.log2(math.e)
SCALE2 = HEAD_DIM ** -0.5 * LOG2E
F_PAD = 128

BF16 = jnp.bfloat16
F32 = jnp.float32
NEG_INF = float("-inf")
POS_INF = float("inf")

VMEM_LIMIT = 56 * 1024 * 1024

IN_TM, IN_TN = 1024, 1024
Q_A_TILE, K_A_TILE, V_A_TILE, Q_B_TILE, K_B_TILE, V_B_TILE = range(6)
N_QKV_TILES = 6 * WIDTH // IN_TN
N_GATE_TILES = 2 * D_MODEL // IN_TN
ATT_T = 2 * MOBA_BLOCK
ATT_G = 4
N_SPLIT = 3
VT_ROWS = HEAD_DIM + 16
N_TILES = SEQ // ATT_T
MIX_TM = 512
MLP_TM, MLP_FC = 512, 1024
ROW_CHUNK = 128
PREP_CHUNK = 256


def _nt_dot(a, b):
    return lax.dot_general(a, b, (((1,), (1,)), ((), ())), preferred_element_type=F32)


def _rms_rows(x, g):
    ms = jnp.mean(x * x, axis=-1, keepdims=True)
    return x * lax.rsqrt(ms + RMS_EPS) * g


def _inproj_kernel(x_ref, g_ref, wt_ref, wft_ref, z_ref, f_ref, vt_ref, hn_ref):
    j = pl.program_id(1)

    @pl.when(j == 0)
    def _():
        wft = wft_ref[...].astype(BF16)

        def body(r, _):
            rows = pl.ds(pl.multiple_of(r * ROW_CHUNK, ROW_CHUNK), ROW_CHUNK)
            hn = _rms_rows(x_ref[rows, :], g_ref[...]).astype(BF16)
            hn_ref[rows, :] = hn
            f_ref[rows, :] = _nt_dot(hn, wft)
            return 0
        lax.fori_loop(0, IN_TM // ROW_CHUNK, body, 0)

    scale = jnp.where((j == Q_A_TILE) | (j == Q_B_TILE), SCALE2, 1.0)
    zf = _nt_dot(hn_ref[...], wt_ref[...].astype(BF16)) * scale
    z = zf.astype(BF16)
    for h in range(IN_TN // HEAD_DIM):
        z_ref[h] = z[:, h * HEAD_DIM:(h + 1) * HEAD_DIM]

    @pl.when((j == V_A_TILE) | (j == V_B_TILE))
    def _():
        ones_row = jnp.where(
            lax.broadcasted_iota(jnp.int32, (VT_ROWS - HEAD_DIM, ATT_T), 0) == 0, 1.0, 0.0)
        for h in range(N_HEADS):
            for t in range(IN_TM // ATT_T):
                blk = zf[t * ATT_T:(t + 1) * ATT_T, h * HEAD_DIM:(h + 1) * HEAD_DIM]
                vt_ref[h, t, :HEAD_DIM, :] = blk.T.astype(BF16)
                vt_ref[h, t, HEAD_DIM:, :] = ones_row.astype(BF16)


def _inproj(x2, g, w_in_t):
    n_i = SEQ // IN_TM
    n_j = N_QKV_TILES + N_GATE_TILES
    slabs = IN_TN // HEAD_DIM
    tiles = IN_TM // ATT_T
    f_row = N_QKV_TILES * IN_TN

    def tile_row(i, j):
        return pl.multiple_of(j * IN_TN + jnp.where(j >= N_QKV_TILES, N_HEADS, 0), N_HEADS), 0

    return pl.pallas_call(
        _inproj_kernel,
        grid=(n_i, n_j),
        in_specs=[
            pl.BlockSpec((IN_TM, D_MODEL), lambda i, j: (i, 0)),
            pl.BlockSpec((1, D_MODEL), lambda i, j: (0, 0)),
            pl.BlockSpec((pl.Element(IN_TN), pl.Element(D_MODEL)), tile_row),
            pl.BlockSpec((F_PAD, D_MODEL), lambda i, j: (f_row // F_PAD, 0)),
        ],
        out_specs=[
            pl.BlockSpec((None, slabs, IN_TM, HEAD_DIM), lambda i, j: (j, 0, i, 0)),
            pl.BlockSpec((IN_TM, F_PAD), lambda i, j: (i, 0)),
            pl.BlockSpec((None, N_HEADS, tiles, VT_ROWS, ATT_T),
                         lambda i, j: (jnp.where(j > V_A_TILE, 1, 0), 0, i, 0, 0)),
        ],
        out_shape=[
            jax.ShapeDtypeStruct((n_j, slabs, SEQ, HEAD_DIM), BF16),
            jax.ShapeDtypeStruct((SEQ, F_PAD), F32),
            jax.ShapeDtypeStruct((2, N_HEADS, N_TILES, VT_ROWS, ATT_T), BF16),
        ],
        scratch_shapes=[pltpu.VMEM((IN_TM, D_MODEL), BF16)],
        compiler_params=pltpu.CompilerParams(
            dimension_semantics=("arbitrary", "arbitrary"), vmem_limit_bytes=VMEM_LIMIT),
        name="inproj",
    )(x2, g, w_in_t, w_in_t)


def _split(v):
    pieces = []
    for _ in range(N_SPLIT):
        p = v.astype(BF16).astype(F32)
        pieces.append(p)
        v = v - p
    return pieces


def _bias_features(v, lane, pieces_at, const_at, const):
    f = jnp.zeros_like(v)
    for n, p in enumerate(_split(v)):
        f = jnp.where(lane == pieces_at + n, p, f)
    f = jnp.where((lane >= const_at) & (lane < const_at + N_SPLIT), const, f)
    return f.astype(BF16)


def _fox_prep_kernel(f_ref, b_ref, ek_ref, eq_ref):
    n = PREP_CHUNK
    tri = (lax.broadcasted_iota(jnp.int32, (n, n), 0)
           >= lax.broadcasted_iota(jnp.int32, (n, n), 1)).astype(BF16)
    lane = lax.broadcasted_iota(jnp.int32, (n, F_PAD), 1)

    def body(ci, carry):
        rows = pl.ds(pl.multiple_of(ci * n, n), n)
        lf = jax.nn.log_sigmoid(f_ref[rows, :] + b_ref[...])
        cs = carry
        for p in _split(lf):
            cs = cs + jnp.dot(tri, p.astype(BF16), preferred_element_type=F32)
        c2 = cs * LOG2E
        for h in range(N_HEADS):
            col = jnp.broadcast_to(c2[:, h:h + 1], (n, F_PAD))
            ek_ref[h, rows, :] = _bias_features(col, lane, 0, N_SPLIT, 1.0)
            eq_ref[h, rows, :] = _bias_features(col, lane, N_SPLIT, 0, -1.0)
        return cs[n - 1:n, :]

    lax.fori_loop(0, SEQ // n, body, jnp.zeros((1, F_PAD), F32))


def _fox_prep(f, b_pad):
    feat = jax.ShapeDtypeStruct((N_HEADS, SEQ, F_PAD), BF16)
    return pl.pallas_call(
        _fox_prep_kernel,
        out_shape=[feat, feat],
        compiler_params=pltpu.CompilerParams(vmem_limit_bytes=VMEM_LIMIT),
        name="fox_prep",
    )(f, b_pad)


def _tile_iotas():
    ko = lax.broadcasted_iota(jnp.int32, (ATT_T, ATT_T), 0)
    qo = lax.broadcasted_iota(jnp.int32, (ATT_T, ATT_T), 1)
    return ko, qo


def _tile_rows(j):
    return pl.ds(pl.multiple_of(j * ATT_T, ATT_T), ATT_T)


def _scores(k_ref, ek, q_ref, eq, g, j):
    keys = jnp.concatenate([k_ref[g, _tile_rows(j), :], ek], axis=1)
    queries = jnp.concatenate([q_ref[g], eq], axis=1)
    return _nt_dot(keys, queries)


class _Stream(NamedTuple):
    s: Any
    p: Any
    alpha: Any
    m: Any
    acc: Any


def _stream_scratch():
    G, T = ATT_G, ATT_T
    row = pltpu.VMEM((G, 1, T), F32)
    return _Stream(s=pltpu.VMEM((G, T, T), F32), p=pltpu.VMEM((G, T, T), BF16),
                   alpha=row, m=row, acc=pltpu.VMEM((G, VT_ROWS, T), F32))


def _fold_first(st, g, s):
    m0 = jnp.max(s, axis=0, keepdims=True)
    st.m[g] = m0
    st.p[g] = jnp.exp2(s - m0).astype(BF16)
    st.alpha[g] = jnp.zeros_like(m0)
    st.acc[g] = jnp.zeros(st.acc.shape[1:], F32)


def _fold(st, g, p, m_new):
    st.alpha[g] = jnp.exp2(st.m[g] - m_new)
    st.p[g] = p.astype(BF16)
    st.m[g] = m_new


def _apply(st, g, vt_tile):
    st.acc[g] = st.alpha[g] * st.acc[g] + jnp.dot(vt_tile, st.p[g], preferred_element_type=F32)


def _stream_past_tiles(st, i, vt_ref, score, fold_past, o_ref):
    chains = range(ATT_G)
    for g in chains:
        st.s[g] = score(g, 0)

    def step(r, score_next):
        prev = jnp.where(r == 1, i, r - 2)
        for g in chains:
            _apply(st, g, vt_ref[g, prev])
        for g in chains:
            fold_past(g, r - 1)
        if score_next:
            for g in chains:
                st.s[g] = score(g, r)

    def body(r, _):
        step(r, True)
        return 0
    lax.fori_loop(1, i, body, 0)

    pl.when(i >= 1)(lambda: step(i, False))

    last = jnp.maximum(i - 1, 0)
    for g in chains:
        _apply(st, g, vt_ref[g, last])
        out = st.acc[g, :HEAD_DIM, :] / st.acc[g, HEAD_DIM:HEAD_DIM + 1, :]
        o_ref[g] = out.T.astype(o_ref.dtype)


def _moba_kernel(n_cast, slope_ref, q_ref, k_ref, vt_ref, *refs):
    w32_refs, (o_ref, *w16_refs), rest = refs[:n_cast], refs[n_cast:2 * n_cast + 1], refs[2 * n_cast + 1:]
    kmean_ref, ek_ref, eq_ref, selb_ref, *stream_refs = rest
    st = _Stream(*stream_refs)
    hg = pl.program_id(0)
    i = pl.program_id(1)
    _cast_slabs(w32_refs, w16_refs)
    ko, qo = _tile_iotas()
    half = MOBA_BLOCK
    slope2 = [slope_ref[hg * ATT_G + g] * LOG2E for g in range(ATT_G)]

    @pl.when(i == 0)
    def _():
        off = lax.broadcasted_iota(jnp.int32, (ATT_T, HEAD_DIM), 0).astype(F32)
        lane = lax.broadcasted_iota(jnp.int32, (ATT_T, HEAD_DIM), 1)
        for g in range(ATT_G):
            def body(b, _):
                rows = pl.ds(pl.multiple_of(b * MOBA_BLOCK, MOBA_BLOCK), MOBA_BLOCK)
                kb = k_ref[g, rows, :].astype(F32)
                kmean_ref[g, pl.ds(b, 1), :] = jnp.mean(kb, axis=0, keepdims=True)
                return 0
            lax.fori_loop(0, N_BLOCKS, body, 0)
            ek_ref[g] = _bias_features(off * slope2[g], lane, 0, N_SPLIT, 1.0)
            eq_ref[g] = _bias_features(off * (-slope2[g]), lane, N_SPLIT, 0, 1.0)

    jj = lax.broadcasted_iota(jnp.int32, (N_BLOCKS, ATT_T), 0)
    jjf = jj.astype(F32)
    lane = lax.broadcasted_iota(jnp.int32, (N_BLOCKS, ATT_T), 1)
    own = 2 * i + jnp.where(lane >= half, 1, 0)
    past = jj < own
    for g in range(ATT_G):
        q = q_ref[g]
        km = kmean_ref[g]
        km_hi = km.astype(BF16)
        km_lo = (km - km_hi.astype(F32)).astype(BF16)
        gate = _nt_dot(km_hi, q) + _nt_dot(km_lo, q)
        gm = jnp.where(past, gate, NEG_INF)
        selb = jnp.full((N_BLOCKS, ATT_T), NEG_INF, F32)
        for _ in range(MOBA_TOPK):
            mx = jnp.max(gm, axis=0, keepdims=True)
            first = jnp.min(jnp.where(gm == mx, jjf, float(N_BLOCKS)), axis=0, keepdims=True)
            pick = (jjf == first) & ((mx > NEG_INF) & (mx < POS_INF))
            selb = jnp.where(pick, 0.0, selb)
            gm = jnp.where(pick, NEG_INF, gm)
        selb_ref[g] = selb

    def score(g, tile):
        return _scores(k_ref, ek_ref[g], q_ref, eq_ref[g], g, tile)

    diag = [score(g, i) for g in range(ATT_G)]
    for g in range(ATT_G):
        sel0 = selb_ref[g, pl.ds(2 * i, 1), :]
        vis0 = jnp.where(qo[:1, :] < half, 0.0, sel0)
        s = diag[g] + jnp.where(ko < half, vis0, 0.0)
        _fold_first(st, g, jnp.where(ko <= qo, s, NEG_INF))

    def fold_past(g, t):
        cj = -slope2[g] * (ATT_T * (i - t)).astype(F32)
        r0 = selb_ref[g, pl.ds(2 * t, 1), :]
        r1 = selb_ref[g, pl.ds(2 * t + 1, 1), :]
        b0, b1 = st.s[g, :half, :], st.s[g, half:, :]
        bm = jnp.maximum(jnp.max(b0, axis=0, keepdims=True) + r0,
                         jnp.max(b1, axis=0, keepdims=True) + r1) + cj
        m_new = jnp.maximum(st.m[g], bm)
        t0 = jnp.where(r0 == 0.0, m_new - cj, POS_INF)
        t1 = jnp.where(r1 == 0.0, m_new - cj, POS_INF)
        p = jnp.concatenate([jnp.exp2(b0 - t0), jnp.exp2(b1 - t1)], axis=0)
        _fold(st, g, p, m_new)

    _stream_past_tiles(st, i, vt_ref, score, fold_past, o_ref)


def _cast_slab_specs(ws):
    steps = (N_HEADS // ATT_G) * N_TILES
    specs = [pl.BlockSpec((w.shape[0] // steps, w.shape[1]), lambda h, i: (h * N_TILES + i, 0))
             for w in ws]
    return specs, [jax.ShapeDtypeStruct(w.shape, BF16) for w in ws]


def _cast_slabs(w32_refs, w16_refs):
    for src, dst in zip(w32_refs, w16_refs):
        dst[...] = src[...].astype(BF16)


def _moba(slopes, z4, vt, ws32):
    G, T = ATT_G, ATT_T
    w_specs, w_shapes = _cast_slab_specs(ws32)
    return pl.pallas_call(
        functools.partial(_moba_kernel, len(ws32)),
        grid=(N_HEADS // G, N_TILES),
        in_specs=[
            pl.BlockSpec(memory_space=pltpu.SMEM),
            pl.BlockSpec((None, G, T, HEAD_DIM), lambda h, i: (Q_A_TILE, h, i, 0)),
            pl.BlockSpec((None, G, SEQ, HEAD_DIM), lambda h, i: (K_A_TILE, h, 0, 0),
                         pipeline_mode=pl.Buffered(1)),
            pl.BlockSpec((None, G, N_TILES, VT_ROWS, T), lambda h, i: (0, h, 0, 0, 0),
                         pipeline_mode=pl.Buffered(1)),
            *w_specs,
        ],
        out_specs=[pl.BlockSpec((G, T, HEAD_DIM), lambda h, i: (h, i, 0)), *w_specs],
        out_shape=[jax.ShapeDtypeStruct((N_HEADS, SEQ, HEAD_DIM), BF16), *w_shapes],
        scratch_shapes=[
            pltpu.VMEM((G, N_BLOCKS, HEAD_DIM), F32),
            pltpu.VMEM((G, T, HEAD_DIM), BF16),
            pltpu.VMEM((G, T, HEAD_DIM), BF16),
            pltpu.VMEM((G, N_BLOCKS, T), F32),
            *_stream_scratch(),
        ],
        compiler_params=pltpu.CompilerParams(
            dimension_semantics=("arbitrary", "arbitrary"), vmem_limit_bytes=VMEM_LIMIT),
        name="moba",
    )(slopes, z4, z4, vt, *ws32)


def _fox_kernel(n_cast, q_ref, eq_ref, k_ref, ek_ref, vt_ref, *refs):
    w32_refs, (o_ref, *w16_refs), stream_refs = (
        refs[:n_cast], refs[n_cast:2 * n_cast + 1], refs[2 * n_cast + 1:])
    st = _Stream(*stream_refs)
    i = pl.program_id(1)
    _cast_slabs(w32_refs, w16_refs)
    ko, qo = _tile_iotas()

    def score(g, tile):
        return _scores(k_ref, ek_ref[g, _tile_rows(tile), :], q_ref, eq_ref[g], g, tile)

    diag = [score(g, i) for g in range(ATT_G)]
    for g in range(ATT_G):
        _fold_first(st, g, jnp.where(ko <= qo, diag[g], NEG_INF))

    def fold_past(g, tile):
        s = st.s[g]
        m_new = jnp.maximum(st.m[g], jnp.max(s, axis=0, keepdims=True))
        _fold(st, g, jnp.exp2(s - m_new), m_new)

    _stream_past_tiles(st, i, vt_ref, score, fold_past, o_ref)


def _fox(z4, vt, ek, eq, ws32):
    G, T = ATT_G, ATT_T
    resident = pl.Buffered(1)
    w_specs, w_shapes = _cast_slab_specs(ws32)
    return pl.pallas_call(
        functools.partial(_fox_kernel, len(ws32)),
        grid=(N_HEADS // G, N_TILES),
        in_specs=[
            pl.BlockSpec((None, G, T, HEAD_DIM), lambda h, i: (Q_B_TILE, h, i, 0)),
            pl.BlockSpec((G, T, F_PAD), lambda h, i: (h, i, 0)),
            pl.BlockSpec((None, G, SEQ, HEAD_DIM), lambda h, i: (K_B_TILE, h, 0, 0),
                         pipeline_mode=resident),
            pl.BlockSpec((G, SEQ, F_PAD), lambda h, i: (h, 0, 0), pipeline_mode=resident),
            pl.BlockSpec((None, G, N_TILES, VT_ROWS, T), lambda h, i: (1, h, 0, 0, 0),
                         pipeline_mode=resident),
            *w_specs,
        ],
        out_specs=[pl.BlockSpec((G, T, HEAD_DIM), lambda h, i: (h, i, 0)), *w_specs],
        out_shape=[jax.ShapeDtypeStruct((N_HEADS, SEQ, HEAD_DIM), BF16), *w_shapes],
        scratch_shapes=list(_stream_scratch()),
        compiler_params=pltpu.CompilerParams(
            dimension_semantics=("arbitrary", "arbitrary"), vmem_limit_bytes=VMEM_LIMIT),
        name="fox",
    )(z4, eq, z4, ek, vt, *ws32)


def _mix_kernel(a_ref, b_ref, ga_ref, gb_ref, x_ref, wpa_ref, wpb_ref, wo_ref, g_ref,
                x1_ref, hm_ref):
    def heads(ref):
        return jnp.concatenate([ref[h] for h in range(N_HEADS)], axis=1)

    def gate(ref):
        cols = [ref[t, h] for t in range(D_MODEL // WIDTH) for h in range(N_HEADS)]
        return jax.nn.sigmoid(jnp.concatenate(cols, axis=1).astype(F32))

    pa = jnp.dot(heads(a_ref), wpa_ref[...], preferred_element_type=F32)
    pb = jnp.dot(heads(b_ref), wpb_ref[...], preferred_element_type=F32)
    merged = gate(ga_ref) * pa + gate(gb_ref) * pb
    x1 = x_ref[...] + jnp.dot(merged.astype(BF16), wo_ref[...], preferred_element_type=F32)
    x1_ref[...] = x1
    hm_ref[...] = _rms_rows(x1, g_ref[...]).astype(BF16)


def _mix(a, b, z4, x2, wpa, wpb, wo, g):
    tm = MIX_TM
    gate_tiles = D_MODEL // IN_TN
    const = lambda i: (0, 0)
    return pl.pallas_call(
        _mix_kernel,
        grid=(SEQ // tm,),
        in_specs=[
            pl.BlockSpec((N_HEADS, tm, HEAD_DIM), lambda i: (0, i, 0)),
            pl.BlockSpec((N_HEADS, tm, HEAD_DIM), lambda i: (0, i, 0)),
            pl.BlockSpec((gate_tiles, N_HEADS, tm, HEAD_DIM),
                         lambda i: (N_QKV_TILES // gate_tiles, 0, i, 0)),
            pl.BlockSpec((gate_tiles, N_HEADS, tm, HEAD_DIM),
                         lambda i: (N_QKV_TILES // gate_tiles + 1, 0, i, 0)),
            pl.BlockSpec((tm, D_MODEL), lambda i: (i, 0)),
            pl.BlockSpec((WIDTH, D_MODEL), const),
            pl.BlockSpec((WIDTH, D_MODEL), const),
            pl.BlockSpec((D_MODEL, D_MODEL), const),
            pl.BlockSpec((1, D_MODEL), const),
        ],
        out_specs=[
            pl.BlockSpec((tm, D_MODEL), lambda i: (i, 0)),
            pl.BlockSpec((tm, D_MODEL), lambda i: (i, 0)),
        ],
        out_shape=[
            jax.ShapeDtypeStruct((SEQ, D_MODEL), F32),
            jax.ShapeDtypeStruct((SEQ, D_MODEL), BF16),
        ],
        compiler_params=pltpu.CompilerParams(
            dimension_semantics=("arbitrary",), vmem_limit_bytes=VMEM_LIMIT),
        name="mix",
    )(a, b, z4, z4, x2, wpa, wpb, wo, g)


def _mlp_up(hm, wu):
    h = jnp.dot(hm, wu, preferred_element_type=F32)
    return jnp.square(jnp.maximum(h, 0.0)).astype(BF16)


def _mlp_head_kernel(hm_ref, wu_ref, h_ref):
    h_ref[...] = _mlp_up(hm_ref[...], wu_ref[...])


def _mlp_kernel(h0_ref, hm_next_ref, x1_ref, wu_next_ref, wd_ref, g_ref, o_ref, acc_ref, h_ref):
    i, c = pl.program_id(0), pl.program_id(1)

    @pl.when((i == 0) & (c == 0))
    def _():
        h_ref[...] = h0_ref[...]

    @pl.when(c == 0)
    def _():
        acc_ref[...] = x1_ref[...]

    acc_ref[...] += jnp.dot(h_ref[...], wd_ref[...], preferred_element_type=F32)
    h_ref[...] = _mlp_up(hm_next_ref[...], wu_next_ref[...])

    @pl.when(c == pl.num_programs(1) - 1)
    def _():
        o_ref[...] = _rms_rows(acc_ref[...], g_ref[...])


def _mlp(hm, x1, wu, wd, g):
    tm, fc = MLP_TM, MLP_FC
    n_i, n_c = SEQ // tm, D_FF // fc
    params = dict(vmem_limit_bytes=VMEM_LIMIT)
    h0 = pl.pallas_call(
        _mlp_head_kernel,
        grid=(1,),
        in_specs=[pl.BlockSpec((tm, D_MODEL), lambda s: (0, 0)),
                  pl.BlockSpec((D_MODEL, fc), lambda s: (0, 0))],
        out_specs=pl.BlockSpec((tm, fc), lambda s: (0, 0)),
        out_shape=jax.ShapeDtypeStruct((tm, fc), BF16),
        compiler_params=pltpu.CompilerParams(dimension_semantics=("arbitrary",), **params),
        name="mlp_head",
    )(hm, wu)

    def next_rows(i, c):
        return jnp.minimum(i + (c + 1) // n_c, n_i - 1)

    return pl.pallas_call(
        _mlp_kernel,
        grid=(n_i, n_c),
        in_specs=[
            pl.BlockSpec((tm, fc), lambda i, c: (0, 0)),
            pl.BlockSpec((tm, D_MODEL), lambda i, c: (next_rows(i, c), 0)),
            pl.BlockSpec((tm, D_MODEL), lambda i, c: (i, 0)),
            pl.BlockSpec((D_MODEL, fc), lambda i, c: (0, (c + 1) % n_c)),
            pl.BlockSpec((fc, D_MODEL), lambda i, c: (c, 0)),
            pl.BlockSpec((1, D_MODEL), lambda i, c: (0, 0)),
        ],
        out_specs=pl.BlockSpec((tm, D_MODEL), lambda i, c: (i, 0)),
        out_shape=jax.ShapeDtypeStruct((SEQ, D_MODEL), F32),
        scratch_shapes=[pltpu.VMEM((tm, D_MODEL), F32), pltpu.VMEM((tm, fc), BF16)],
        compiler_params=pltpu.CompilerParams(
            dimension_semantics=("arbitrary", "arbitrary"), **params),
        name="mlp",
    )(h0, hm, x1, wu, wd, g)


def kernel(x, norm_mix_g, w_in, b_forget, w_proj_a, w_proj_b, w_out, norm_mlp_g, w_up, w_down,
           norm_final_g):
    row = lambda g: g.reshape(1, D_MODEL).astype(F32)

    x2 = x.reshape(SEQ, D_MODEL)
    z4, f, vt = _inproj(x2, row(norm_mix_g), w_in.T)

    b_pad = jnp.pad(b_forget.astype(F32), (0, F_PAD - N_HEADS)).reshape(1, F_PAD)
    ek, eq = _fox_prep(f, b_pad)

    slopes = jnp.exp2(-8.0 * jnp.arange(1, N_HEADS + 1, dtype=F32) / N_HEADS)
    a, w_up16, w_pa16, w_pb16 = _moba(slopes, z4, vt, (w_up, w_proj_a, w_proj_b))
    b, w_down16, w_out16 = _fox(z4, vt, ek, eq, (w_down, w_out))

    x1, hm = _mix(a, b, z4, x2, w_pa16, w_pb16, w_out16, row(norm_mlp_g))
    out = _mlp(hm, x1, w_up16, w_down16, row(norm_final_g))
    return out.reshape(x.shape)
```

```python
import functools
import math
from typing import Any, NamedTuple

import jax
import jax.numpy as jnp
from jax import lax
from jax.experimental import pallas as pl
from jax.experimental.pallas import tpu as pltpu

D_MODEL = 2048
SEQ = 8192
HEAD_DIM = 128
N_HEADS = 8
WIDTH = N_HEADS * HEAD_DIM
MOBA_BLOCK = 256
MOBA_TOPK = 3
N_BLOCKS = SEQ // MOBA_BLOCK
D_FF = 4 * D_MODEL
RMS_EPS = 1e-6
LOG2E = math.log2(math.e)
SCALE2 = HEAD_DIM ** -0.5 * LOG2E
F_PAD = 128

BF16 = jnp.bfloat16
F32 = jnp.float32
NEG_INF = float("-inf")
POS_INF = float("inf")

VMEM_LIMIT = 56 * 1024 * 1024

IN_TM, IN_TN = 1024, 1024
Q_A_TILE, K_A_TILE, V_A_TILE, Q_B_TILE, K_B_TILE, V_B_TILE = range(6)
N_QKV_TILES = 6 * WIDTH // IN_TN
N_GATE_TILES = 2 * D_MODEL // IN_TN
ATT_T = 2 * MOBA_BLOCK
ATT_G = 4
N_SPLIT = 3
VT_ROWS = HEAD_DIM + 16
N_TILES = SEQ // ATT_T
MIX_TM = 512
MLP_TM, MLP_FC = 512, 1024
ROW_CHUNK = 128
PREP_CHUNK = 256


def _nt_dot(a, b):
    return lax.dot_general(a, b, (((1,), (1,)), ((), ())), preferred_element_type=F32)


def _rms_rows(x, g):
    ms = jnp.mean(x * x, axis=-1, keepdims=True)
    return x * lax.rsqrt(ms + RMS_EPS) * g


def _inproj_kernel(x_ref, g_ref, wt_ref, wft_ref, z_ref, f_ref, vt_ref, hn_ref):
    j = pl.program_id(1)

    @pl.when(j == 0)
    def _():
        def body(r, _):
            rows = pl.ds(pl.multiple_of(r * ROW_CHUNK, ROW_CHUNK), ROW_CHUNK)
            hn_ref[rows, :] = _rms_rows(x_ref[rows, :], g_ref[...]).astype(BF16)
            return 0
        lax.fori_loop(0, IN_TM // ROW_CHUNK, body, 0)
        f_ref[...] = _nt_dot(hn_ref[...], wft_ref[...].astype(BF16))

    scale = jnp.where((j == Q_A_TILE) | (j == Q_B_TILE), SCALE2, 1.0)
    zf = _nt_dot(hn_ref[...], wt_ref[...].astype(BF16)) * scale
    z = zf.astype(BF16)
    for h in range(IN_TN // HEAD_DIM):
        z_ref[h] = z[:, h * HEAD_DIM:(h + 1) * HEAD_DIM]

    @pl.when((j == V_A_TILE) | (j == V_B_TILE))
    def _():
        ones_row = jnp.where(
            lax.broadcasted_iota(jnp.int32, (VT_ROWS - HEAD_DIM, ATT_T), 0) == 0, 1.0, 0.0)
        for h in range(N_HEADS):
            for t in range(IN_TM // ATT_T):
                blk = zf[t * ATT_T:(t + 1) * ATT_T, h * HEAD_DIM:(h + 1) * HEAD_DIM]
                vt_ref[h, t, :HEAD_DIM, :] = blk.T.astype(BF16)
                vt_ref[h, t, HEAD_DIM:, :] = ones_row.astype(BF16)


def _inproj(x2, g, w_in_t):
    n_i = SEQ // IN_TM
    n_j = N_QKV_TILES + N_GATE_TILES
    slabs = IN_TN // HEAD_DIM
    tiles = IN_TM // ATT_T
    f_row = N_QKV_TILES * IN_TN

    def tile_row(i, j):
        return pl.multiple_of(j * IN_TN + jnp.where(j >= N_QKV_TILES, N_HEADS, 0), N_HEADS), 0

    return pl.pallas_call(
        _inproj_kernel,
        grid=(n_i, n_j),
        in_specs=[
            pl.BlockSpec((IN_TM, D_MODEL), lambda i, j: (i, 0)),
            pl.BlockSpec((1, D_MODEL), lambda i, j: (0, 0)),
            pl.BlockSpec((pl.Element(IN_TN), pl.Element(D_MODEL)), tile_row),
            pl.BlockSpec((F_PAD, D_MODEL), lambda i, j: (f_row // F_PAD, 0)),
        ],
        out_specs=[
            pl.BlockSpec((None, slabs, IN_TM, HEAD_DIM), lambda i, j: (j, 0, i, 0)),
            pl.BlockSpec((IN_TM, F_PAD), lambda i, j: (i, 0)),
            pl.BlockSpec((None, N_HEADS, tiles, VT_ROWS, ATT_T),
                         lambda i, j: (jnp.where(j > V_A_TILE, 1, 0), 0, i, 0, 0)),
        ],
        out_shape=[
            jax.ShapeDtypeStruct((n_j, slabs, SEQ, HEAD_DIM), BF16),
            jax.ShapeDtypeStruct((SEQ, F_PAD), F32),
            jax.ShapeDtypeStruct((2, N_HEADS, N_TILES, VT_ROWS, ATT_T), BF16),
        ],
        scratch_shapes=[pltpu.VMEM((IN_TM, D_MODEL), BF16)],
        compiler_params=pltpu.CompilerParams(
            dimension_semantics=("arbitrary", "arbitrary"), vmem_limit_bytes=VMEM_LIMIT),
        name="inproj",
    )(x2, g, w_in_t, w_in_t)


def _split(v):
    pieces = []
    for _ in range(N_SPLIT):
        p = v.astype(BF16).astype(F32)
        pieces.append(p)
        v = v - p
    return pieces


def _bias_features(v, lane, pieces_at, const_at, const):
    f = jnp.zeros_like(v)
    for n, p in enumerate(_split(v)):
        f = jnp.where(lane == pieces_at + n, p, f)
    f = jnp.where((lane >= const_at) & (lane < const_at + N_SPLIT), const, f)
    return f.astype(BF16)


def _fox_prep_kernel(f_ref, b_ref, ek_ref, eq_ref):
    n = PREP_CHUNK
    tri = (lax.broadcasted_iota(jnp.int32, (n, n), 0)
           >= lax.broadcasted_iota(jnp.int32, (n, n), 1)).astype(BF16)
    lane = lax.broadcasted_iota(jnp.int32, (n, F_PAD), 1)

    def body(ci, carry):
        rows = pl.ds(pl.multiple_of(ci * n, n), n)
        lf = jax.nn.log_sigmoid(f_ref[rows, :] + b_ref[...])
        cs = carry
        for p in _split(lf):
            cs = cs + jnp.dot(tri, p.astype(BF16), preferred_element_type=F32)
        c2 = cs * LOG2E
        for h in range(N_HEADS):
            col = jnp.broadcast_to(c2[:, h:h + 1], (n, F_PAD))
            ek_ref[h, rows, :] = _bias_features(col, lane, 0, N_SPLIT, 1.0)
            eq_ref[h, rows, :] = _bias_features(col, lane, N_SPLIT, 0, -1.0)
        return cs[n - 1:n, :]

    lax.fori_loop(0, SEQ // n, body, jnp.zeros((1, F_PAD), F32))


def _fox_prep(f, b_pad):
    feat = jax.ShapeDtypeStruct((N_HEADS, SEQ, F_PAD), BF16)
    return pl.pallas_call(
        _fox_prep_kernel,
        out_shape=[feat, feat],
        compiler_params=pltpu.CompilerParams(vmem_limit_bytes=VMEM_LIMIT),
        name="fox_prep",
    )(f, b_pad)


def _tile_iotas():
    ko = lax.broadcasted_iota(jnp.int32, (ATT_T, ATT_T), 0)
    qo = lax.broadcasted_iota(jnp.int32, (ATT_T, ATT_T), 1)
    return ko, qo


def _tile_rows(j):
    return pl.ds(pl.multiple_of(j * ATT_T, ATT_T), ATT_T)


def _scores(k_ref, ek, q_ref, eq, g, j):
    keys = jnp.concatenate([k_ref[g, _tile_rows(j), :], ek], axis=1)
    queries = jnp.concatenate([q_ref[g], eq], axis=1)
    return _nt_dot(keys, queries)


class _Stream(NamedTuple):
    s: Any
    p: Any
    alpha: Any
    m: Any
    acc: Any


def _stream_scratch():
    G, T = ATT_G, ATT_T
    row = pltpu.VMEM((G, 1, T), F32)
    return _Stream(s=pltpu.VMEM((G, T, T), F32), p=pltpu.VMEM((G, T, T), BF16),
                   alpha=row, m=row, acc=pltpu.VMEM((G, VT_ROWS, T), F32))


def _fold_first(st, g, s):
    m0 = jnp.max(s, axis=0, keepdims=True)
    st.m[g] = m0
    st.p[g] = jnp.exp2(s - m0).astype(BF16)
    st.alpha[g] = jnp.zeros_like(m0)
    st.acc[g] = jnp.zeros(st.acc.shape[1:], F32)


def _fold(st, g, p, m_new):
    st.alpha[g] = jnp.exp2(st.m[g] - m_new)
    st.p[g] = p.astype(BF16)
    st.m[g] = m_new


def _apply(st, g, vt_tile):
    st.acc[g] = st.alpha[g] * st.acc[g] + jnp.dot(vt_tile, st.p[g], preferred_element_type=F32)


def _stream_past_tiles(st, i, vt_ref, score, fold_past, o_ref):
    chains = range(ATT_G)
    for g in chains:
        st.s[g] = score(g, 0)

    def step(r, score_next):
        prev = jnp.where(r == 1, i, r - 2)
        for g in chains:
            _apply(st, g, vt_ref[g, prev])
        for g in chains:
            fold_past(g, r - 1)
        if score_next:
            for g in chains:
                st.s[g] = score(g, r)

    def body(r, _):
        step(r, True)
        return 0
    lax.fori_loop(1, i, body, 0)

    pl.when(i >= 1)(lambda: step(i, False))

    last = jnp.maximum(i - 1, 0)
    for g in chains:
        _apply(st, g, vt_ref[g, last])
        out = st.acc[g, :HEAD_DIM, :] / st.acc[g, HEAD_DIM:HEAD_DIM + 1, :]
        o_ref[g] = out.T.astype(o_ref.dtype)


def _moba_kernel(n_cast, slope_ref, q_ref, k_ref, vt_ref, *refs):
    w32_refs, (o_ref, *w16_refs), rest = refs[:n_cast], refs[n_cast:2 * n_cast + 1], refs[2 * n_cast + 1:]
    kmean_ref, ek_ref, eq_ref, selb_ref, *stream_refs = rest
    st = _Stream(*stream_refs)
    hg = pl.program_id(0)
    i = pl.program_id(1)
    _cast_slabs(w32_refs, w16_refs)
    ko, qo = _tile_iotas()
    half = MOBA_BLOCK
    slope2 = [slope_ref[hg * ATT_G + g] * LOG2E for g in range(ATT_G)]

    @pl.when(i == 0)
    def _():
        off = lax.broadcasted_iota(jnp.int32, (ATT_T, HEAD_DIM), 0).astype(F32)
        lane = lax.broadcasted_iota(jnp.int32, (ATT_T, HEAD_DIM), 1)
        for g in range(ATT_G):
            def body(b, _):
                rows = pl.ds(pl.multiple_of(b * MOBA_BLOCK, MOBA_BLOCK), MOBA_BLOCK)
                kb = k_ref[g, rows, :].astype(F32)
                kmean_ref[g, pl.ds(b, 1), :] = jnp.mean(kb, axis=0, keepdims=True)
                return 0
            lax.fori_loop(0, N_BLOCKS, body, 0)
            ek_ref[g] = _bias_features(off * slope2[g], lane, 0, N_SPLIT, 1.0)
            eq_ref[g] = _bias_features(off * (-slope2[g]), lane, N_SPLIT, 0, 1.0)

    jj = lax.broadcasted_iota(jnp.int32, (N_BLOCKS, ATT_T), 0)
    jjf = jj.astype(F32)
    lane = lax.broadcasted_iota(jnp.int32, (N_BLOCKS, ATT_T), 1)
    own = 2 * i + jnp.where(lane >= half, 1, 0)
    past = jj < own
    for g in range(ATT_G):
        q = q_ref[g]
        km = kmean_ref[g]
        km_hi = km.astype(BF16)
        km_lo = (km - km_hi.astype(F32)).astype(BF16)
        gate = _nt_dot(km_hi, q) + _nt_dot(km_lo, q)
        gm = jnp.where(past, gate, NEG_INF)
        selb = jnp.full((N_BLOCKS, ATT_T), NEG_INF, F32)
        for _ in range(MOBA_TOPK):
            mx = jnp.max(gm, axis=0, keepdims=True)
            first = jnp.min(jnp.where(gm == mx, jjf, float(N_BLOCKS)), axis=0, keepdims=True)
            pick = (jjf == first) & ((mx > NEG_INF) & (mx < POS_INF))
            selb = jnp.where(pick, 0.0, selb)
            gm = jnp.where(pick, NEG_INF, gm)
        selb_ref[g] = selb

    def score(g, tile):
        return _scores(k_ref, ek_ref[g], q_ref, eq_ref[g], g, tile)

    diag = [score(g, i) for g in range(ATT_G)]
    for g in range(ATT_G):
        sel0 = selb_ref[g, pl.ds(2 * i, 1), :]
        vis0 = jnp.where(qo[:1, :] < half, 0.0, sel0)
        s = diag[g] + jnp.where(ko < half, vis0, 0.0)
        _fold_first(st, g, jnp.where(ko <= qo, s, NEG_INF))

    def fold_past(g, t):
        cj = -slope2[g] * (ATT_T * (i - t)).astype(F32)
        r0 = selb_ref[g, pl.ds(2 * t, 1), :]
        r1 = selb_ref[g, pl.ds(2 * t + 1, 1), :]
        b0, b1 = st.s[g, :half, :], st.s[g, half:, :]
        bm = jnp.maximum(jnp.max(b0, axis=0, keepdims=True) + r0,
                         jnp.max(b1, axis=0, keepdims=True) + r1) + cj
        m_new = jnp.maximum(st.m[g], bm)
        t0 = jnp.where(r0 == 0.0, m_new - cj, POS_INF)
        t1 = jnp.where(r1 == 0.0, m_new - cj, POS_INF)
        p = jnp.concatenate([jnp.exp2(b0 - t0), jnp.exp2(b1 - t1)], axis=0)
        _fold(st, g, p, m_new)

    _stream_past_tiles(st, i, vt_ref, score, fold_past, o_ref)


def _cast_slab_specs(ws):
    steps = (N_HEADS // ATT_G) * N_TILES
    specs = [pl.BlockSpec((w.shape[0] // steps, w.shape[1]), lambda h, i: (h * N_TILES + i, 0))
             for w in ws]
    return specs, [jax.ShapeDtypeStruct(w.shape, BF16) for w in ws]


def _cast_slabs(w32_refs, w16_refs):
    for src, dst in zip(w32_refs, w16_refs):
        dst[...] = src[...].astype(BF16)


def _moba(slopes, z4, vt, ws32):
    G, T = ATT_G, ATT_T
    w_specs, w_shapes = _cast_slab_specs(ws32)
    return pl.pallas_call(
        functools.partial(_moba_kernel, len(ws32)),
        grid=(N_HEADS // G, N_TILES),
        in_specs=[
            pl.BlockSpec(memory_space=pltpu.SMEM),
            pl.BlockSpec((None, G, T, HEAD_DIM), lambda h, i: (Q_A_TILE, h, i, 0)),
            pl.BlockSpec((None, G, SEQ, HEAD_DIM), lambda h, i: (K_A_TILE, h, 0, 0),
                         pipeline_mode=pl.Buffered(1)),
            pl.BlockSpec((None, G, N_TILES, VT_ROWS, T), lambda h, i: (0, h, 0, 0, 0),
                         pipeline_mode=pl.Buffered(1)),
            *w_specs,
        ],
        out_specs=[pl.BlockSpec((G, T, HEAD_DIM), lambda h, i: (h, i, 0)), *w_specs],
        out_shape=[jax.ShapeDtypeStruct((N_HEADS, SEQ, HEAD_DIM), BF16), *w_shapes],
        scratch_shapes=[
            pltpu.VMEM((G, N_BLOCKS, HEAD_DIM), F32),
            pltpu.VMEM((G, T, HEAD_DIM), BF16),
            pltpu.VMEM((G, T, HEAD_DIM), BF16),
            pltpu.VMEM((G, N_BLOCKS, T), F32),
            *_stream_scratch(),
        ],
        compiler_params=pltpu.CompilerParams(
            dimension_semantics=("arbitrary", "arbitrary"), vmem_limit_bytes=VMEM_LIMIT),
        name="moba",
    )(slopes, z4, z4, vt, *ws32)


def _fox_kernel(n_cast, q_ref, eq_ref, k_ref, ek_ref, vt_ref, *refs):
    w32_refs, (o_ref, *w16_refs), stream_refs = (
        refs[:n_cast], refs[n_cast:2 * n_cast + 1], refs[2 * n_cast + 1:])
    st = _Stream(*stream_refs)
    i = pl.program_id(1)
    _cast_slabs(w32_refs, w16_refs)
    ko, qo = _tile_iotas()

    def score(g, tile):
        return _scores(k_ref, ek_ref[g, _tile_rows(tile), :], q_ref, eq_ref[g], g, tile)

    diag = [score(g, i) for g in range(ATT_G)]
    for g in range(ATT_G):
        _fold_first(st, g, jnp.where(ko <= qo, diag[g], NEG_INF))

    def fold_past(g, tile):
        s = st.s[g]
        m_new = jnp.maximum(st.m[g], jnp.max(s, axis=0, keepdims=True))
        _fold(st, g, jnp.exp2(s - m_new), m_new)

    _stream_past_tiles(st, i, vt_ref, score, fold_past, o_ref)


def _fox(z4, vt, ek, eq, ws32):
    G, T = ATT_G, ATT_T
    resident = pl.Buffered(1)
    w_specs, w_shapes = _cast_slab_specs(ws32)
    return pl.pallas_call(
        functools.partial(_fox_kernel, len(ws32)),
        grid=(N_HEADS // G, N_TILES),
        in_specs=[
            pl.BlockSpec((None, G, T, HEAD_DIM), lambda h, i: (Q_B_TILE, h, i, 0)),
            pl.BlockSpec((G, T, F_PAD), lambda h, i: (h, i, 0)),
            pl.BlockSpec((None, G, SEQ, HEAD_DIM), lambda h, i: (K_B_TILE, h, 0, 0),
                         pipeline_mode=resident),
            pl.BlockSpec((G, SEQ, F_PAD), lambda h, i: (h, 0, 0), pipeline_mode=resident),
            pl.BlockSpec((None, G, N_TILES, VT_ROWS, T), lambda h, i: (1, h, 0, 0, 0),
                         pipeline_mode=resident),
            *w_specs,
        ],
        out_specs=[pl.BlockSpec((G, T, HEAD_DIM), lambda h, i: (h, i, 0)), *w_specs],
        out_shape=[jax.ShapeDtypeStruct((N_HEADS, SEQ, HEAD_DIM), BF16), *w_shapes],
        scratch_shapes=list(_stream_scratch()),
        compiler_params=pltpu.CompilerParams(
            dimension_semantics=("arbitrary", "arbitrary"), vmem_limit_bytes=VMEM_LIMIT),
        name="fox",
    )(z4, eq, z4, ek, vt, *ws32)


def _mix_kernel(a_ref, b_ref, ga_ref, gb_ref, x_ref, wpa_ref, wpb_ref, wo_ref, g_ref,
                x1_ref, hm_ref):
    def heads(ref):
        return jnp.concatenate([ref[h] for h in range(N_HEADS)], axis=1)

    def gate(ref):
        cols = [ref[t, h] for t in range(D_MODEL // WIDTH) for h in range(N_HEADS)]
        return jax.nn.sigmoid(jnp.concatenate(cols, axis=1).astype(F32))

    pa = jnp.dot(heads(a_ref), wpa_ref[...], preferred_element_type=F32)
    pb = jnp.dot(heads(b_ref), wpb_ref[...], preferred_element_type=F32)
    merged = gate(ga_ref) * pa + gate(gb_ref) * pb
    x1 = x_ref[...] + jnp.dot(merged.astype(BF16), wo_ref[...], preferred_element_type=F32)
    x1_ref[...] = x1
    hm_ref[...] = _rms_rows(x1, g_ref[...]).astype(BF16)


def _mix(a, b, z4, x2, wpa, wpb, wo, g):
    tm = MIX_TM
    gate_tiles = D_MODEL // IN_TN
    const = lambda i: (0, 0)
    return pl.pallas_call(
        _mix_kernel,
        grid=(SEQ // tm,),
        in_specs=[
            pl.BlockSpec((N_HEADS, tm, HEAD_DIM), lambda i: (0, i, 0)),
            pl.BlockSpec((N_HEADS, tm, HEAD_DIM), lambda i: (0, i, 0)),
            pl.BlockSpec((gate_tiles, N_HEADS, tm, HEAD_DIM),
                         lambda i: (N_QKV_TILES // gate_tiles, 0, i, 0)),
            pl.BlockSpec((gate_tiles, N_HEADS, tm, HEAD_DIM),
                         lambda i: (N_QKV_TILES // gate_tiles + 1, 0, i, 0)),
            pl.BlockSpec((tm, D_MODEL), lambda i: (i, 0)),
            pl.BlockSpec((WIDTH, D_MODEL), const),
            pl.BlockSpec((WIDTH, D_MODEL), const),
            pl.BlockSpec((D_MODEL, D_MODEL), const),
            pl.BlockSpec((1, D_MODEL), const),
        ],
        out_specs=[
            pl.BlockSpec((tm, D_MODEL), lambda i: (i, 0)),
            pl.BlockSpec((tm, D_MODEL), lambda i: (i, 0)),
        ],
        out_shape=[
            jax.ShapeDtypeStruct((SEQ, D_MODEL), F32),
            jax.ShapeDtypeStruct((SEQ, D_MODEL), BF16),
        ],
        compiler_params=pltpu.CompilerParams(
            dimension_semantics=("arbitrary",), vmem_limit_bytes=VMEM_LIMIT),
        name="mix",
    )(a, b, z4, z4, x2, wpa, wpb, wo, g)


def _mlp_up(hm, wu):
    h = jnp.dot(hm, wu, preferred_element_type=F32)
    return jnp.square(jnp.maximum(h, 0.0)).astype(BF16)


def _mlp_head_kernel(hm_ref, wu_ref, h_ref):
    h_ref[...] = _mlp_up(hm_ref[...], wu_ref[...])


def _mlp_kernel(h0_ref, hm_next_ref, x1_ref, wu_next_ref, wd_ref, g_ref, o_ref, acc_ref, h_ref):
    i, c = pl.program_id(0), pl.program_id(1)

    @pl.when((i == 0) & (c == 0))
    def _():
        h_ref[...] = h0_ref[...]

    @pl.when(c == 0)
    def _():
        acc_ref[...] = x1_ref[...]

    acc_ref[...] += jnp.dot(h_ref[...], wd_ref[...], preferred_element_type=F32)
    h_ref[...] = _mlp_up(hm_next_ref[...], wu_next_ref[...])

    @pl.when(c == pl.num_programs(1) - 1)
    def _():
        o_ref[...] = _rms_rows(acc_ref[...], g_ref[...])


def _mlp(hm, x1, wu, wd, g):
    tm, fc = MLP_TM, MLP_FC
    n_i, n_c = SEQ // tm, D_FF // fc
    params = dict(vmem_limit_bytes=VMEM_LIMIT)
    h0 = pl.pallas_call(
        _mlp_head_kernel,
        grid=(1,),
        in_specs=[pl.BlockSpec((tm, D_MODEL), lambda s: (0, 0)),
                  pl.BlockSpec((D_MODEL, fc), lambda s: (0, 0))],
        out_specs=pl.BlockSpec((tm, fc), lambda s: (0, 0)),
        out_shape=jax.ShapeDtypeStruct((tm, fc), BF16),
        compiler_params=pltpu.CompilerParams(dimension_semantics=("arbitrary",), **params),
        name="mlp_head",
    )(hm, wu)

    def next_rows(i, c):
        return jnp.minimum(i + (c + 1) // n_c, n_i - 1)

    return pl.pallas_call(
        _mlp_kernel,
        grid=(n_i, n_c),
        in_specs=[
            pl.BlockSpec((tm, fc), lambda i, c: (0, 0)),
            pl.BlockSpec((tm, D_MODEL), lambda i, c: (next_rows(i, c), 0)),
            pl.BlockSpec((tm, D_MODEL), lambda i, c: (i, 0)),
            pl.BlockSpec((D_MODEL, fc), lambda i, c: (0, (c + 1) % n_c)),
            pl.BlockSpec((fc, D_MODEL), lambda i, c: (c, 0)),
            pl.BlockSpec((1, D_MODEL), lambda i, c: (0, 0)),
        ],
        out_specs=pl.BlockSpec((tm, D_MODEL), lambda i, c: (i, 0)),
        out_shape=jax.ShapeDtypeStruct((SEQ, D_MODEL), F32),
        scratch_shapes=[pltpu.VMEM((tm, D_MODEL), F32), pltpu.VMEM((tm, fc), BF16)],
        compiler_params=pltpu.CompilerParams(
            dimension_semantics=("arbitrary", "arbitrary"), **params),
        name="mlp",
    )(h0, hm, x1, wu, wd, g)


def kernel(x, norm_mix_g, w_in, b_forget, w_proj_a, w_proj_b, w_out, norm_mlp_g, w_up, w_down,
           norm_final_g):
    row = lambda g: g.reshape(1, D_MODEL).astype(F32)

    x2 = x.reshape(SEQ, D_MODEL)
    z4, f, vt = _inproj(x2, row(norm_mix_g), w_in.T)

    b_pad = jnp.pad(b_forget.astype(F32), (0, F_PAD - N_HEADS)).reshape(1, F_PAD)
    ek, eq = _fox_prep(f, b_pad)

    slopes = jnp.exp2(-8.0 * jnp.arange(1, N_HEADS + 1, dtype=F32) / N_HEADS)
    a, w_up16, w_pa16, w_pb16 = _moba(slopes, z4, vt, (w_up, w_proj_a, w_proj_b))
    b, w_down16, w_out16 = _fox(z4, vt, ek, eq, (w_down, w_out))

    x1, hm = _mix(a, b, z4, x2, w_pa16, w_pb16, w_out16, row(norm_mlp_g))
    out = _mlp(hm, x1, w_up16, w_down16, row(norm_final_g))
    return out.reshape(x.shape)
```

```python
import functools
import math
from typing import Any, NamedTuple

import jax
import jax.numpy as jnp
from jax import lax
from jax.experimental import pallas as pl
from jax.experimental.pallas import tpu as pltpu

D_MODEL = 2048
SEQ = 8192
HEAD_DIM = 128
N_HEADS = 8
WIDTH = N_HEADS * HEAD_DIM
MOBA_BLOCK = 256
MOBA_TOPK = 3
N_BLOCKS = SEQ // MOBA_BLOCK
D_FF = 4 * D_MODEL
RMS_EPS = 1e-6
LOG2E = math.log2(math.e)
SCALE2 = HEAD_DIM ** -0.5 * LOG2E
F_PAD = 128

BF16 = jnp.bfloat16
F32 = jnp.float32
NEG_INF = float("-inf")
POS_INF = float("inf")

VMEM_LIMIT = 56 * 1024 * 1024

IN_TM, IN_TN = 1024, 1024
Q_A_TILE, K_A_TILE, V_A_TILE, Q_B_TILE, K_B_TILE, V_B_TILE = range(6)
N_QKV_TILES = 6 * WIDTH // IN_TN
N_GATE_TILES = 2 * D_MODEL // IN_TN
ATT_T = 2 * MOBA_BLOCK
ATT_G = 4
N_SPLIT = 3
VT_ROWS = HEAD_DIM + 16
N_TILES = SEQ // ATT_T
MIX_TM = 512
MLP_TM, MLP_FC = 512, 1024
ROW_CHUNK = 128
PREP_CHUNK = 256


def _nt_dot(a, b):
    return lax.dot_general(a, b, (((1,), (1,)), ((), ())), preferred_element_type=F32)


def _rms_rows(x, g):
    ms = jnp.mean(x * x, axis=-1, keepdims=True)
    return x * lax.rsqrt(ms + RMS_EPS) * g


def _inproj_kernel(x_ref, g_ref, wt_ref, wft_ref, z_ref, f_ref, vt_ref, hn_ref):
    j = pl.program_id(1)

    @pl.when(j == 0)
    def _():
        def body(r, _):
            rows = pl.ds(pl.multiple_of(r * ROW_CHUNK, ROW_CHUNK), ROW_CHUNK)
            hn_ref[rows, :] = _rms_rows(x_ref[rows, :], g_ref[...]).astype(BF16)
            return 0
        lax.fori_loop(0, IN_TM // ROW_CHUNK, body, 0)
        f_ref[...] = _nt_dot(hn_ref[...], wft_ref[...].astype(BF16))

    scale = jnp.where((j == Q_A_TILE) | (j == Q_B_TILE), SCALE2, 1.0)
    zf = _nt_dot(hn_ref[...], wt_ref[...].astype(BF16)) * scale
    z = zf.astype(BF16)
    for h in range(IN_TN // HEAD_DIM):
        z_ref[h] = z[:, h * HEAD_DIM:(h + 1) * HEAD_DIM]

    @pl.when((j == V_A_TILE) | (j == V_B_TILE))
    def _():
        ones_row = jnp.where(
            lax.broadcasted_iota(jnp.int32, (VT_ROWS - HEAD_DIM, ATT_T), 0) == 0, 1.0, 0.0)
        for h in range(N_HEADS):
            for t in range(IN_TM // ATT_T):
                blk = zf[t * ATT_T:(t + 1) * ATT_T, h * HEAD_DIM:(h + 1) * HEAD_DIM]
                vt_ref[h, t, :HEAD_DIM, :] = blk.T.astype(BF16)
                vt_ref[h, t, HEAD_DIM:, :] = ones_row.astype(BF16)


def _inproj(x2, g, w_in_t):
    n_i = SEQ // IN_TM
    n_j = N_QKV_TILES + N_GATE_TILES
    slabs = IN_TN // HEAD_DIM
    tiles = IN_TM // ATT_T
    f_row = N_QKV_TILES * IN_TN

    def tile_row(i, j):
        return pl.multiple_of(j * IN_TN + jnp.where(j >= N_QKV_TILES, N_HEADS, 0), N_HEADS), 0

    return pl.pallas_call(
        _inproj_kernel,
        grid=(n_i, n_j),
        in_specs=[
            pl.BlockSpec((IN_TM, D_MODEL), lambda i, j: (i, 0)),
            pl.BlockSpec((1, D_MODEL), lambda i, j: (0, 0)),
            pl.BlockSpec((pl.Element(IN_TN), pl.Element(D_MODEL)), tile_row),
            pl.BlockSpec((F_PAD, D_MODEL), lambda i, j: (f_row // F_PAD, 0)),
        ],
        out_specs=[
            pl.BlockSpec((None, slabs, IN_TM, HEAD_DIM), lambda i, j: (j, 0, i, 0)),
            pl.BlockSpec((IN_TM, F_PAD), lambda i, j: (i, 0)),
            pl.BlockSpec((None, N_HEADS, tiles, VT_ROWS, ATT_T),
                         lambda i, j: (jnp.where(j > V_A_TILE, 1, 0), 0, i, 0, 0)),
        ],
        out_shape=[
            jax.ShapeDtypeStruct((n_j, slabs, SEQ, HEAD_DIM), BF16),
            jax.ShapeDtypeStruct((SEQ, F_PAD), F32),
            jax.ShapeDtypeStruct((2, N_HEADS, N_TILES, VT_ROWS, ATT_T), BF16),
        ],
        scratch_shapes=[pltpu.VMEM((IN_TM, D_MODEL), BF16)],
        compiler_params=pltpu.CompilerParams(
            dimension_semantics=("arbitrary", "arbitrary"), vmem_limit_bytes=VMEM_LIMIT),
        name="inproj",
    )(x2, g, w_in_t, w_in_t)


def _split(v):
    pieces = []
    for _ in range(N_SPLIT):
        p = v.astype(BF16).astype(F32)
        pieces.append(p)
        v = v - p
    return pieces


def _bias_features(v, lane, pieces_at, const_at, const):
    f = jnp.zeros_like(v)
    for n, p in enumerate(_split(v)):
        f = jnp.where(lane == pieces_at + n, p, f)
    f = jnp.where((lane >= const_at) & (lane < const_at + N_SPLIT), const, f)
    return f.astype(BF16)


def _fox_prep_kernel(f_ref, b_ref, ek_ref, eq_ref):
    n = PREP_CHUNK
    tri = (lax.broadcasted_iota(jnp.int32, (n, n), 0)
           >= lax.broadcasted_iota(jnp.int32, (n, n), 1)).astype(BF16)
    lane = lax.broadcasted_iota(jnp.int32, (n, F_PAD), 1)

    def body(ci, carry):
        rows = pl.ds(pl.multiple_of(ci * n, n), n)
        lf = jax.nn.log_sigmoid(f_ref[rows, :] + b_ref[...])
        cs = carry
        for p in _split(lf):
            cs = cs + jnp.dot(tri, p.astype(BF16), preferred_element_type=F32)
        c2 = cs * LOG2E
        for h in range(N_HEADS):
            col = jnp.broadcast_to(c2[:, h:h + 1], (n, F_PAD))
            ek_ref[h, rows, :] = _bias_features(col, lane, 0, N_SPLIT, 1.0)
            eq_ref[h, rows, :] = _bias_features(col, lane, N_SPLIT, 0, -1.0)
        return cs[n - 1:n, :]

    lax.fori_loop(0, SEQ // n, body, jnp.zeros((1, F_PAD), F32))


def _fox_prep(f, b_pad):
    feat = jax.ShapeDtypeStruct((N_HEADS, SEQ, F_PAD), BF16)
    return pl.pallas_call(
        _fox_prep_kernel,
        out_shape=[feat, feat],
        compiler_params=pltpu.CompilerParams(vmem_limit_bytes=VMEM_LIMIT),
        name="fox_prep",
    )(f, b_pad)


def _tile_iotas():
    ko = lax.broadcasted_iota(jnp.int32, (ATT_T, ATT_T), 0)
    qo = lax.broadcasted_iota(jnp.int32, (ATT_T, ATT_T), 1)
    return ko, qo


def _tile_rows(j):
    return pl.ds(pl.multiple_of(j * ATT_T, ATT_T), ATT_T)


def _scores(k_ref, ek, q_ref, eq, g, j):
    keys = jnp.concatenate([k_ref[g, _tile_rows(j), :], ek], axis=1)
    queries = jnp.concatenate([q_ref[g], eq], axis=1)
    return _nt_dot(keys, queries)


class _Stream(NamedTuple):
    s: Any
    p: Any
    alpha: Any
    m: Any
    acc: Any


def _stream_scratch():
    G, T = ATT_G, ATT_T
    row = pltpu.VMEM((G, 1, T), F32)
    return _Stream(s=pltpu.VMEM((G, T, T), F32), p=pltpu.VMEM((G, T, T), BF16),
                   alpha=row, m=row, acc=pltpu.VMEM((G, VT_ROWS, T), F32))


def _fold_first(st, g, s):
    m0 = jnp.max(s, axis=0, keepdims=True)
    st.m[g] = m0
    st.p[g] = jnp.exp2(s - m0).astype(BF16)
    st.alpha[g] = jnp.zeros_like(m0)
    st.acc[g] = jnp.zeros(st.acc.shape[1:], F32)


def _fold(st, g, p, m_new):
    st.alpha[g] = jnp.exp2(st.m[g] - m_new)
    st.p[g] = p.astype(BF16)
    st.m[g] = m_new


def _apply(st, g, vt_tile):
    st.acc[g] = st.alpha[g] * st.acc[g] + jnp.dot(vt_tile, st.p[g], preferred_element_type=F32)


def _stream_past_tiles(st, i, vt_ref, score, fold_past, o_ref):
    chains = range(ATT_G)
    for g in chains:
        st.s[g] = score(g, 0)

    def step(r, score_next):
        prev = jnp.where(r == 1, i, r - 2)
        for g in chains:
            _apply(st, g, vt_ref[g, prev])
        for g in chains:
            fold_past(g, r - 1)
        if score_next:
            for g in chains:
                st.s[g] = score(g, r)

    n_full = jnp.maximum(i - 1, 0)

    def body(u, _):
        step(2 * u + 1, True)
        step(2 * u + 2, True)
        return 0
    lax.fori_loop(0, n_full // 2, body, 0)

    pl.when(n_full % 2 == 1)(lambda: step(n_full, True))
    pl.when(i >= 1)(lambda: step(i, False))

    last = jnp.maximum(i - 1, 0)
    for g in chains:
        _apply(st, g, vt_ref[g, last])
        out = st.acc[g, :HEAD_DIM, :] / st.acc[g, HEAD_DIM:HEAD_DIM + 1, :]
        o_ref[g] = out.T.astype(o_ref.dtype)


def _moba_kernel(n_cast, slope_ref, q_ref, k_ref, vt_ref, *refs):
    w32_refs, (o_ref, *w16_refs), rest = refs[:n_cast], refs[n_cast:2 * n_cast + 1], refs[2 * n_cast + 1:]
    kmean_ref, ek_ref, eq_ref, selb_ref, *stream_refs = rest
    st = _Stream(*stream_refs)
    hg = pl.program_id(0)
    i = pl.program_id(1)
    _cast_slabs(w32_refs, w16_refs)
    ko, qo = _tile_iotas()
    half = MOBA_BLOCK
    slope2 = [slope_ref[hg * ATT_G + g] * LOG2E for g in range(ATT_G)]

    @pl.when(i == 0)
    def _():
        off = lax.broadcasted_iota(jnp.int32, (ATT_T, HEAD_DIM), 0).astype(F32)
        lane = lax.broadcasted_iota(jnp.int32, (ATT_T, HEAD_DIM), 1)
        for g in range(ATT_G):
            def body(b, _):
                rows = pl.ds(pl.multiple_of(b * MOBA_BLOCK, MOBA_BLOCK), MOBA_BLOCK)
                kb = k_ref[g, rows, :].astype(F32)
                kmean_ref[g, pl.ds(b, 1), :] = jnp.mean(kb, axis=0, keepdims=True)
                return 0
            lax.fori_loop(0, N_BLOCKS, body, 0)
            ek_ref[g] = _bias_features(off * slope2[g], lane, 0, N_SPLIT, 1.0)
            eq_ref[g] = _bias_features(off * (-slope2[g]), lane, N_SPLIT, 0, 1.0)

    jj = lax.broadcasted_iota(jnp.int32, (N_BLOCKS, ATT_T), 0)
    jjf = jj.astype(F32)
    lane = lax.broadcasted_iota(jnp.int32, (N_BLOCKS, ATT_T), 1)
    own = 2 * i + jnp.where(lane >= half, 1, 0)
    past = jj < own
    for g in range(ATT_G):
        q = q_ref[g]
        km = kmean_ref[g]
        km_hi = km.astype(BF16)
        km_lo = (km - km_hi.astype(F32)).astype(BF16)
        gate = _nt_dot(km_hi, q) + _nt_dot(km_lo, q)
        gm = jnp.where(past, gate, NEG_INF)
        selb = jnp.full((N_BLOCKS, ATT_T), NEG_INF, F32)
        for _ in range(MOBA_TOPK):
            mx = jnp.max(gm, axis=0, keepdims=True)
            first = jnp.min(jnp.where(gm == mx, jjf, float(N_BLOCKS)), axis=0, keepdims=True)
            pick = (jjf == first) & ((mx > NEG_INF) & (mx < POS_INF))
            selb = jnp.where(pick, 0.0, selb)
            gm = jnp.where(pick, NEG_INF, gm)
        selb_ref[g] = selb

    def score(g, tile):
        return _scores(k_ref, ek_ref[g], q_ref, eq_ref[g], g, tile)

    diag = [score(g, i) for g in range(ATT_G)]
    for g in range(ATT_G):
        sel0 = selb_ref[g, pl.ds(2 * i, 1), :]
        vis0 = jnp.where(qo[:1, :] < half, 0.0, sel0)
        s = diag[g] + jnp.where(ko < half, vis0, 0.0)
        _fold_first(st, g, jnp.where(ko <= qo, s, NEG_INF))

    def fold_past(g, t):
        cj = -slope2[g] * (ATT_T * (i - t)).astype(F32)
        r0 = selb_ref[g, pl.ds(2 * t, 1), :]
        r1 = selb_ref[g, pl.ds(2 * t + 1, 1), :]
        b0, b1 = st.s[g, :half, :], st.s[g, half:, :]
        bm = jnp.maximum(jnp.max(b0, axis=0, keepdims=True) + r0,
                         jnp.max(b1, axis=0, keepdims=True) + r1) + cj
        m_new = jnp.maximum(st.m[g], bm)
        t0 = jnp.where(r0 == 0.0, m_new - cj, POS_INF)
        t1 = jnp.where(r1 == 0.0, m_new - cj, POS_INF)
        p = jnp.concatenate([jnp.exp2(b0 - t0), jnp.exp2(b1 - t1)], axis=0)
        _fold(st, g, p, m_new)

    _stream_past_tiles(st, i, vt_ref, score, fold_past, o_ref)


def _cast_slab_specs(ws):
    steps = (N_HEADS // ATT_G) * N_TILES
    specs = [pl.BlockSpec((w.shape[0] // steps, w.shape[1]), lambda h, i: (h * N_TILES + i, 0))
             for w in ws]
    return specs, [jax.ShapeDtypeStruct(w.shape, BF16) for w in ws]


def _cast_slabs(w32_refs, w16_refs):
    for src, dst in zip(w32_refs, w16_refs):
        dst[...] = src[...].astype(BF16)


def _moba(slopes, z4, vt, ws32):
    G, T = ATT_G, ATT_T
    w_specs, w_shapes = _cast_slab_specs(ws32)
    return pl.pallas_call(
        functools.partial(_moba_kernel, len(ws32)),
        grid=(N_HEADS // G, N_TILES),
        in_specs=[
            pl.BlockSpec(memory_space=pltpu.SMEM),
            pl.BlockSpec((None, G, T, HEAD_DIM), lambda h, i: (Q_A_TILE, h, i, 0)),
            pl.BlockSpec((None, G, SEQ, HEAD_DIM), lambda h, i: (K_A_TILE, h, 0, 0),
                         pipeline_mode=pl.Buffered(1)),
            pl.BlockSpec((None, G, N_TILES, VT_ROWS, T), lambda h, i: (0, h, 0, 0, 0),
                         pipeline_mode=pl.Buffered(1)),
            *w_specs,
        ],
        out_specs=[pl.BlockSpec((G, T, HEAD_DIM), lambda h, i: (h, i, 0)), *w_specs],
        out_shape=[jax.ShapeDtypeStruct((N_HEADS, SEQ, HEAD_DIM), BF16), *w_shapes],
        scratch_shapes=[
            pltpu.VMEM((G, N_BLOCKS, HEAD_DIM), F32),
            pltpu.VMEM((G, T, HEAD_DIM), BF16),
            pltpu.VMEM((G, T, HEAD_DIM), BF16),
            pltpu.VMEM((G, N_BLOCKS, T), F32),
            *_stream_scratch(),
        ],
        compiler_params=pltpu.CompilerParams(
            dimension_semantics=("arbitrary", "arbitrary"), vmem_limit_bytes=VMEM_LIMIT),
        name="moba",
    )(slopes, z4, z4, vt, *ws32)


def _fox_kernel(n_cast, q_ref, eq_ref, k_ref, ek_ref, vt_ref, *refs):
    w32_refs, (o_ref, *w16_refs), stream_refs = (
        refs[:n_cast], refs[n_cast:2 * n_cast + 1], refs[2 * n_cast + 1:])
    st = _Stream(*stream_refs)
    i = pl.program_id(1)
    _cast_slabs(w32_refs, w16_refs)
    ko, qo = _tile_iotas()

    def score(g, tile):
        return _scores(k_ref, ek_ref[g, _tile_rows(tile), :], q_ref, eq_ref[g], g, tile)

    diag = [score(g, i) for g in range(ATT_G)]
    for g in range(ATT_G):
        _fold_first(st, g, jnp.where(ko <= qo, diag[g], NEG_INF))

    def fold_past(g, tile):
        s = st.s[g]
        m_new = jnp.maximum(st.m[g], jnp.max(s, axis=0, keepdims=True))
        _fold(st, g, jnp.exp2(s - m_new), m_new)

    _stream_past_tiles(st, i, vt_ref, score, fold_past, o_ref)


def _fox(z4, vt, ek, eq, ws32):
    G, T = ATT_G, ATT_T
    resident = pl.Buffered(1)
    w_specs, w_shapes = _cast_slab_specs(ws32)
    return pl.pallas_call(
        functools.partial(_fox_kernel, len(ws32)),
        grid=(N_HEADS // G, N_TILES),
        in_specs=[
            pl.BlockSpec((None, G, T, HEAD_DIM), lambda h, i: (Q_B_TILE, h, i, 0)),
            pl.BlockSpec((G, T, F_PAD), lambda h, i: (h, i, 0)),
            pl.BlockSpec((None, G, SEQ, HEAD_DIM), lambda h, i: (K_B_TILE, h, 0, 0),
                         pipeline_mode=resident),
            pl.BlockSpec((G, SEQ, F_PAD), lambda h, i: (h, 0, 0), pipeline_mode=resident),
            pl.BlockSpec((None, G, N_TILES, VT_ROWS, T), lambda h, i: (1, h, 0, 0, 0),
                         pipeline_mode=resident),
            *w_specs,
        ],
        out_specs=[pl.BlockSpec((G, T, HEAD_DIM), lambda h, i: (h, i, 0)), *w_specs],
        out_shape=[jax.ShapeDtypeStruct((N_HEADS, SEQ, HEAD_DIM), BF16), *w_shapes],
        scratch_shapes=list(_stream_scratch()),
        compiler_params=pltpu.CompilerParams(
            dimension_semantics=("arbitrary", "arbitrary"), vmem_limit_bytes=VMEM_LIMIT),
        name="fox",
    )(z4, eq, z4, ek, vt, *ws32)


def _mix_kernel(a_ref, b_ref, ga_ref, gb_ref, x_ref, wpa_ref, wpb_ref, wo_ref, g_ref,
                x1_ref, hm_ref):
    def heads(ref):
        return jnp.concatenate([ref[h] for h in range(N_HEADS)], axis=1)

    def gate(ref):
        cols = [ref[t, h] for t in range(D_MODEL // WIDTH) for h in range(N_HEADS)]
        return jax.nn.sigmoid(jnp.concatenate(cols, axis=1).astype(F32))

    pa = jnp.dot(heads(a_ref), wpa_ref[...], preferred_element_type=F32)
    pb = jnp.dot(heads(b_ref), wpb_ref[...], preferred_element_type=F32)
    merged = gate(ga_ref) * pa + gate(gb_ref) * pb
    x1 = x_ref[...] + jnp.dot(merged.astype(BF16), wo_ref[...], preferred_element_type=F32)
    x1_ref[...] = x1
    hm_ref[...] = _rms_rows(x1, g_ref[...]).astype(BF16)


def _mix(a, b, z4, x2, wpa, wpb, wo, g):
    tm = MIX_TM
    gate_tiles = D_MODEL // IN_TN
    const = lambda i: (0, 0)
    return pl.pallas_call(
        _mix_kernel,
        grid=(SEQ // tm,),
        in_specs=[
            pl.BlockSpec((N_HEADS, tm, HEAD_DIM), lambda i: (0, i, 0)),
            pl.BlockSpec((N_HEADS, tm, HEAD_DIM), lambda i: (0, i, 0)),
            pl.BlockSpec((gate_tiles, N_HEADS, tm, HEAD_DIM),
                         lambda i: (N_QKV_TILES // gate_tiles, 0, i, 0)),
            pl.BlockSpec((gate_tiles, N_HEADS, tm, HEAD_DIM),
                         lambda i: (N_QKV_TILES // gate_tiles + 1, 0, i, 0)),
            pl.BlockSpec((tm, D_MODEL), lambda i: (i, 0)),
            pl.BlockSpec((WIDTH, D_MODEL), const),
            pl.BlockSpec((WIDTH, D_MODEL), const),
            pl.BlockSpec((D_MODEL, D_MODEL), const),
            pl.BlockSpec((1, D_MODEL), const),
        ],
        out_specs=[
            pl.BlockSpec((tm, D_MODEL), lambda i: (i, 0)),
            pl.BlockSpec((tm, D_MODEL), lambda i: (i, 0)),
        ],
        out_shape=[
            jax.ShapeDtypeStruct((SEQ, D_MODEL), F32),
            jax.ShapeDtypeStruct((SEQ, D_MODEL), BF16),
        ],
        compiler_params=pltpu.CompilerParams(
            dimension_semantics=("arbitrary",), vmem_limit_bytes=VMEM_LIMIT),
        name="mix",
    )(a, b, z4, z4, x2, wpa, wpb, wo, g)


def _mlp_up(hm, wu):
    h = jnp.dot(hm, wu, preferred_element_type=F32)
    return jnp.square(jnp.maximum(h, 0.0)).astype(BF16)


def _mlp_head_kernel(hm_ref, wu_ref, h_ref):
    h_ref[...] = _mlp_up(hm_ref[...], wu_ref[...])


def _mlp_kernel(h0_ref, hm_next_ref, x1_ref, wu_next_ref, wd_ref, g_ref, o_ref, acc_ref, h_ref):
    i, c = pl.program_id(0), pl.program_id(1)

    @pl.when((i == 0) & (c == 0))
    def _():
        h_ref[...] = h0_ref[...]

    @pl.when(c == 0)
    def _():
        acc_ref[...] = x1_ref[...]

    acc_ref[...] += jnp.dot(h_ref[...], wd_ref[...], preferred_element_type=F32)
    h_ref[...] = _mlp_up(hm_next_ref[...], wu_next_ref[...])

    @pl.when(c == pl.num_programs(1) - 1)
    def _():
        o_ref[...] = _rms_rows(acc_ref[...], g_ref[...])


def _mlp(hm, x1, wu, wd, g):
    tm, fc = MLP_TM, MLP_FC
    n_i, n_c = SEQ // tm, D_FF // fc
    params = dict(vmem_limit_bytes=VMEM_LIMIT)
    h0 = pl.pallas_call(
        _mlp_head_kernel,
        grid=(1,),
        in_specs=[pl.BlockSpec((tm, D_MODEL), lambda s: (0, 0)),
                  pl.BlockSpec((D_MODEL, fc), lambda s: (0, 0))],
        out_specs=pl.BlockSpec((tm, fc), lambda s: (0, 0)),
        out_shape=jax.ShapeDtypeStruct((tm, fc), BF16),
        compiler_params=pltpu.CompilerParams(dimension_semantics=("arbitrary",), **params),
        name="mlp_head",
    )(hm, wu)

    def next_rows(i, c):
        return jnp.minimum(i + (c + 1) // n_c, n_i - 1)

    return pl.pallas_call(
        _mlp_kernel,
        grid=(n_i, n_c),
        in_specs=[
            pl.BlockSpec((tm, fc), lambda i, c: (0, 0)),
            pl.BlockSpec((tm, D_MODEL), lambda i, c: (next_rows(i, c), 0)),
            pl.BlockSpec((tm, D_MODEL), lambda i, c: (i, 0)),
            pl.BlockSpec((D_MODEL, fc), lambda i, c: (0, (c + 1) % n_c)),
            pl.BlockSpec((fc, D_MODEL), lambda i, c: (c, 0)),
            pl.BlockSpec((1, D_MODEL), lambda i, c: (0, 0)),
        ],
        out_specs=pl.BlockSpec((tm, D_MODEL), lambda i, c: (i, 0)),
        out_shape=jax.ShapeDtypeStruct((SEQ, D_MODEL), F32),
        scratch_shapes=[pltpu.VMEM((tm, D_MODEL), F32), pltpu.VMEM((tm, fc), BF16)],
        compiler_params=pltpu.CompilerParams(
            dimension_semantics=("arbitrary", "arbitrary"), **params),
        name="mlp",
    )(h0, hm, x1, wu, wd, g)


def kernel(x, norm_mix_g, w_in, b_forget, w_proj_a, w_proj_b, w_out, norm_mlp_g, w_up, w_down,
           norm_final_g):
    row = lambda g: g.reshape(1, D_MODEL).astype(F32)

    x2 = x.reshape(SEQ, D_MODEL)
    z4, f, vt = _inproj(x2, row(norm_mix_g), w_in.T)

    b_pad = jnp.pad(b_forget.astype(F32), (0, F_PAD - N_HEADS)).reshape(1, F_PAD)
    ek, eq = _fox_prep(f, b_pad)

    slopes = jnp.exp2(-8.0 * jnp.arange(1, N_HEADS + 1, dtype=F32) / N_HEADS)
    a, w_up16, w_pa16, w_pb16 = _moba(slopes, z4, vt, (w_up, w_proj_a, w_proj_b))
    b, w_down16, w_out16 = _fox(z4, vt, ek, eq, (w_down, w_out))

    x1, hm = _mix(a, b, z4, x2, w_pa16, w_pb16, w_out16, row(norm_mlp_g))
    out = _mlp(hm, x1, w_up16, w_down16, row(norm_final_g))
    return out.reshape(x.shape)
```

```python
import functools
import math
from typing import Any, NamedTuple

import jax
import jax.numpy as jnp
from jax import lax
from jax.experimental import pallas as pl
from jax.experimental.pallas import tpu as pltpu

D_MODEL = 2048
SEQ = 8192
HEAD_DIM = 128
N_HEADS = 8
WIDTH = N_HEADS * HEAD_DIM
MOBA_BLOCK = 256
MOBA_TOPK = 3
N_BLOCKS = SEQ // MOBA_BLOCK
D_FF = 4 * D_MODEL
RMS_EPS = 1e-6
LOG2E = math.log2(math.e)
SCALE2 = HEAD_DIM ** -0.5 * LOG2E
F_PAD = 128

BF16 = jnp.bfloat16
F32 = jnp.float32
NEG_INF = float("-inf")
POS_INF = float("inf")

VMEM_LIMIT = 56 * 1024 * 1024

IN_TM, IN_TN = 1024, 1024
Q_A_TILE, K_A_TILE, V_A_TILE, Q_B_TILE, K_B_TILE, V_B_TILE = range(6)
N_QKV_TILES = 6 * WIDTH // IN_TN
N_GATE_TILES = 2 * D_MODEL // IN_TN
ATT_T = 2 * MOBA_BLOCK
ATT_G = 4
STEPS_PER_RUN = 4
N_SPLIT = 3
VT_ROWS = HEAD_DIM + 16
N_TILES = SEQ // ATT_T
MIX_TM = 512
MLP_TM, MLP_FC = 512, 1024
ROW_CHUNK = 128
PREP_CHUNK = 256


def _nt_dot(a, b):
    return lax.dot_general(a, b, (((1,), (1,)), ((), ())), preferred_element_type=F32)


def _rms_rows(x, g):
    ms = jnp.mean(x * x, axis=-1, keepdims=True)
    return x * lax.rsqrt(ms + RMS_EPS) * g


def _inproj_kernel(x_ref, g_ref, wt_ref, wft_ref, z_ref, f_ref, vt_ref, hn_ref):
    j = pl.program_id(1)

    @pl.when(j == 0)
    def _():
        def body(r, _):
            rows = pl.ds(pl.multiple_of(r * ROW_CHUNK, ROW_CHUNK), ROW_CHUNK)
            hn_ref[rows, :] = _rms_rows(x_ref[rows, :], g_ref[...]).astype(BF16)
            return 0
        lax.fori_loop(0, IN_TM // ROW_CHUNK, body, 0)
        f_ref[...] = _nt_dot(hn_ref[...], wft_ref[...].astype(BF16))

    scale = jnp.where((j == Q_A_TILE) | (j == Q_B_TILE), SCALE2, 1.0)
    zf = _nt_dot(hn_ref[...], wt_ref[...].astype(BF16)) * scale
    z = zf.astype(BF16)
    for h in range(IN_TN // HEAD_DIM):
        z_ref[h] = z[:, h * HEAD_DIM:(h + 1) * HEAD_DIM]

    @pl.when((j == V_A_TILE) | (j == V_B_TILE))
    def _():
        ones_row = jnp.where(
            lax.broadcasted_iota(jnp.int32, (VT_ROWS - HEAD_DIM, ATT_T), 0) == 0, 1.0, 0.0)
        for h in range(N_HEADS):
            for t in range(IN_TM // ATT_T):
                blk = zf[t * ATT_T:(t + 1) * ATT_T, h * HEAD_DIM:(h + 1) * HEAD_DIM]
                vt_ref[h, t, :HEAD_DIM, :] = blk.T.astype(BF16)
                vt_ref[h, t, HEAD_DIM:, :] = ones_row.astype(BF16)


def _inproj(x2, g, w_in_t):
    n_i = SEQ // IN_TM
    n_j = N_QKV_TILES + N_GATE_TILES
    slabs = IN_TN // HEAD_DIM
    tiles = IN_TM // ATT_T
    f_row = N_QKV_TILES * IN_TN

    def tile_row(i, j):
        return pl.multiple_of(j * IN_TN + jnp.where(j >= N_QKV_TILES, N_HEADS, 0), N_HEADS), 0

    return pl.pallas_call(
        _inproj_kernel,
        grid=(n_i, n_j),
        in_specs=[
            pl.BlockSpec((IN_TM, D_MODEL), lambda i, j: (i, 0)),
            pl.BlockSpec((1, D_MODEL), lambda i, j: (0, 0)),
            pl.BlockSpec((pl.Element(IN_TN), pl.Element(D_MODEL)), tile_row),
            pl.BlockSpec((F_PAD, D_MODEL), lambda i, j: (f_row // F_PAD, 0)),
        ],
        out_specs=[
            pl.BlockSpec((None, slabs, IN_TM, HEAD_DIM), lambda i, j: (j, 0, i, 0)),
            pl.BlockSpec((IN_TM, F_PAD), lambda i, j: (i, 0)),
            pl.BlockSpec((None, N_HEADS, tiles, VT_ROWS, ATT_T),
                         lambda i, j: (jnp.where(j > V_A_TILE, 1, 0), 0, i, 0, 0)),
        ],
        out_shape=[
            jax.ShapeDtypeStruct((n_j, slabs, SEQ, HEAD_DIM), BF16),
            jax.ShapeDtypeStruct((SEQ, F_PAD), F32),
            jax.ShapeDtypeStruct((2, N_HEADS, N_TILES, VT_ROWS, ATT_T), BF16),
        ],
        scratch_shapes=[pltpu.VMEM((IN_TM, D_MODEL), BF16)],
        compiler_params=pltpu.CompilerParams(
            dimension_semantics=("arbitrary", "arbitrary"), vmem_limit_bytes=VMEM_LIMIT),
        name="inproj",
    )(x2, g, w_in_t, w_in_t)


def _split(v):
    pieces = []
    for _ in range(N_SPLIT):
        p = v.astype(BF16).astype(F32)
        pieces.append(p)
        v = v - p
    return pieces


def _bias_features(v, lane, pieces_at, const_at, const):
    f = jnp.zeros_like(v)
    for n, p in enumerate(_split(v)):
        f = jnp.where(lane == pieces_at + n, p, f)
    f = jnp.where((lane >= const_at) & (lane < const_at + N_SPLIT), const, f)
    return f.astype(BF16)


def _fox_prep_kernel(f_ref, b_ref, ek_ref, eq_ref):
    n = PREP_CHUNK
    tri = (lax.broadcasted_iota(jnp.int32, (n, n), 0)
           >= lax.broadcasted_iota(jnp.int32, (n, n), 1)).astype(BF16)
    lane = lax.broadcasted_iota(jnp.int32, (n, F_PAD), 1)

    def body(ci, carry):
        rows = pl.ds(pl.multiple_of(ci * n, n), n)
        lf = jax.nn.log_sigmoid(f_ref[rows, :] + b_ref[...])
        cs = carry
        for p in _split(lf):
            cs = cs + jnp.dot(tri, p.astype(BF16), preferred_element_type=F32)
        c2 = cs * LOG2E
        for h in range(N_HEADS):
            col = jnp.broadcast_to(c2[:, h:h + 1], (n, F_PAD))
            ek_ref[h, rows, :] = _bias_features(col, lane, 0, N_SPLIT, 1.0)
            eq_ref[h, rows, :] = _bias_features(col, lane, N_SPLIT, 0, -1.0)
        return cs[n - 1:n, :]

    lax.fori_loop(0, SEQ // n, body, jnp.zeros((1, F_PAD), F32))


def _fox_prep(f, b_pad):
    feat = jax.ShapeDtypeStruct((N_HEADS, SEQ, F_PAD), BF16)
    return pl.pallas_call(
        _fox_prep_kernel,
        out_shape=[feat, feat],
        compiler_params=pltpu.CompilerParams(vmem_limit_bytes=VMEM_LIMIT),
        name="fox_prep",
    )(f, b_pad)


def _tile_iotas():
    ko = lax.broadcasted_iota(jnp.int32, (ATT_T, ATT_T), 0)
    qo = lax.broadcasted_iota(jnp.int32, (ATT_T, ATT_T), 1)
    return ko, qo


def _tile_rows(j):
    return pl.ds(pl.multiple_of(j * ATT_T, ATT_T), ATT_T)


def _scores(k_ref, ek, q_ref, eq, g, j):
    keys = jnp.concatenate([k_ref[g, _tile_rows(j), :], ek], axis=1)
    queries = jnp.concatenate([q_ref[g], eq], axis=1)
    return _nt_dot(keys, queries)


class _Stream(NamedTuple):
    s: Any
    p: Any
    alpha: Any
    m: Any
    acc: Any


def _stream_scratch():
    G, T = ATT_G, ATT_T
    row = pltpu.VMEM((G, 1, T), F32)
    return _Stream(s=pltpu.VMEM((G, T, T), F32), p=pltpu.VMEM((G, T, T), BF16),
                   alpha=row, m=row, acc=pltpu.VMEM((G, VT_ROWS, T), F32))


def _fold_first(st, g, s):
    m0 = jnp.max(s, axis=0, keepdims=True)
    st.m[g] = m0
    st.p[g] = jnp.exp2(s - m0).astype(BF16)
    st.alpha[g] = jnp.zeros_like(m0)
    st.acc[g] = jnp.zeros(st.acc.shape[1:], F32)


def _fold(st, g, p, m_new):
    st.alpha[g] = jnp.exp2(st.m[g] - m_new)
    st.p[g] = p.astype(BF16)
    st.m[g] = m_new


def _apply(st, g, vt_tile):
    st.acc[g] = st.alpha[g] * st.acc[g] + jnp.dot(vt_tile, st.p[g], preferred_element_type=F32)


def _stream_past_tiles(st, i, vt_ref, score, fold_past, o_ref):
    chains = range(ATT_G)
    for g in chains:
        st.s[g] = score(g, 0)

    def step(r, score_next):
        prev = jnp.where(r == 1, i, r - 2)
        for g in chains:
            _apply(st, g, vt_ref[g, prev])
        for g in chains:
            fold_past(g, r - 1)
        if score_next:
            for g in chains:
                st.s[g] = score(g, r)

    def run(first, count):
        for k in range(count):
            step(first + k, True)

    n_full = jnp.maximum(i - 1, 0)
    n_runs = n_full // STEPS_PER_RUN

    def body(u, _):
        run(STEPS_PER_RUN * u + 1, STEPS_PER_RUN)
        return 0
    lax.fori_loop(0, n_runs, body, 0)

    first = STEPS_PER_RUN * n_runs + 1
    left = n_full - STEPS_PER_RUN * n_runs
    size = STEPS_PER_RUN // 2
    while size >= 1:
        pl.when((left & size) != 0)(functools.partial(run, first, size))
        first = first + (left & size)
        size //= 2
    pl.when(i >= 1)(lambda: step(i, False))

    last = jnp.maximum(i - 1, 0)
    for g in chains:
        _apply(st, g, vt_ref[g, last])
        out = st.acc[g, :HEAD_DIM, :] / st.acc[g, HEAD_DIM:HEAD_DIM + 1, :]
        o_ref[g] = out.T.astype(o_ref.dtype)


def _moba_kernel(n_cast, slope_ref, q_ref, k_ref, vt_ref, *refs):
    w32_refs, (o_ref, *w16_refs), rest = refs[:n_cast], refs[n_cast:2 * n_cast + 1], refs[2 * n_cast + 1:]
    kmean_ref, ek_ref, eq_ref, selb_ref, *stream_refs = rest
    st = _Stream(*stream_refs)
    hg = pl.program_id(0)
    i = pl.program_id(1)
    _cast_slabs(w32_refs, w16_refs)
    ko, qo = _tile_iotas()
    half = MOBA_BLOCK
    slope2 = [slope_ref[hg * ATT_G + g] * LOG2E for g in range(ATT_G)]

    @pl.when(i == 0)
    def _():
        off = lax.broadcasted_iota(jnp.int32, (ATT_T, HEAD_DIM), 0).astype(F32)
        lane = lax.broadcasted_iota(jnp.int32, (ATT_T, HEAD_DIM), 1)
        for g in range(ATT_G):
            def body(b, _):
                rows = pl.ds(pl.multiple_of(b * MOBA_BLOCK, MOBA_BLOCK), MOBA_BLOCK)
                kb = k_ref[g, rows, :].astype(F32)
                kmean_ref[g, pl.ds(b, 1), :] = jnp.mean(kb, axis=0, keepdims=True)
                return 0
            lax.fori_loop(0, N_BLOCKS, body, 0)
            ek_ref[g] = _bias_features(off * slope2[g], lane, 0, N_SPLIT, 1.0)
            eq_ref[g] = _bias_features(off * (-slope2[g]), lane, N_SPLIT, 0, 1.0)

    jj = lax.broadcasted_iota(jnp.int32, (N_BLOCKS, ATT_T), 0)
    jjf = jj.astype(F32)
    lane = lax.broadcasted_iota(jnp.int32, (N_BLOCKS, ATT_T), 1)
    own = 2 * i + jnp.where(lane >= half, 1, 0)
    past = jj < own
    for g in range(ATT_G):
        q = q_ref[g]
        km = kmean_ref[g]
        km_hi = km.astype(BF16)
        km_lo = (km - km_hi.astype(F32)).astype(BF16)
        gate = _nt_dot(km_hi, q) + _nt_dot(km_lo, q)
        gm = jnp.where(past, gate, NEG_INF)
        selb = jnp.full((N_BLOCKS, ATT_T), NEG_INF, F32)
        for _ in range(MOBA_TOPK):
            mx = jnp.max(gm, axis=0, keepdims=True)
            first = jnp.min(jnp.where(gm == mx, jjf, float(N_BLOCKS)), axis=0, keepdims=True)
            pick = (jjf == first) & ((mx > NEG_INF) & (mx < POS_INF))
            selb = jnp.where(pick, 0.0, selb)
            gm = jnp.where(pick, NEG_INF, gm)
        selb_ref[g] = selb

    def score(g, tile):
        return _scores(k_ref, ek_ref[g], q_ref, eq_ref[g], g, tile)

    diag = [score(g, i) for g in range(ATT_G)]
    for g in range(ATT_G):
        sel0 = selb_ref[g, pl.ds(2 * i, 1), :]
        vis0 = jnp.where(qo[:1, :] < half, 0.0, sel0)
        s = diag[g] + jnp.where(ko < half, vis0, 0.0)
        _fold_first(st, g, jnp.where(ko <= qo, s, NEG_INF))

    def fold_past(g, t):
        cj = -slope2[g] * (ATT_T * (i - t)).astype(F32)
        r0 = selb_ref[g, pl.ds(2 * t, 1), :]
        r1 = selb_ref[g, pl.ds(2 * t + 1, 1), :]
        b0, b1 = st.s[g, :half, :], st.s[g, half:, :]
        bm = jnp.maximum(jnp.max(b0, axis=0, keepdims=True) + r0,
                         jnp.max(b1, axis=0, keepdims=True) + r1) + cj
        m_new = jnp.maximum(st.m[g], bm)
        t0 = jnp.where(r0 == 0.0, m_new - cj, POS_INF)
        t1 = jnp.where(r1 == 0.0, m_new - cj, POS_INF)
        p = jnp.concatenate([jnp.exp2(b0 - t0), jnp.exp2(b1 - t1)], axis=0)
        _fold(st, g, p, m_new)

    _stream_past_tiles(st, i, vt_ref, score, fold_past, o_ref)


def _cast_slab_specs(ws):
    steps = (N_HEADS // ATT_G) * N_TILES
    specs = [pl.BlockSpec((w.shape[0] // steps, w.shape[1]), lambda h, i: (h * N_TILES + i, 0))
             for w in ws]
    return specs, [jax.ShapeDtypeStruct(w.shape, BF16) for w in ws]


def _cast_slabs(w32_refs, w16_refs):
    for src, dst in zip(w32_refs, w16_refs):
        dst[...] = src[...].astype(BF16)


def _moba(slopes, z4, vt, ws32):
    G, T = ATT_G, ATT_T
    w_specs, w_shapes = _cast_slab_specs(ws32)
    return pl.pallas_call(
        functools.partial(_moba_kernel, len(ws32)),
        grid=(N_HEADS // G, N_TILES),
        in_specs=[
            pl.BlockSpec(memory_space=pltpu.SMEM),
            pl.BlockSpec((None, G, T, HEAD_DIM), lambda h, i: (Q_A_TILE, h, i, 0)),
            pl.BlockSpec((None, G, SEQ, HEAD_DIM), lambda h, i: (K_A_TILE, h, 0, 0),
                         pipeline_mode=pl.Buffered(1)),
            pl.BlockSpec((None, G, N_TILES, VT_ROWS, T), lambda h, i: (0, h, 0, 0, 0),
                         pipeline_mode=pl.Buffered(1)),
            *w_specs,
        ],
        out_specs=[pl.BlockSpec((G, T, HEAD_DIM), lambda h, i: (h, i, 0)), *w_specs],
        out_shape=[jax.ShapeDtypeStruct((N_HEADS, SEQ, HEAD_DIM), BF16), *w_shapes],
        scratch_shapes=[
            pltpu.VMEM((G, N_BLOCKS, HEAD_DIM), F32),
            pltpu.VMEM((G, T, HEAD_DIM), BF16),
            pltpu.VMEM((G, T, HEAD_DIM), BF16),
            pltpu.VMEM((G, N_BLOCKS, T), F32),
            *_stream_scratch(),
        ],
        compiler_params=pltpu.CompilerParams(
            dimension_semantics=("arbitrary", "arbitrary"), vmem_limit_bytes=VMEM_LIMIT),
        name="moba",
    )(slopes, z4, z4, vt, *ws32)


def _fox_kernel(n_cast, q_ref, eq_ref, k_ref, ek_ref, vt_ref, *refs):
    w32_refs, (o_ref, *w16_refs), stream_refs = (
        refs[:n_cast], refs[n_cast:2 * n_cast + 1], refs[2 * n_cast + 1:])
    st = _Stream(*stream_refs)
    i = pl.program_id(1)
    _cast_slabs(w32_refs, w16_refs)
    ko, qo = _tile_iotas()

    def score(g, tile):
        return _scores(k_ref, ek_ref[g, _tile_rows(tile), :], q_ref, eq_ref[g], g, tile)

    diag = [score(g, i) for g in range(ATT_G)]
    for g in range(ATT_G):
        _fold_first(st, g, jnp.where(ko <= qo, diag[g], NEG_INF))

    def fold_past(g, tile):
        s = st.s[g]
        m_new = jnp.maximum(st.m[g], jnp.max(s, axis=0, keepdims=True))
        _fold(st, g, jnp.exp2(s - m_new), m_new)

    _stream_past_tiles(st, i, vt_ref, score, fold_past, o_ref)


def _fox(z4, vt, ek, eq, ws32):
    G, T = ATT_G, ATT_T
    resident = pl.Buffered(1)
    w_specs, w_shapes = _cast_slab_specs(ws32)
    return pl.pallas_call(
        functools.partial(_fox_kernel, len(ws32)),
        grid=(N_HEADS // G, N_TILES),
        in_specs=[
            pl.BlockSpec((None, G, T, HEAD_DIM), lambda h, i: (Q_B_TILE, h, i, 0)),
            pl.BlockSpec((G, T, F_PAD), lambda h, i: (h, i, 0)),
            pl.BlockSpec((None, G, SEQ, HEAD_DIM), lambda h, i: (K_B_TILE, h, 0, 0),
                         pipeline_mode=resident),
            pl.BlockSpec((G, SEQ, F_PAD), lambda h, i: (h, 0, 0), pipeline_mode=resident),
            pl.BlockSpec((None, G, N_TILES, VT_ROWS, T), lambda h, i: (1, h, 0, 0, 0),
                         pipeline_mode=resident),
            *w_specs,
        ],
        out_specs=[pl.BlockSpec((G, T, HEAD_DIM), lambda h, i: (h, i, 0)), *w_specs],
        out_shape=[jax.ShapeDtypeStruct((N_HEADS, SEQ, HEAD_DIM), BF16), *w_shapes],
        scratch_shapes=list(_stream_scratch()),
        compiler_params=pltpu.CompilerParams(
            dimension_semantics=("arbitrary", "arbitrary"), vmem_limit_bytes=VMEM_LIMIT),
        name="fox",
    )(z4, eq, z4, ek, vt, *ws32)


def _mix_kernel(a_ref, b_ref, ga_ref, gb_ref, x_ref, wpa_ref, wpb_ref, wo_ref, g_ref,
                x1_ref, hm_ref):
    def heads(ref):
        return jnp.concatenate([ref[h] for h in range(N_HEADS)], axis=1)

    def gate(ref):
        cols = [ref[t, h] for t in range(D_MODEL // WIDTH) for h in range(N_HEADS)]
        return jax.nn.sigmoid(jnp.concatenate(cols, axis=1).astype(F32))

    pa = jnp.dot(heads(a_ref), wpa_ref[...], preferred_element_type=F32)
    pb = jnp.dot(heads(b_ref), wpb_ref[...], preferred_element_type=F32)
    merged = gate(ga_ref) * pa + gate(gb_ref) * pb
    x1 = x_ref[...] + jnp.dot(merged.astype(BF16), wo_ref[...], preferred_element_type=F32)
    x1_ref[...] = x1
    hm_ref[...] = _rms_rows(x1, g_ref[...]).astype(BF16)


def _mix(a, b, z4, x2, wpa, wpb, wo, g):
    tm = MIX_TM
    gate_tiles = D_MODEL // IN_TN
    const = lambda i: (0, 0)
    return pl.pallas_call(
        _mix_kernel,
        grid=(SEQ // tm,),
        in_specs=[
            pl.BlockSpec((N_HEADS, tm, HEAD_DIM), lambda i: (0, i, 0)),
            pl.BlockSpec((N_HEADS, tm, HEAD_DIM), lambda i: (0, i, 0)),
            pl.BlockSpec((gate_tiles, N_HEADS, tm, HEAD_DIM),
                         lambda i: (N_QKV_TILES // gate_tiles, 0, i, 0)),
            pl.BlockSpec((gate_tiles, N_HEADS, tm, HEAD_DIM),
                         lambda i: (N_QKV_TILES // gate_tiles + 1, 0, i, 0)),
            pl.BlockSpec((tm, D_MODEL), lambda i: (i, 0)),
            pl.BlockSpec((WIDTH, D_MODEL), const),
            pl.BlockSpec((WIDTH, D_MODEL), const),
            pl.BlockSpec((D_MODEL, D_MODEL), const),
            pl.BlockSpec((1, D_MODEL), const),
        ],
        out_specs=[
            pl.BlockSpec((tm, D_MODEL), lambda i: (i, 0)),
            pl.BlockSpec((tm, D_MODEL), lambda i: (i, 0)),
        ],
        out_shape=[
            jax.ShapeDtypeStruct((SEQ, D_MODEL), F32),
            jax.ShapeDtypeStruct((SEQ, D_MODEL), BF16),
        ],
        compiler_params=pltpu.CompilerParams(
            dimension_semantics=("arbitrary",), vmem_limit_bytes=VMEM_LIMIT),
        name="mix",
    )(a, b, z4, z4, x2, wpa, wpb, wo, g)


def _mlp_up(hm, wu):
    h = jnp.dot(hm, wu, preferred_element_type=F32)
    return jnp.square(jnp.maximum(h, 0.0)).astype(BF16)


def _mlp_head_kernel(hm_ref, wu_ref, h_ref):
    h_ref[...] = _mlp_up(hm_ref[...], wu_ref[...])


def _mlp_kernel(h0_ref, hm_next_ref, x1_ref, wu_next_ref, wd_ref, g_ref, o_ref, acc_ref, h_ref):
    i, c = pl.program_id(0), pl.program_id(1)

    @pl.when((i == 0) & (c == 0))
    def _():
        h_ref[...] = h0_ref[...]

    @pl.when(c == 0)
    def _():
        acc_ref[...] = x1_ref[...]

    acc_ref[...] += jnp.dot(h_ref[...], wd_ref[...], preferred_element_type=F32)
    h_ref[...] = _mlp_up(hm_next_ref[...], wu_next_ref[...])

    @pl.when(c == pl.num_programs(1) - 1)
    def _():
        o_ref[...] = _rms_rows(acc_ref[...], g_ref[...])


def _mlp(hm, x1, wu, wd, g):
    tm, fc = MLP_TM, MLP_FC
    n_i, n_c = SEQ // tm, D_FF // fc
    params = dict(vmem_limit_bytes=VMEM_LIMIT)
    h0 = pl.pallas_call(
        _mlp_head_kernel,
        grid=(1,),
        in_specs=[pl.BlockSpec((tm, D_MODEL), lambda s: (0, 0)),
                  pl.BlockSpec((D_MODEL, fc), lambda s: (0, 0))],
        out_specs=pl.BlockSpec((tm, fc), lambda s: (0, 0)),
        out_shape=jax.ShapeDtypeStruct((tm, fc), BF16),
        compiler_params=pltpu.CompilerParams(dimension_semantics=("arbitrary",), **params),
        name="mlp_head",
    )(hm, wu)

    def next_rows(i, c):
        return jnp.minimum(i + (c + 1) // n_c, n_i - 1)

    return pl.pallas_call(
        _mlp_kernel,
        grid=(n_i, n_c),
        in_specs=[
            pl.BlockSpec((tm, fc), lambda i, c: (0, 0)),
            pl.BlockSpec((tm, D_MODEL), lambda i, c: (next_rows(i, c), 0)),
            pl.BlockSpec((tm, D_MODEL), lambda i, c: (i, 0)),
            pl.BlockSpec((D_MODEL, fc), lambda i, c: (0, (c + 1) % n_c)),
            pl.BlockSpec((fc, D_MODEL), lambda i, c: (c, 0)),
            pl.BlockSpec((1, D_MODEL), lambda i, c: (0, 0)),
        ],
        out_specs=pl.BlockSpec((tm, D_MODEL), lambda i, c: (i, 0)),
        out_shape=jax.ShapeDtypeStruct((SEQ, D_MODEL), F32),
        scratch_shapes=[pltpu.VMEM((tm, D_MODEL), F32), pltpu.VMEM((tm, fc), BF16)],
        compiler_params=pltpu.CompilerParams(
            dimension_semantics=("arbitrary", "arbitrary"), **params),
        name="mlp",
    )(h0, hm, x1, wu, wd, g)


def kernel(x, norm_mix_g, w_in, b_forget, w_proj_a, w_proj_b, w_out, norm_mlp_g, w_up, w_down,
           norm_final_g):
    row = lambda g: g.reshape(1, D_MODEL).astype(F32)

    x2 = x.reshape(SEQ, D_MODEL)
    z4, f, vt = _inproj(x2, row(norm_mix_g), w_in.T)

    b_pad = jnp.pad(b_forget.astype(F32), (0, F_PAD - N_HEADS)).reshape(1, F_PAD)
    ek, eq = _fox_prep(f, b_pad)

    slopes = jnp.exp2(-8.0 * jnp.arange(1, N_HEADS + 1, dtype=F32) / N_HEADS)
    a, w_up16, w_pa16, w_pb16 = _moba(slopes, z4, vt, (w_up, w_proj_a, w_proj_b))
    b, w_down16, w_out16 = _fox(z4, vt, ek, eq, (w_down, w_out))

    x1, hm = _mix(a, b, z4, x2, w_pa16, w_pb16, w_out16, row(norm_mlp_g))
    out = _mlp(hm, x1, w_up16, w_down16, row(norm_final_g))
    return out.reshape(x.shape)
```

```python
import functools
import math
from typing import Any, NamedTuple

import jax
import jax.numpy as jnp
from jax import lax
from jax.experimental import pallas as pl
from jax.experimental.pallas import tpu as pltpu

D_MODEL = 2048
SEQ = 8192
HEAD_DIM = 128
N_HEADS = 8
WIDTH = N_HEADS * HEAD_DIM
MOBA_BLOCK = 256
MOBA_TOPK = 3
N_BLOCKS = SEQ // MOBA_BLOCK
D_FF = 4 * D_MODEL
RMS_EPS = 1e-6
LOG2E = math.log2(math.e)
SCALE2 = HEAD_DIM ** -0.5 * LOG2E
F_PAD = 128

BF16 = jnp.bfloat16
F32 = jnp.float32
NEG_INF = float("-inf")
POS_INF = float("inf")

VMEM_LIMIT = 56 * 1024 * 1024

IN_TM, IN_TN = 1024, 1024
Q_A_TILE, K_A_TILE, V_A_TILE, Q_B_TILE, K_B_TILE, V_B_TILE = range(6)
N_QKV_TILES = 6 * WIDTH // IN_TN
N_GATE_TILES = 2 * D_MODEL // IN_TN
ATT_T = 2 * MOBA_BLOCK
ATT_G = 4
N_SPLIT = 3
VT_ROWS = HEAD_DIM + 16
N_TILES = SEQ // ATT_T
MIX_TM = 512
MLP_TM, MLP_FC = 512, 1024
ROW_CHUNK = 128
PREP_CHUNK = 256


def _nt_dot(a, b):
    return lax.dot_general(a, b, (((1,), (1,)), ((), ())), preferred_element_type=F32)


def _rms_rows(x, g):
    ms = jnp.mean(x * x, axis=-1, keepdims=True)
    return x * lax.rsqrt(ms + RMS_EPS) * g


def _inproj_kernel(x_ref, g_ref, wt_ref, wft_ref, z_ref, f_ref, vt_ref, hn_ref):
    j = pl.program_id(1)

    @pl.when(j == 0)
    def _():
        def body(r, _):
            rows = pl.ds(pl.multiple_of(r * ROW_CHUNK, ROW_CHUNK), ROW_CHUNK)
            hn_ref[rows, :] = _rms_rows(x_ref[rows, :], g_ref[...]).astype(BF16)
            return 0
        lax.fori_loop(0, IN_TM // ROW_CHUNK, body, 0)
        f_ref[...] = _nt_dot(hn_ref[...], wft_ref[...].astype(BF16))

    scale = jnp.where((j == Q_A_TILE) | (j == Q_B_TILE), SCALE2, 1.0)
    zf = _nt_dot(hn_ref[...], wt_ref[...].astype(BF16)) * scale
    z = zf.astype(BF16)
    for h in range(IN_TN // HEAD_DIM):
        z_ref[h] = z[:, h * HEAD_DIM:(h + 1) * HEAD_DIM]

    @pl.when((j == V_A_TILE) | (j == V_B_TILE))
    def _():
        ones_row = jnp.where(
            lax.broadcasted_iota(jnp.int32, (VT_ROWS - HEAD_DIM, ATT_T), 0) == 0, 1.0, 0.0)
        for h in range(N_HEADS):
            for t in range(IN_TM // ATT_T):
                blk = zf[t * ATT_T:(t + 1) * ATT_T, h * HEAD_DIM:(h + 1) * HEAD_DIM]
                vt_ref[h, t, :HEAD_DIM, :] = blk.T.astype(BF16)
                vt_ref[h, t, HEAD_DIM:, :] = ones_row.astype(BF16)


def _inproj(x2, g, w_in_t):
    n_i = SEQ // IN_TM
    n_j = N_QKV_TILES + N_GATE_TILES
    slabs = IN_TN // HEAD_DIM
    tiles = IN_TM // ATT_T
    f_row = N_QKV_TILES * IN_TN

    def tile_row(i, j):
        return pl.multiple_of(j * IN_TN + jnp.where(j >= N_QKV_TILES, N_HEADS, 0), N_HEADS), 0

    return pl.pallas_call(
        _inproj_kernel,
        grid=(n_i, n_j),
        in_specs=[
            pl.BlockSpec((IN_TM, D_MODEL), lambda i, j: (i, 0)),
            pl.BlockSpec((1, D_MODEL), lambda i, j: (0, 0)),
            pl.BlockSpec((pl.Element(IN_TN), pl.Element(D_MODEL)), tile_row),
            pl.BlockSpec((F_PAD, D_MODEL), lambda i, j: (f_row // F_PAD, 0)),
        ],
        out_specs=[
            pl.BlockSpec((None, slabs, IN_TM, HEAD_DIM), lambda i, j: (j, 0, i, 0)),
            pl.BlockSpec((IN_TM, F_PAD), lambda i, j: (i, 0)),
            pl.BlockSpec((None, N_HEADS, tiles, VT_ROWS, ATT_T),
                         lambda i, j: (jnp.where(j > V_A_TILE, 1, 0), 0, i, 0, 0)),
        ],
        out_shape=[
            jax.ShapeDtypeStruct((n_j, slabs, SEQ, HEAD_DIM), BF16),
            jax.ShapeDtypeStruct((SEQ, F_PAD), F32),
            jax.ShapeDtypeStruct((2, N_HEADS, N_TILES, VT_ROWS, ATT_T), BF16),
        ],
        scratch_shapes=[pltpu.VMEM((IN_TM, D_MODEL), BF16)],
        compiler_params=pltpu.CompilerParams(
            dimension_semantics=("arbitrary", "arbitrary"), vmem_limit_bytes=VMEM_LIMIT),
        name="inproj",
    )(x2, g, w_in_t, w_in_t)


def _split(v):
    pieces = []
    for _ in range(N_SPLIT):
        p = v.astype(BF16).astype(F32)
        pieces.append(p)
        v = v - p
    return pieces


def _bias_features(v, lane, pieces_at, const_at, const):
    f = jnp.zeros_like(v)
    for n, p in enumerate(_split(v)):
        f = jnp.where(lane == pieces_at + n, p, f)
    f = jnp.where((lane >= const_at) & (lane < const_at + N_SPLIT), const, f)
    return f.astype(BF16)


def _fox_prep_kernel(f_ref, b_ref, ek_ref, eq_ref):
    n = PREP_CHUNK
    tri = (lax.broadcasted_iota(jnp.int32, (n, n), 0)
           >= lax.broadcasted_iota(jnp.int32, (n, n), 1)).astype(BF16)
    lane = lax.broadcasted_iota(jnp.int32, (n, F_PAD), 1)

    def body(ci, carry):
        rows = pl.ds(pl.multiple_of(ci * n, n), n)
        lf = jax.nn.log_sigmoid(f_ref[rows, :] + b_ref[...])
        cs = carry
        for p in _split(lf):
            cs = cs + jnp.dot(tri, p.astype(BF16), preferred_element_type=F32)
        c2 = cs * LOG2E
        for h in range(N_HEADS):
            col = jnp.broadcast_to(c2[:, h:h + 1], (n, F_PAD))
            ek_ref[h, rows, :] = _bias_features(col, lane, 0, N_SPLIT, 1.0)
            eq_ref[h, rows, :] = _bias_features(col, lane, N_SPLIT, 0, -1.0)
        return cs[n - 1:n, :]

    lax.fori_loop(0, SEQ // n, body, jnp.zeros((1, F_PAD), F32))


def _fox_prep(f, b_pad):
    feat = jax.ShapeDtypeStruct((N_HEADS, SEQ, F_PAD), BF16)
    return pl.pallas_call(
        _fox_prep_kernel,
        out_shape=[feat, feat],
        compiler_params=pltpu.CompilerParams(vmem_limit_bytes=VMEM_LIMIT),
        name="fox_prep",
    )(f, b_pad)


def _tile_iotas():
    ko = lax.broadcasted_iota(jnp.int32, (ATT_T, ATT_T), 0)
    qo = lax.broadcasted_iota(jnp.int32, (ATT_T, ATT_T), 1)
    return ko, qo


def _tile_rows(j):
    return pl.ds(pl.multiple_of(j * ATT_T, ATT_T), ATT_T)


def _scores(k_ref, ek, q_ref, eq, g, j):
    keys = jnp.concatenate([k_ref[g, _tile_rows(j), :], ek], axis=1)
    queries = jnp.concatenate([q_ref[g], eq], axis=1)
    return _nt_dot(keys, queries)


class _Stream(NamedTuple):
    s: Any
    p: Any
    alpha: Any
    m: Any
    acc: Any


def _stream_scratch():
    G, T = ATT_G, ATT_T
    row = pltpu.VMEM((G, 1, T), F32)
    return _Stream(s=pltpu.VMEM((G, T, T), F32), p=pltpu.VMEM((G, T, T), BF16),
                   alpha=row, m=row, acc=pltpu.VMEM((G, VT_ROWS, T), F32))


def _fold_first(st, g, s):
    m0 = jnp.max(s, axis=0, keepdims=True)
    st.m[g] = m0
    st.p[g] = jnp.exp2(s - m0).astype(BF16)
    st.alpha[g] = jnp.zeros_like(m0)
    st.acc[g] = jnp.zeros(st.acc.shape[1:], F32)


def _fold(st, g, p, m_new):
    st.alpha[g] = jnp.exp2(st.m[g] - m_new)
    st.p[g] = p.astype(BF16)
    st.m[g] = m_new


def _apply(st, g, vt_tile):
    st.acc[g] = st.alpha[g] * st.acc[g] + jnp.dot(vt_tile, st.p[g], preferred_element_type=F32)


def _stream_past_tiles(st, i, vt_ref, score, fold_past, o_ref):
    chains = range(ATT_G)
    for g in chains:
        st.s[g] = score(g, 0)

    def step(r, score_next):
        prev = jnp.where(r == 1, i, r - 2)
        for g in chains:
            _apply(st, g, vt_ref[g, prev])
        for g in chains:
            fold_past(g, r - 1)
        if score_next:
            for g in chains:
                st.s[g] = score(g, r)

    n_full = jnp.maximum(i - 1, 0)

    def body(u, _):
        step(2 * u + 1, True)
        step(2 * u + 2, True)
        return 0
    lax.fori_loop(0, n_full // 2, body, 0)

    @pl.when(n_full % 2 == 1)
    def _():
        step(n_full, True)
        step(i, False)

    pl.when((n_full % 2 == 0) & (i >= 1))(lambda: step(i, False))

    last = jnp.maximum(i - 1, 0)
    for g in chains:
        _apply(st, g, vt_ref[g, last])
        out = st.acc[g, :HEAD_DIM, :] / st.acc[g, HEAD_DIM:HEAD_DIM + 1, :]
        o_ref[g] = out.T.astype(o_ref.dtype)


def _moba_kernel(n_cast, slope_ref, q_ref, k_ref, vt_ref, *refs):
    w32_refs, (o_ref, *w16_refs), rest = refs[:n_cast], refs[n_cast:2 * n_cast + 1], refs[2 * n_cast + 1:]
    kmean_ref, ek_ref, eq_ref, selb_ref, *stream_refs = rest
    st = _Stream(*stream_refs)
    hg = pl.program_id(0)
    i = pl.program_id(1)
    _cast_slabs(w32_refs, w16_refs)
    ko, qo = _tile_iotas()
    half = MOBA_BLOCK
    slope2 = [slope_ref[hg * ATT_G + g] * LOG2E for g in range(ATT_G)]

    @pl.when(i == 0)
    def _():
        off = lax.broadcasted_iota(jnp.int32, (ATT_T, HEAD_DIM), 0).astype(F32)
        lane = lax.broadcasted_iota(jnp.int32, (ATT_T, HEAD_DIM), 1)
        for g in range(ATT_G):
            def body(b, _):
                rows = pl.ds(pl.multiple_of(b * MOBA_BLOCK, MOBA_BLOCK), MOBA_BLOCK)
                kb = k_ref[g, rows, :].astype(F32)
                kmean_ref[g, pl.ds(b, 1), :] = jnp.mean(kb, axis=0, keepdims=True)
                return 0
            lax.fori_loop(0, N_BLOCKS, body, 0)
            ek_ref[g] = _bias_features(off * slope2[g], lane, 0, N_SPLIT, 1.0)
            eq_ref[g] = _bias_features(off * (-slope2[g]), lane, N_SPLIT, 0, 1.0)

    jj = lax.broadcasted_iota(jnp.int32, (N_BLOCKS, ATT_T), 0)
    jjf = jj.astype(F32)
    lane = lax.broadcasted_iota(jnp.int32, (N_BLOCKS, ATT_T), 1)
    own = 2 * i + jnp.where(lane >= half, 1, 0)
    past = jj < own
    for g in range(ATT_G):
        q = q_ref[g]
        km = kmean_ref[g]
        km_hi = km.astype(BF16)
        km_lo = (km - km_hi.astype(F32)).astype(BF16)
        gate = _nt_dot(km_hi, q) + _nt_dot(km_lo, q)
        gm = jnp.where(past, gate, NEG_INF)
        selb = jnp.full((N_BLOCKS, ATT_T), NEG_INF, F32)
        for _ in range(MOBA_TOPK):
            mx = jnp.max(gm, axis=0, keepdims=True)
            first = jnp.min(jnp.where(gm == mx, jjf, float(N_BLOCKS)), axis=0, keepdims=True)
            pick = (jjf == first) & ((mx > NEG_INF) & (mx < POS_INF))
            selb = jnp.where(pick, 0.0, selb)
            gm = jnp.where(pick, NEG_INF, gm)
        selb_ref[g] = selb

    def score(g, tile):
        return _scores(k_ref, ek_ref[g], q_ref, eq_ref[g], g, tile)

    diag = [score(g, i) for g in range(ATT_G)]
    for g in range(ATT_G):
        sel0 = selb_ref[g, pl.ds(2 * i, 1), :]
        vis0 = jnp.where(qo[:1, :] < half, 0.0, sel0)
        s = diag[g] + jnp.where(ko < half, vis0, 0.0)
        _fold_first(st, g, jnp.where(ko <= qo, s, NEG_INF))

    def fold_past(g, t):
        cj = -slope2[g] * (ATT_T * (i - t)).astype(F32)
        r0 = selb_ref[g, pl.ds(2 * t, 1), :]
        r1 = selb_ref[g, pl.ds(2 * t + 1, 1), :]
        b0, b1 = st.s[g, :half, :], st.s[g, half:, :]
        bm = jnp.maximum(jnp.max(b0, axis=0, keepdims=True) + r0,
                         jnp.max(b1, axis=0, keepdims=True) + r1) + cj
        m_new = jnp.maximum(st.m[g], bm)
        t0 = jnp.where(r0 == 0.0, m_new - cj, POS_INF)
        t1 = jnp.where(r1 == 0.0, m_new - cj, POS_INF)
        p = jnp.concatenate([jnp.exp2(b0 - t0), jnp.exp2(b1 - t1)], axis=0)
        _fold(st, g, p, m_new)

    _stream_past_tiles(st, i, vt_ref, score, fold_past, o_ref)


def _cast_slab_specs(ws):
    steps = (N_HEADS // ATT_G) * N_TILES
    specs = [pl.BlockSpec((w.shape[0] // steps, w.shape[1]), lambda h, i: (h * N_TILES + i, 0))
             for w in ws]
    return specs, [jax.ShapeDtypeStruct(w.shape, BF16) for w in ws]


def _cast_slabs(w32_refs, w16_refs):
    for src, dst in zip(w32_refs, w16_refs):
        dst[...] = src[...].astype(BF16)


def _moba(slopes, z4, vt, ws32):
    G, T = ATT_G, ATT_T
    w_specs, w_shapes = _cast_slab_specs(ws32)
    return pl.pallas_call(
        functools.partial(_moba_kernel, len(ws32)),
        grid=(N_HEADS // G, N_TILES),
        in_specs=[
            pl.BlockSpec(memory_space=pltpu.SMEM),
            pl.BlockSpec((None, G, T, HEAD_DIM), lambda h, i: (Q_A_TILE, h, i, 0)),
            pl.BlockSpec((None, G, SEQ, HEAD_DIM), lambda h, i: (K_A_TILE, h, 0, 0),
                         pipeline_mode=pl.Buffered(1)),
            pl.BlockSpec((None, G, N_TILES, VT_ROWS, T), lambda h, i: (0, h, 0, 0, 0),
                         pipeline_mode=pl.Buffered(1)),
            *w_specs,
        ],
        out_specs=[pl.BlockSpec((G, T, HEAD_DIM), lambda h, i: (h, i, 0)), *w_specs],
        out_shape=[jax.ShapeDtypeStruct((N_HEADS, SEQ, HEAD_DIM), BF16), *w_shapes],
        scratch_shapes=[
            pltpu.VMEM((G, N_BLOCKS, HEAD_DIM), F32),
            pltpu.VMEM((G, T, HEAD_DIM), BF16),
            pltpu.VMEM((G, T, HEAD_DIM), BF16),
            pltpu.VMEM((G, N_BLOCKS, T), F32),
            *_stream_scratch(),
        ],
        compiler_params=pltpu.CompilerParams(
            dimension_semantics=("arbitrary", "arbitrary"), vmem_limit_bytes=VMEM_LIMIT),
        name="moba",
    )(slopes, z4, z4, vt, *ws32)


def _fox_kernel(n_cast, q_ref, eq_ref, k_ref, ek_ref, vt_ref, *refs):
    w32_refs, (o_ref, *w16_refs), stream_refs = (
        refs[:n_cast], refs[n_cast:2 * n_cast + 1], refs[2 * n_cast + 1:])
    st = _Stream(*stream_refs)
    i = pl.program_id(1)
    _cast_slabs(w32_refs, w16_refs)
    ko, qo = _tile_iotas()

    def score(g, tile):
        return _scores(k_ref, ek_ref[g, _tile_rows(tile), :], q_ref, eq_ref[g], g, tile)

    diag = [score(g, i) for g in range(ATT_G)]
    for g in range(ATT_G):
        _fold_first(st, g, jnp.where(ko <= qo, diag[g], NEG_INF))

    def fold_past(g, tile):
        s = st.s[g]
        m_new = jnp.maximum(st.m[g], jnp.max(s, axis=0, keepdims=True))
        _fold(st, g, jnp.exp2(s - m_new), m_new)

    _stream_past_tiles(st, i, vt_ref, score, fold_past, o_ref)


def _fox(z4, vt, ek, eq, ws32):
    G, T = ATT_G, ATT_T
    resident = pl.Buffered(1)
    w_specs, w_shapes = _cast_slab_specs(ws32)
    return pl.pallas_call(
        functools.partial(_fox_kernel, len(ws32)),
        grid=(N_HEADS // G, N_TILES),
        in_specs=[
            pl.BlockSpec((None, G, T, HEAD_DIM), lambda h, i: (Q_B_TILE, h, i, 0)),
            pl.BlockSpec((G, T, F_PAD), lambda h, i: (h, i, 0)),
            pl.BlockSpec((None, G, SEQ, HEAD_DIM), lambda h, i: (K_B_TILE, h, 0, 0),
                         pipeline_mode=resident),
            pl.BlockSpec((G, SEQ, F_PAD), lambda h, i: (h, 0, 0), pipeline_mode=resident),
            pl.BlockSpec((None, G, N_TILES, VT_ROWS, T), lambda h, i: (1, h, 0, 0, 0),
                         pipeline_mode=resident),
            *w_specs,
        ],
        out_specs=[pl.BlockSpec((G, T, HEAD_DIM), lambda h, i: (h, i, 0)), *w_specs],
        out_shape=[jax.ShapeDtypeStruct((N_HEADS, SEQ, HEAD_DIM), BF16), *w_shapes],
        scratch_shapes=list(_stream_scratch()),
        compiler_params=pltpu.CompilerParams(
            dimension_semantics=("arbitrary", "arbitrary"), vmem_limit_bytes=VMEM_LIMIT),
        name="fox",
    )(z4, eq, z4, ek, vt, *ws32)


def _mix_kernel(a_ref, b_ref, ga_ref, gb_ref, x_ref, wpa_ref, wpb_ref, wo_ref, g_ref,
                x1_ref, hm_ref):
    def heads(ref):
        return jnp.concatenate([ref[h] for h in range(N_HEADS)], axis=1)

    def gate(ref):
        cols = [ref[t, h] for t in range(D_MODEL // WIDTH) for h in range(N_HEADS)]
        return jax.nn.sigmoid(jnp.concatenate(cols, axis=1).astype(F32))

    pa = jnp.dot(heads(a_ref), wpa_ref[...], preferred_element_type=F32)
    pb = jnp.dot(heads(b_ref), wpb_ref[...], preferred_element_type=F32)
    merged = gate(ga_ref) * pa + gate(gb_ref) * pb
    x1 = x_ref[...] + jnp.dot(merged.astype(BF16), wo_ref[...], preferred_element_type=F32)
    x1_ref[...] = x1
    hm_ref[...] = _rms_rows(x1, g_ref[...]).astype(BF16)


def _mix(a, b, z4, x2, wpa, wpb, wo, g):
    tm = MIX_TM
    gate_tiles = D_MODEL // IN_TN
    const = lambda i: (0, 0)
    return pl.pallas_call(
        _mix_kernel,
        grid=(SEQ // tm,),
        in_specs=[
            pl.BlockSpec((N_HEADS, tm, HEAD_DIM), lambda i: (0, i, 0)),
            pl.BlockSpec((N_HEADS, tm, HEAD_DIM), lambda i: (0, i, 0)),
            pl.BlockSpec((gate_tiles, N_HEADS, tm, HEAD_DIM),
                         lambda i: (N_QKV_TILES // gate_tiles, 0, i, 0)),
            pl.BlockSpec((gate_tiles, N_HEADS, tm, HEAD_DIM),
                         lambda i: (N_QKV_TILES // gate_tiles + 1, 0, i, 0)),
            pl.BlockSpec((tm, D_MODEL), lambda i: (i, 0)),
            pl.BlockSpec((WIDTH, D_MODEL), const),
            pl.BlockSpec((WIDTH, D_MODEL), const),
            pl.BlockSpec((D_MODEL, D_MODEL), const),
            pl.BlockSpec((1, D_MODEL), const),
        ],
        out_specs=[
            pl.BlockSpec((tm, D_MODEL), lambda i: (i, 0)),
            pl.BlockSpec((tm, D_MODEL), lambda i: (i, 0)),
        ],
        out_shape=[
            jax.ShapeDtypeStruct((SEQ, D_MODEL), F32),
            jax.ShapeDtypeStruct((SEQ, D_MODEL), BF16),
        ],
        compiler_params=pltpu.CompilerParams(
            dimension_semantics=("arbitrary",), vmem_limit_bytes=VMEM_LIMIT),
        name="mix",
    )(a, b, z4, z4, x2, wpa, wpb, wo, g)


def _mlp_up(hm, wu):
    h = jnp.dot(hm, wu, preferred_element_type=F32)
    return jnp.square(jnp.maximum(h, 0.0)).astype(BF16)


def _mlp_head_kernel(hm_ref, wu_ref, h_ref):
    h_ref[...] = _mlp_up(hm_ref[...], wu_ref[...])


def _mlp_kernel(h0_ref, hm_next_ref, x1_ref, wu_next_ref, wd_ref, g_ref, o_ref, acc_ref, h_ref):
    i, c = pl.program_id(0), pl.program_id(1)

    @pl.when((i == 0) & (c == 0))
    def _():
        h_ref[...] = h0_ref[...]

    @pl.when(c == 0)
    def _():
        acc_ref[...] = x1_ref[...]

    acc_ref[...] += jnp.dot(h_ref[...], wd_ref[...], preferred_element_type=F32)
    h_ref[...] = _mlp_up(hm_next_ref[...], wu_next_ref[...])

    @pl.when(c == pl.num_programs(1) - 1)
    def _():
        o_ref[...] = _rms_rows(acc_ref[...], g_ref[...])


def _mlp(hm, x1, wu, wd, g):
    tm, fc = MLP_TM, MLP_FC
    n_i, n_c = SEQ // tm, D_FF // fc
    params = dict(vmem_limit_bytes=VMEM_LIMIT)
    h0 = pl.pallas_call(
        _mlp_head_kernel,
        grid=(1,),
        in_specs=[pl.BlockSpec((tm, D_MODEL), lambda s: (0, 0)),
                  pl.BlockSpec((D_MODEL, fc), lambda s: (0, 0))],
        out_specs=pl.BlockSpec((tm, fc), lambda s: (0, 0)),
        out_shape=jax.ShapeDtypeStruct((tm, fc), BF16),
        compiler_params=pltpu.CompilerParams(dimension_semantics=("arbitrary",), **params),
        name="mlp_head",
    )(hm, wu)

    def next_rows(i, c):
        return jnp.minimum(i + (c + 1) // n_c, n_i - 1)

    return pl.pallas_call(
        _mlp_kernel,
        grid=(n_i, n_c),
        in_specs=[
            pl.BlockSpec((tm, fc), lambda i, c: (0, 0)),
            pl.BlockSpec((tm, D_MODEL), lambda i, c: (next_rows(i, c), 0)),
            pl.BlockSpec((tm, D_MODEL), lambda i, c: (i, 0)),
            pl.BlockSpec((D_MODEL, fc), lambda i, c: (0, (c + 1) % n_c)),
            pl.BlockSpec((fc, D_MODEL), lambda i, c: (c, 0)),
            pl.BlockSpec((1, D_MODEL), lambda i, c: (0, 0)),
        ],
        out_specs=pl.BlockSpec((tm, D_MODEL), lambda i, c: (i, 0)),
        out_shape=jax.ShapeDtypeStruct((SEQ, D_MODEL), F32),
        scratch_shapes=[pltpu.VMEM((tm, D_MODEL), F32), pltpu.VMEM((tm, fc), BF16)],
        compiler_params=pltpu.CompilerParams(
            dimension_semantics=("arbitrary", "arbitrary"), **params),
        name="mlp",
    )(h0, hm, x1, wu, wd, g)


def kernel(x, norm_mix_g, w_in, b_forget, w_proj_a, w_proj_b, w_out, norm_mlp_g, w_up, w_down,
           norm_final_g):
    row = lambda g: g.reshape(1, D_MODEL).astype(F32)

    x2 = x.reshape(SEQ, D_MODEL)
    z4, f, vt = _inproj(x2, row(norm_mix_g), w_in.T)

    b_pad = jnp.pad(b_forget.astype(F32), (0, F_PAD - N_HEADS)).reshape(1, F_PAD)
    ek, eq = _fox_prep(f, b_pad)

    slopes = jnp.exp2(-8.0 * jnp.arange(1, N_HEADS + 1, dtype=F32) / N_HEADS)
    a, w_up16, w_pa16, w_pb16 = _moba(slopes, z4, vt, (w_up, w_proj_a, w_proj_b))
    b, w_down16, w_out16 = _fox(z4, vt, ek, eq, (w_down, w_out))

    x1, hm = _mix(a, b, z4, x2, w_pa16, w_pb16, w_out16, row(norm_mlp_g))
    out = _mlp(hm, x1, w_up16, w_down16, row(norm_final_g))
    return out.reshape(x.shape)
```

```python
import functools
import math
from typing import Any, NamedTuple

import jax
import jax.numpy as jnp
from jax import lax
from jax.experimental import pallas as pl
from jax.experimental.pallas import tpu as pltpu

D_MODEL = 2048
SEQ = 8192
HEAD_DIM = 128
N_HEADS = 8
WIDTH = N_HEADS * HEAD_DIM
MOBA_BLOCK = 256
MOBA_TOPK = 3
N_BLOCKS = SEQ // MOBA_BLOCK
D_FF = 4 * D_MODEL
RMS_EPS = 1e-6
LOG2E = math.log2(math.e)
SCALE2 = HEAD_DIM ** -0.5 * LOG2E
F_PAD = 128

BF16 = jnp.bfloat16
F32 = jnp.float32
NEG_INF = float("-inf")
POS_INF = float("inf")

VMEM_LIMIT = 56 * 1024 * 1024

IN_TM, IN_TN = 1024, 1024
Q_A_TILE, K_A_TILE, V_A_TILE, Q_B_TILE, K_B_TILE, V_B_TILE = range(6)
N_QKV_TILES = 6 * WIDTH // IN_TN
N_GATE_TILES = 2 * D_MODEL // IN_TN
ATT_T = 2 * MOBA_BLOCK
ATT_G = 4
N_SPLIT = 3
VT_ROWS = HEAD_DIM + 16
N_TILES = SEQ // ATT_T
MIX_TM = 512
MLP_TM, MLP_FC = 512, 1024
ROW_CHUNK = 128
PREP_CHUNK = 256


def _nt_dot(a, b):
    return lax.dot_general(a, b, (((1,), (1,)), ((), ())), preferred_element_type=F32)


def _rms_rows(x, g):
    ms = jnp.mean(x * x, axis=-1, keepdims=True)
    return x * lax.rsqrt(ms + RMS_EPS) * g


def _inproj_kernel(first_rows, x_ref, g_ref, wt_ref, wft_ref, *refs):
    if first_rows:
        z_ref, f_ref, vt_ref, w16_ref, wf16_ref, hn_ref = refs
    else:
        z_ref, f_ref, vt_ref, hn_ref = refs[3:]
    j = pl.program_id(1)

    @pl.when(j == 0)
    def _():
        def body(r, _):
            rows = pl.ds(pl.multiple_of(r * ROW_CHUNK, ROW_CHUNK), ROW_CHUNK)
            hn_ref[rows, :] = _rms_rows(x_ref[rows, :], g_ref[...]).astype(BF16)
            return 0
        lax.fori_loop(0, IN_TM // ROW_CHUNK, body, 0)
        wf = wft_ref[...].astype(BF16)
        if first_rows:
            wf16_ref[...] = wf
        f_ref[...] = _nt_dot(hn_ref[...], wf)

    w = wt_ref[...].astype(BF16)
    if first_rows:
        w16_ref[...] = w
    scale = jnp.where((j == Q_A_TILE) | (j == Q_B_TILE), SCALE2, 1.0)
    zf = _nt_dot(hn_ref[...], w) * scale
    z = zf.astype(BF16)
    for h in range(IN_TN // HEAD_DIM):
        z_ref[h] = z[:, h * HEAD_DIM:(h + 1) * HEAD_DIM]

    @pl.when((j == V_A_TILE) | (j == V_B_TILE))
    def _():
        ones_row = jnp.where(
            lax.broadcasted_iota(jnp.int32, (VT_ROWS - HEAD_DIM, ATT_T), 0) == 0, 1.0, 0.0)
        for h in range(N_HEADS):
            for t in range(IN_TM // ATT_T):
                blk = zf[t * ATT_T:(t + 1) * ATT_T, h * HEAD_DIM:(h + 1) * HEAD_DIM]
                vt_ref[h, t, :HEAD_DIM, :] = blk.T.astype(BF16)
                vt_ref[h, t, HEAD_DIM:, :] = ones_row.astype(BF16)


def _inproj(x2, g, w_in_t):
    n_i = SEQ // IN_TM
    n_j = N_QKV_TILES + N_GATE_TILES
    slabs = IN_TN // HEAD_DIM
    tiles = IN_TM // ATT_T
    f_row = N_QKV_TILES * IN_TN

    def tile_row(i, j):
        return pl.multiple_of(j * IN_TN + jnp.where(j >= N_QKV_TILES, N_HEADS, 0), N_HEADS), 0

    def call(first_rows, row0, n_rows, w_specs, extra_in, extra_out_specs, extra_out_shapes, operands):
        aliases = {} if first_rows else {4: 0, 5: 1, 6: 2}
        return pl.pallas_call(
            functools.partial(_inproj_kernel, first_rows),
            grid=(n_rows, n_j),
            in_specs=[
                pl.BlockSpec((IN_TM, D_MODEL), lambda i, j: (i + row0, 0)),
                pl.BlockSpec((1, D_MODEL), lambda i, j: (0, 0)),
                *w_specs,
                *extra_in,
            ],
            out_specs=[
                pl.BlockSpec((None, slabs, IN_TM, HEAD_DIM), lambda i, j: (j, 0, i + row0, 0)),
                pl.BlockSpec((IN_TM, F_PAD), lambda i, j: (i + row0, 0)),
                pl.BlockSpec((None, N_HEADS, tiles, VT_ROWS, ATT_T),
                             lambda i, j: (jnp.where(j > V_A_TILE, 1, 0), 0, i + row0, 0, 0)),
                *extra_out_specs,
            ],
            out_shape=[
                jax.ShapeDtypeStruct((n_j, slabs, SEQ, HEAD_DIM), BF16),
                jax.ShapeDtypeStruct((SEQ, F_PAD), F32),
                jax.ShapeDtypeStruct((2, N_HEADS, N_TILES, VT_ROWS, ATT_T), BF16),
                *extra_out_shapes,
            ],
            input_output_aliases=aliases,
            scratch_shapes=[pltpu.VMEM((IN_TM, D_MODEL), BF16)],
            compiler_params=pltpu.CompilerParams(
                dimension_semantics=("arbitrary", "arbitrary"), vmem_limit_bytes=VMEM_LIMIT),
            name="inproj_first" if first_rows else "inproj_rest",
        )(x2, g, *operands)

    z4, f, vt, w16, wf16 = call(
        True, 0, 1,
        [pl.BlockSpec((pl.Element(IN_TN), pl.Element(D_MODEL)), tile_row),
         pl.BlockSpec((F_PAD, D_MODEL), lambda i, j: (f_row // F_PAD, 0))],
        [],
        [pl.BlockSpec((None, IN_TN, D_MODEL), lambda i, j: (j, 0, 0)),
         pl.BlockSpec((F_PAD, D_MODEL), lambda i, j: (0, 0))],
        [jax.ShapeDtypeStruct((n_j, IN_TN, D_MODEL), BF16),
         jax.ShapeDtypeStruct((F_PAD, D_MODEL), BF16)],
        (w_in_t, w_in_t))
    hbm = pl.BlockSpec(memory_space=pl.ANY)
    return call(
        False, 1, n_i - 1,
        [pl.BlockSpec((None, IN_TN, D_MODEL), lambda i, j: (j, 0, 0)),
         pl.BlockSpec((F_PAD, D_MODEL), lambda i, j: (0, 0))],
        [hbm, hbm, hbm], [], [],
        (w16, wf16, z4, f, vt))


def _split(v):
    pieces = []
    for _ in range(N_SPLIT):
        p = v.astype(BF16).astype(F32)
        pieces.append(p)
        v = v - p
    return pieces


def _bias_features(v, lane, pieces_at, const_at, const):
    f = jnp.zeros_like(v)
    for n, p in enumerate(_split(v)):
        f = jnp.where(lane == pieces_at + n, p, f)
    f = jnp.where((lane >= const_at) & (lane < const_at + N_SPLIT), const, f)
    return f.astype(BF16)


def _fox_prep_kernel(f_ref, b_ref, ek_ref, eq_ref):
    n = PREP_CHUNK
    tri = (lax.broadcasted_iota(jnp.int32, (n, n), 0)
           >= lax.broadcasted_iota(jnp.int32, (n, n), 1)).astype(BF16)
    lane = lax.broadcasted_iota(jnp.int32, (n, F_PAD), 1)

    def body(ci, carry):
        rows = pl.ds(pl.multiple_of(ci * n, n), n)
        lf = jax.nn.log_sigmoid(f_ref[rows, :] + b_ref[...])
        cs = carry
        for p in _split(lf):
            cs = cs + jnp.dot(tri, p.astype(BF16), preferred_element_type=F32)
        c2 = cs * LOG2E
        for h in range(N_HEADS):
            col = jnp.broadcast_to(c2[:, h:h + 1], (n, F_PAD))
            ek_ref[h, rows, :] = _bias_features(col, lane, 0, N_SPLIT, 1.0)
            eq_ref[h, rows, :] = _bias_features(col, lane, N_SPLIT, 0, -1.0)
        return cs[n - 1:n, :]

    lax.fori_loop(0, SEQ // n, body, jnp.zeros((1, F_PAD), F32))


def _fox_prep(f, b_pad):
    feat = jax.ShapeDtypeStruct((N_HEADS, SEQ, F_PAD), BF16)
    return pl.pallas_call(
        _fox_prep_kernel,
        out_shape=[feat, feat],
        compiler_params=pltpu.CompilerParams(vmem_limit_bytes=VMEM_LIMIT),
        name="fox_prep",
    )(f, b_pad)


def _tile_iotas():
    ko = lax.broadcasted_iota(jnp.int32, (ATT_T, ATT_T), 0)
    qo = lax.broadcasted_iota(jnp.int32, (ATT_T, ATT_T), 1)
    return ko, qo


def _tile_rows(j):
    return pl.ds(pl.multiple_of(j * ATT_T, ATT_T), ATT_T)


def _scores(k_ref, ek, q_ref, eq, g, j):
    keys = jnp.concatenate([k_ref[g, _tile_rows(j), :], ek], axis=1)
    queries = jnp.concatenate([q_ref[g], eq], axis=1)
    return _nt_dot(keys, queries)


class _Stream(NamedTuple):
    s: Any
    p: Any
    alpha: Any
    m: Any
    acc: Any


def _stream_scratch():
    G, T = ATT_G, ATT_T
    row = pltpu.VMEM((G, 1, T), F32)
    return _Stream(s=pltpu.VMEM((G, T, T), F32), p=pltpu.VMEM((G, T, T), BF16),
                   alpha=row, m=row, acc=pltpu.VMEM((G, VT_ROWS, T), F32))


def _fold_first(st, g, s):
    m0 = jnp.max(s, axis=0, keepdims=True)
    st.m[g] = m0
    st.p[g] = jnp.exp2(s - m0).astype(BF16)
    st.alpha[g] = jnp.zeros_like(m0)
    st.acc[g] = jnp.zeros(st.acc.shape[1:], F32)


def _fold(st, g, p, m_new):
    st.alpha[g] = jnp.exp2(st.m[g] - m_new)
    st.p[g] = p.astype(BF16)
    st.m[g] = m_new


def _apply(st, g, vt_tile):
    st.acc[g] = st.alpha[g] * st.acc[g] + jnp.dot(vt_tile, st.p[g], preferred_element_type=F32)


def _stream_past_tiles(st, i, vt_ref, score, fold_past, o_ref):
    chains = range(ATT_G)
    for g in chains:
        st.s[g] = score(g, 0)

    def step(r, score_next):
        prev = jnp.where(r == 1, i, r - 2)
        for g in chains:
            _apply(st, g, vt_ref[g, prev])
        for g in chains:
            fold_past(g, r - 1)
        if score_next:
            for g in chains:
                st.s[g] = score(g, r)

    n_full = jnp.maximum(i - 1, 0)

    def body(u, _):
        step(2 * u + 1, True)
        step(2 * u + 2, True)
        return 0
    lax.fori_loop(0, n_full // 2, body, 0)

    @pl.when(n_full % 2 == 1)
    def _():
        step(n_full, True)
        step(i, False)

    pl.when((n_full % 2 == 0) & (i >= 1))(lambda: step(i, False))

    last = jnp.maximum(i - 1, 0)
    for g in chains:
        _apply(st, g, vt_ref[g, last])
        out = st.acc[g, :HEAD_DIM, :] / st.acc[g, HEAD_DIM:HEAD_DIM + 1, :]
        o_ref[g] = out.T.astype(o_ref.dtype)


def _moba_kernel(n_cast, slope_ref, q_ref, k_ref, vt_ref, *refs):
    w32_refs, (o_ref, *w16_refs), rest = refs[:n_cast], refs[n_cast:2 * n_cast + 1], refs[2 * n_cast + 1:]
    kmean_ref, ek_ref, eq_ref, selb_ref, *stream_refs = rest
    st = _Stream(*stream_refs)
    hg = pl.program_id(0)
    i = pl.program_id(1)
    _cast_slabs(w32_refs, w16_refs)
    ko, qo = _tile_iotas()
    half = MOBA_BLOCK
    slope2 = [slope_ref[hg * ATT_G + g] * LOG2E for g in range(ATT_G)]

    @pl.when(i == 0)
    def _():
        off = lax.broadcasted_iota(jnp.int32, (ATT_T, HEAD_DIM), 0).astype(F32)
        lane = lax.broadcasted_iota(jnp.int32, (ATT_T, HEAD_DIM), 1)
        for g in range(ATT_G):
            def body(b, _):
                rows = pl.ds(pl.multiple_of(b * MOBA_BLOCK, MOBA_BLOCK), MOBA_BLOCK)
                kb = k_ref[g, rows, :].astype(F32)
                kmean_ref[g, pl.ds(b, 1), :] = jnp.mean(kb, axis=0, keepdims=True)
                return 0
            lax.fori_loop(0, N_BLOCKS, body, 0)
            ek_ref[g] = _bias_features(off * slope2[g], lane, 0, N_SPLIT, 1.0)
            eq_ref[g] = _bias_features(off * (-slope2[g]), lane, N_SPLIT, 0, 1.0)

    jj = lax.broadcasted_iota(jnp.int32, (N_BLOCKS, ATT_T), 0)
    jjf = jj.astype(F32)
    lane = lax.broadcasted_iota(jnp.int32, (N_BLOCKS, ATT_T), 1)
    own = 2 * i + jnp.where(lane >= half, 1, 0)
    past = jj < own
    for g in range(ATT_G):
        q = q_ref[g]
        km = kmean_ref[g]
        km_hi = km.astype(BF16)
        km_lo = (km - km_hi.astype(F32)).astype(BF16)
        gate = _nt_dot(km_hi, q) + _nt_dot(km_lo, q)
        gm = jnp.where(past, gate, NEG_INF)
        selb = jnp.full((N_BLOCKS, ATT_T), NEG_INF, F32)
        for _ in range(MOBA_TOPK):
            mx = jnp.max(gm, axis=0, keepdims=True)
            first = jnp.min(jnp.where(gm == mx, jjf, float(N_BLOCKS)), axis=0, keepdims=True)
            pick = (jjf == first) & ((mx > NEG_INF) & (mx < POS_INF))
            selb = jnp.where(pick, 0.0, selb)
            gm = jnp.where(pick, NEG_INF, gm)
        selb_ref[g] = selb

    def score(g, tile):
        return _scores(k_ref, ek_ref[g], q_ref, eq_ref[g], g, tile)

    diag = [score(g, i) for g in range(ATT_G)]
    for g in range(ATT_G):
        sel0 = selb_ref[g, pl.ds(2 * i, 1), :]
        vis0 = jnp.where(qo[:1, :] < half, 0.0, sel0)
        s = diag[g] + jnp.where(ko < half, vis0, 0.0)
        _fold_first(st, g, jnp.where(ko <= qo, s, NEG_INF))

    def fold_past(g, t):
        cj = -slope2[g] * (ATT_T * (i - t)).astype(F32)
        r0 = selb_ref[g, pl.ds(2 * t, 1), :]
        r1 = selb_ref[g, pl.ds(2 * t + 1, 1), :]
        b0, b1 = st.s[g, :half, :], st.s[g, half:, :]
        bm = jnp.maximum(jnp.max(b0, axis=0, keepdims=True) + r0,
                         jnp.max(b1, axis=0, keepdims=True) + r1) + cj
        m_new = jnp.maximum(st.m[g], bm)
        t0 = jnp.where(r0 == 0.0, m_new - cj, POS_INF)
        t1 = jnp.where(r1 == 0.0, m_new - cj, POS_INF)
        p = jnp.concatenate([jnp.exp2(b0 - t0), jnp.exp2(b1 - t1)], axis=0)
        _fold(st, g, p, m_new)

    _stream_past_tiles(st, i, vt_ref, score, fold_past, o_ref)


def _cast_slab_specs(ws):
    steps = (N_HEADS // ATT_G) * N_TILES
    specs = [pl.BlockSpec((w.shape[0] // steps, w.shape[1]), lambda h, i: (h * N_TILES + i, 0))
             for w in ws]
    return specs, [jax.ShapeDtypeStruct(w.shape, BF16) for w in ws]


def _cast_slabs(w32_refs, w16_refs):
    for src, dst in zip(w32_refs, w16_refs):
        dst[...] = src[...].astype(BF16)


def _moba(slopes, z4, vt, ws32):
    G, T = ATT_G, ATT_T
    w_specs, w_shapes = _cast_slab_specs(ws32)
    return pl.pallas_call(
        functools.partial(_moba_kernel, len(ws32)),
        grid=(N_HEADS // G, N_TILES),
        in_specs=[
            pl.BlockSpec(memory_space=pltpu.SMEM),
            pl.BlockSpec((None, G, T, HEAD_DIM), lambda h, i: (Q_A_TILE, h, i, 0)),
            pl.BlockSpec((None, G, SEQ, HEAD_DIM), lambda h, i: (K_A_TILE, h, 0, 0),
                         pipeline_mode=pl.Buffered(1)),
            pl.BlockSpec((None, G, N_TILES, VT_ROWS, T), lambda h, i: (0, h, 0, 0, 0),
                         pipeline_mode=pl.Buffered(1)),
            *w_specs,
        ],
        out_specs=[pl.BlockSpec((G, T, HEAD_DIM), lambda h, i: (h, i, 0)), *w_specs],
        out_shape=[jax.ShapeDtypeStruct((N_HEADS, SEQ, HEAD_DIM), BF16), *w_shapes],
        scratch_shapes=[
            pltpu.VMEM((G, N_BLOCKS, HEAD_DIM), F32),
            pltpu.VMEM((G, T, HEAD_DIM), BF16),
            pltpu.VMEM((G, T, HEAD_DIM), BF16),
            pltpu.VMEM((G, N_BLOCKS, T), F32),
            *_stream_scratch(),
        ],
        compiler_params=pltpu.CompilerParams(
            dimension_semantics=("arbitrary", "arbitrary"), vmem_limit_bytes=VMEM_LIMIT),
        name="moba",
    )(slopes, z4, z4, vt, *ws32)


def _fox_kernel(n_cast, q_ref, eq_ref, k_ref, ek_ref, vt_ref, *refs):
    w32_refs, (o_ref, *w16_refs), stream_refs = (
        refs[:n_cast], refs[n_cast:2 * n_cast + 1], refs[2 * n_cast + 1:])
    st = _Stream(*stream_refs)
    i = pl.program_id(1)
    _cast_slabs(w32_refs, w16_refs)
    ko, qo = _tile_iotas()

    def score(g, tile):
        return _scores(k_ref, ek_ref[g, _tile_rows(tile), :], q_ref, eq_ref[g], g, tile)

    diag = [score(g, i) for g in range(ATT_G)]
    for g in range(ATT_G):
        _fold_first(st, g, jnp.where(ko <= qo, diag[g], NEG_INF))

    def fold_past(g, tile):
        s = st.s[g]
        m_new = jnp.maximum(st.m[g], jnp.max(s, axis=0, keepdims=True))
        _fold(st, g, jnp.exp2(s - m_new), m_new)

    _stream_past_tiles(st, i, vt_ref, score, fold_past, o_ref)


def _fox(z4, vt, ek, eq, ws32):
    G, T = ATT_G, ATT_T
    resident = pl.Buffered(1)
    w_specs, w_shapes = _cast_slab_specs(ws32)
    return pl.pallas_call(
        functools.partial(_fox_kernel, len(ws32)),
        grid=(N_HEADS // G, N_TILES),
        in_specs=[
            pl.BlockSpec((None, G, T, HEAD_DIM), lambda h, i: (Q_B_TILE, h, i, 0)),
            pl.BlockSpec((G, T, F_PAD), lambda h, i: (h, i, 0)),
            pl.BlockSpec((None, G, SEQ, HEAD_DIM), lambda h, i: (K_B_TILE, h, 0, 0),
                         pipeline_mode=resident),
            pl.BlockSpec((G, SEQ, F_PAD), lambda h, i: (h, 0, 0), pipeline_mode=resident),
            pl.BlockSpec((None, G, N_TILES, VT_ROWS, T), lambda h, i: (1, h, 0, 0, 0),
                         pipeline_mode=resident),
            *w_specs,
        ],
        out_specs=[pl.BlockSpec((G, T, HEAD_DIM), lambda h, i: (h, i, 0)), *w_specs],
        out_shape=[jax.ShapeDtypeStruct((N_HEADS, SEQ, HEAD_DIM), BF16), *w_shapes],
        scratch_shapes=list(_stream_scratch()),
        compiler_params=pltpu.CompilerParams(
            dimension_semantics=("arbitrary", "arbitrary"), vmem_limit_bytes=VMEM_LIMIT),
        name="fox",
    )(z4, eq, z4, ek, vt, *ws32)


def _mix_kernel(a_ref, b_ref, ga_ref, gb_ref, x_ref, wpa_ref, wpb_ref, wo_ref, g_ref,
                x1_ref, hm_ref):
    def heads(ref):
        return jnp.concatenate([ref[h] for h in range(N_HEADS)], axis=1)

    def gate(ref):
        cols = [ref[t, h] for t in range(D_MODEL // WIDTH) for h in range(N_HEADS)]
        return jax.nn.sigmoid(jnp.concatenate(cols, axis=1).astype(F32))

    pa = jnp.dot(heads(a_ref), wpa_ref[...], preferred_element_type=F32)
    pb = jnp.dot(heads(b_ref), wpb_ref[...], preferred_element_type=F32)
    merged = gate(ga_ref) * pa + gate(gb_ref) * pb
    x1 = x_ref[...] + jnp.dot(merged.astype(BF16), wo_ref[...], preferred_element_type=F32)
    x1_ref[...] = x1
    hm_ref[...] = _rms_rows(x1, g_ref[...]).astype(BF16)


def _mix(a, b, z4, x2, wpa, wpb, wo, g):
    tm = MIX_TM
    gate_tiles = D_MODEL // IN_TN
    const = lambda i: (0, 0)
    return pl.pallas_call(
        _mix_kernel,
        grid=(SEQ // tm,),
        in_specs=[
            pl.BlockSpec((N_HEADS, tm, HEAD_DIM), lambda i: (0, i, 0)),
            pl.BlockSpec((N_HEADS, tm, HEAD_DIM), lambda i: (0, i, 0)),
            pl.BlockSpec((gate_tiles, N_HEADS, tm, HEAD_DIM),
                         lambda i: (N_QKV_TILES // gate_tiles, 0, i, 0)),
            pl.BlockSpec((gate_tiles, N_HEADS, tm, HEAD_DIM),
                         lambda i: (N_QKV_TILES // gate_tiles + 1, 0, i, 0)),
            pl.BlockSpec((tm, D_MODEL), lambda i: (i, 0)),
            pl.BlockSpec((WIDTH, D_MODEL), const),
            pl.BlockSpec((WIDTH, D_MODEL), const),
            pl.BlockSpec((D_MODEL, D_MODEL), const),
            pl.BlockSpec((1, D_MODEL), const),
        ],
        out_specs=[
            pl.BlockSpec((tm, D_MODEL), lambda i: (i, 0)),
            pl.BlockSpec((tm, D_MODEL), lambda i: (i, 0)),
        ],
        out_shape=[
            jax.ShapeDtypeStruct((SEQ, D_MODEL), F32),
            jax.ShapeDtypeStruct((SEQ, D_MODEL), BF16),
        ],
        compiler_params=pltpu.CompilerParams(
            dimension_semantics=("arbitrary",), vmem_limit_bytes=VMEM_LIMIT),
        name="mix",
    )(a, b, z4, z4, x2, wpa, wpb, wo, g)


def _mlp_up(hm, wu):
    h = jnp.dot(hm, wu, preferred_element_type=F32)
    return jnp.square(jnp.maximum(h, 0.0)).astype(BF16)


def _mlp_head_kernel(hm_ref, wu_ref, h_ref):
    h_ref[...] = _mlp_up(hm_ref[...], wu_ref[...])


def _mlp_kernel(h0_ref, hm_next_ref, x1_ref, wu_next_ref, wd_ref, g_ref, o_ref, acc_ref, h_ref):
    i, c = pl.program_id(0), pl.program_id(1)

    @pl.when((i == 0) & (c == 0))
    def _():
        h_ref[...] = h0_ref[...]

    @pl.when(c == 0)
    def _():
        acc_ref[...] = x1_ref[...]

    acc_ref[...] += jnp.dot(h_ref[...], wd_ref[...], preferred_element_type=F32)
    h_ref[...] = _mlp_up(hm_next_ref[...], wu_next_ref[...])

    @pl.when(c == pl.num_programs(1) - 1)
    def _():
        o_ref[...] = _rms_rows(acc_ref[...], g_ref[...])


def _mlp(hm, x1, wu, wd, g):
    tm, fc = MLP_TM, MLP_FC
    n_i, n_c = SEQ // tm, D_FF // fc
    params = dict(vmem_limit_bytes=VMEM_LIMIT)
    h0 = pl.pallas_call(
        _mlp_head_kernel,
        grid=(1,),
        in_specs=[pl.BlockSpec((tm, D_MODEL), lambda s: (0, 0)),
                  pl.BlockSpec((D_MODEL, fc), lambda s: (0, 0))],
        out_specs=pl.BlockSpec((tm, fc), lambda s: (0, 0)),
        out_shape=jax.ShapeDtypeStruct((tm, fc), BF16),
        compiler_params=pltpu.CompilerParams(dimension_semantics=("arbitrary",), **params),
        name="mlp_head",
    )(hm, wu)

    def next_rows(i, c):
        return jnp.minimum(i + (c + 1) // n_c, n_i - 1)

    return pl.pallas_call(
        _mlp_kernel,
        grid=(n_i, n_c),
        in_specs=[
            pl.BlockSpec((tm, fc), lambda i, c: (0, 0)),
            pl.BlockSpec((tm, D_MODEL), lambda i, c: (next_rows(i, c), 0)),
            pl.BlockSpec((tm, D_MODEL), lambda i, c: (i, 0)),
            pl.BlockSpec((D_MODEL, fc), lambda i, c: (0, (c + 1) % n_c)),
            pl.BlockSpec((fc, D_MODEL), lambda i, c: (c, 0)),
            pl.BlockSpec((1, D_MODEL), lambda i, c: (0, 0)),
        ],
        out_specs=pl.BlockSpec((tm, D_MODEL), lambda i, c: (i, 0)),
        out_shape=jax.ShapeDtypeStruct((SEQ, D_MODEL), F32),
        scratch_shapes=[pltpu.VMEM((tm, D_MODEL), F32), pltpu.VMEM((tm, fc), BF16)],
        compiler_params=pltpu.CompilerParams(
            dimension_semantics=("arbitrary", "arbitrary"), **params),
        name="mlp",
    )(h0, hm, x1, wu, wd, g)


def kernel(x, norm_mix_g, w_in, b_forget, w_proj_a, w_proj_b, w_out, norm_mlp_g, w_up, w_down,
           norm_final_g):
    row = lambda g: g.reshape(1, D_MODEL).astype(F32)

    x2 = x.reshape(SEQ, D_MODEL)
    z4, f, vt = _inproj(x2, row(norm_mix_g), w_in.T)

    b_pad = jnp.pad(b_forget.astype(F32), (0, F_PAD - N_HEADS)).reshape(1, F_PAD)
    ek, eq = _fox_prep(f, b_pad)

    slopes = jnp.exp2(-8.0 * jnp.arange(1, N_HEADS + 1, dtype=F32) / N_HEADS)
    a, w_up16, w_pa16, w_pb16 = _moba(slopes, z4, vt, (w_up, w_proj_a, w_proj_b))
    b, w_down16, w_out16 = _fox(z4, vt, ek, eq, (w_down, w_out))

    x1, hm = _mix(a, b, z4, x2, w_pa16, w_pb16, w_out16, row(norm_mlp_g))
    out = _mlp(hm, x1, w_up16, w_down16, row(norm_final_g))
    return out.reshape(x.shape)
```

```python
import functools
import math
from typing import Any, NamedTuple

import jax
import jax.numpy as jnp
from jax import lax
from jax.experimental import pallas as pl
from jax.experimental.pallas import tpu as pltpu

D_MODEL = 2048
SEQ = 8192
HEAD_DIM = 128
N_HEADS = 8
WIDTH = N_HEADS * HEAD_DIM
MOBA_BLOCK = 256
MOBA_TOPK = 3
N_BLOCKS = SEQ // MOBA_BLOCK
D_FF = 4 * D_MODEL
RMS_EPS = 1e-6
LOG2E = math.log2(math.e)
SCALE2 = HEAD_DIM ** -0.5 * LOG2E
F_PAD = 128

BF16 = jnp.bfloat16
F32 = jnp.float32
NEG_INF = float("-inf")
POS_INF = float("inf")

VMEM_LIMIT = 56 * 1024 * 1024

IN_TM, IN_TN = 1024, 1024
Q_A_TILE, K_A_TILE, V_A_TILE, Q_B_TILE, K_B_TILE, V_B_TILE = range(6)
N_QKV_TILES = 6 * WIDTH // IN_TN
N_GATE_TILES = 2 * D_MODEL // IN_TN
ATT_T = 2 * MOBA_BLOCK
ATT_G = 4
N_SPLIT = 3
VT_ROWS = HEAD_DIM + 16
N_TILES = SEQ // ATT_T
MIX_TM = 512
MLP_TM, MLP_FC = 512, 1024
ROW_CHUNK = 128
PREP_CHUNK = 256


def _nt_dot(a, b):
    return lax.dot_general(a, b, (((1,), (1,)), ((), ())), preferred_element_type=F32)


def _rms_rows(x, g):
    ms = jnp.mean(x * x, axis=-1, keepdims=True)
    return x * lax.rsqrt(ms + RMS_EPS) * g


def _inproj_kernel(x_ref, g_ref, wt_ref, wft_ref, z_ref, f_ref, vt_ref, hn_ref):
    j = pl.program_id(1)

    @pl.when(j == 0)
    def _():
        def body(r, _):
            rows = pl.ds(pl.multiple_of(r * ROW_CHUNK, ROW_CHUNK), ROW_CHUNK)
            hn_ref[rows, :] = _rms_rows(x_ref[rows, :], g_ref[...]).astype(BF16)
            return 0
        lax.fori_loop(0, IN_TM // ROW_CHUNK, body, 0)
        f_ref[...] = _nt_dot(hn_ref[...], wft_ref[...].astype(BF16))

    scale = jnp.where((j == Q_A_TILE) | (j == Q_B_TILE), SCALE2, 1.0)
    zf = _nt_dot(hn_ref[...], wt_ref[...].astype(BF16)) * scale
    z = zf.astype(BF16)
    for h in range(IN_TN // HEAD_DIM):
        z_ref[h] = z[:, h * HEAD_DIM:(h + 1) * HEAD_DIM]

    @pl.when((j == V_A_TILE) | (j == V_B_TILE))
    def _():
        ones_row = jnp.where(
            lax.broadcasted_iota(jnp.int32, (VT_ROWS - HEAD_DIM, ATT_T), 0) == 0, 1.0, 0.0)
        for h in range(N_HEADS):
            for t in range(IN_TM // ATT_T):
                blk = zf[t * ATT_T:(t + 1) * ATT_T, h * HEAD_DIM:(h + 1) * HEAD_DIM]
                vt_ref[h, t, :HEAD_DIM, :] = blk.T.astype(BF16)
                vt_ref[h, t, HEAD_DIM:, :] = ones_row.astype(BF16)


def _inproj(x2, g, w_in_t):
    n_i = SEQ // IN_TM
    n_j = N_QKV_TILES + N_GATE_TILES
    slabs = IN_TN // HEAD_DIM
    tiles = IN_TM // ATT_T
    f_row = N_QKV_TILES * IN_TN

    def tile_row(i, j):
        return pl.multiple_of(j * IN_TN + jnp.where(j >= N_QKV_TILES, N_HEADS, 0), N_HEADS), 0

    return pl.pallas_call(
        _inproj_kernel,
        grid=(n_i, n_j),
        in_specs=[
            pl.BlockSpec((IN_TM, D_MODEL), lambda i, j: (i, 0)),
            pl.BlockSpec((1, D_MODEL), lambda i, j: (0, 0)),
            pl.BlockSpec((pl.Element(IN_TN), pl.Element(D_MODEL)), tile_row),
            pl.BlockSpec((F_PAD, D_MODEL), lambda i, j: (f_row // F_PAD, 0)),
        ],
        out_specs=[
            pl.BlockSpec((None, slabs, IN_TM, HEAD_DIM), lambda i, j: (j, 0, i, 0)),
            pl.BlockSpec((IN_TM, F_PAD), lambda i, j: (i, 0)),
            pl.BlockSpec((None, N_HEADS, tiles, VT_ROWS, ATT_T),
                         lambda i, j: (jnp.where(j > V_A_TILE, 1, 0), 0, i, 0, 0)),
        ],
        out_shape=[
            jax.ShapeDtypeStruct((n_j, slabs, SEQ, HEAD_DIM), BF16),
            jax.ShapeDtypeStruct((SEQ, F_PAD), F32),
            jax.ShapeDtypeStruct((2, N_HEADS, N_TILES, VT_ROWS, ATT_T), BF16),
        ],
        scratch_shapes=[pltpu.VMEM((IN_TM, D_MODEL), BF16)],
        compiler_params=pltpu.CompilerParams(
            dimension_semantics=("arbitrary", "arbitrary"), vmem_limit_bytes=VMEM_LIMIT),
        name="inproj",
    )(x2, g, w_in_t, w_in_t)


def _split(v):
    pieces = []
    for _ in range(N_SPLIT):
        p = v.astype(BF16).astype(F32)
        pieces.append(p)
        v = v - p
    return pieces


def _bias_features(v, lane, pieces_at, const_at, const):
    f = jnp.zeros_like(v)
    for n, p in enumerate(_split(v)):
        f = jnp.where(lane == pieces_at + n, p, f)
    f = jnp.where((lane >= const_at) & (lane < const_at + N_SPLIT), const, f)
    return f.astype(BF16)


def _fox_prep_kernel(f_ref, b_ref, ek_ref, eq_ref):
    n = PREP_CHUNK
    tri = (lax.broadcasted_iota(jnp.int32, (n, n), 0)
           >= lax.broadcasted_iota(jnp.int32, (n, n), 1)).astype(BF16)
    lane = lax.broadcasted_iota(jnp.int32, (n, F_PAD), 1)

    def body(ci, carry):
        rows = pl.ds(pl.multiple_of(ci * n, n), n)
        lf = jax.nn.log_sigmoid(f_ref[rows, :] + b_ref[...])
        cs = carry
        for p in _split(lf):
            cs = cs + jnp.dot(tri, p.astype(BF16), preferred_element_type=F32)
        c2 = cs * LOG2E
        for h in range(N_HEADS):
            col = jnp.broadcast_to(c2[:, h:h + 1], (n, F_PAD))
            ek_ref[h, rows, :] = _bias_features(col, lane, 0, N_SPLIT, 1.0)
            eq_ref[h, rows, :] = _bias_features(col, lane, N_SPLIT, 0, -1.0)
        return cs[n - 1:n, :]

    lax.fori_loop(0, SEQ // n, body, jnp.zeros((1, F_PAD), F32))


def _fox_prep(f, b_pad):
    feat = jax.ShapeDtypeStruct((N_HEADS, SEQ, F_PAD), BF16)
    return pl.pallas_call(
        _fox_prep_kernel,
        out_shape=[feat, feat],
        compiler_params=pltpu.CompilerParams(vmem_limit_bytes=VMEM_LIMIT),
        name="fox_prep",
    )(f, b_pad)


def _tile_iotas():
    ko = lax.broadcasted_iota(jnp.int32, (ATT_T, ATT_T), 0)
    qo = lax.broadcasted_iota(jnp.int32, (ATT_T, ATT_T), 1)
    return ko, qo


def _tile_rows(j):
    return pl.ds(pl.multiple_of(j * ATT_T, ATT_T), ATT_T)


def _scores(k_ref, ek, q_ref, eq, g, j):
    keys = jnp.concatenate([k_ref[g, _tile_rows(j), :], ek], axis=1)
    queries = jnp.concatenate([q_ref[g], eq], axis=1)
    return _nt_dot(keys, queries)


class _Stream(NamedTuple):
    s: Any
    p: Any
    alpha: Any
    m: Any
    acc: Any


def _stream_scratch():
    G, T = ATT_G, ATT_T
    row = pltpu.VMEM((G, 1, T), F32)
    return _Stream(s=pltpu.VMEM((G, T, T), F32), p=pltpu.VMEM((G, T, T), BF16),
                   alpha=row, m=row, acc=pltpu.VMEM((G, VT_ROWS, T), F32))


def _fold_first(st, g, s):
    m0 = jnp.max(s, axis=0, keepdims=True)
    st.m[g] = m0
    st.p[g] = jnp.exp2(s - m0).astype(BF16)
    st.alpha[g] = jnp.zeros_like(m0)
    st.acc[g] = jnp.zeros(st.acc.shape[1:], F32)


def _fold(st, g, p, m_new):
    st.alpha[g] = jnp.exp2(st.m[g] - m_new)
    st.p[g] = p.astype(BF16)
    st.m[g] = m_new


def _apply(st, g, vt_tile):
    st.acc[g] = st.alpha[g] * st.acc[g] + jnp.dot(vt_tile, st.p[g], preferred_element_type=F32)


def _stream_past_tiles(st, i, vt_ref, score, fold_past, o_ref):
    chains = range(ATT_G)
    for g in chains:
        st.s[g] = score(g, 0)

    def step(r, score_next):
        prev = jnp.where(r == 1, i, r - 2)
        for g in chains:
            _apply(st, g, vt_ref[g, prev])
        for g in chains:
            fold_past(g, r - 1)
        if score_next:
            for g in chains:
                st.s[g] = score(g, r)

    n_full = jnp.maximum(i - 1, 0)

    def body(u, _):
        step(2 * u + 1, True)
        step(2 * u + 2, True)
        return 0
    lax.fori_loop(0, n_full // 2, body, 0)

    @pl.when(n_full % 2 == 1)
    def _():
        step(n_full, True)
        step(i, False)

    pl.when((n_full % 2 == 0) & (i >= 1))(lambda: step(i, False))

    last = jnp.maximum(i - 1, 0)
    for g in chains:
        _apply(st, g, vt_ref[g, last])
        out = st.acc[g, :HEAD_DIM, :] / st.acc[g, HEAD_DIM:HEAD_DIM + 1, :]
        o_ref[g] = out.T.astype(o_ref.dtype)


def _moba_kernel(n_cast, slope_ref, q_ref, k_ref, vt_ref, *refs):
    w32_refs, (o_ref, *w16_refs), rest = refs[:n_cast], refs[n_cast:2 * n_cast + 1], refs[2 * n_cast + 1:]
    kmean_ref, ek_ref, eq_ref, selb_ref, *stream_refs = rest
    st = _Stream(*stream_refs)
    hg = pl.program_id(0)
    i = pl.program_id(1)
    _cast_slabs(w32_refs, w16_refs)
    ko, qo = _tile_iotas()
    half = MOBA_BLOCK
    slope2 = [slope_ref[hg * ATT_G + g] * LOG2E for g in range(ATT_G)]

    @pl.when(i == 0)
    def _():
        off = lax.broadcasted_iota(jnp.int32, (ATT_T, HEAD_DIM), 0).astype(F32)
        lane = lax.broadcasted_iota(jnp.int32, (ATT_T, HEAD_DIM), 1)
        for g in range(ATT_G):
            def body(b, _):
                rows = pl.ds(pl.multiple_of(b * MOBA_BLOCK, MOBA_BLOCK), MOBA_BLOCK)
                kb = k_ref[g, rows, :].astype(F32)
                kmean_ref[g, pl.ds(b, 1), :] = jnp.mean(kb, axis=0, keepdims=True)
                return 0
            lax.fori_loop(0, N_BLOCKS, body, 0)
            ek_ref[g] = _bias_features(off * slope2[g], lane, 0, N_SPLIT, 1.0)
            eq_ref[g] = _bias_features(off * (-slope2[g]), lane, N_SPLIT, 0, 1.0)

    jj = lax.broadcasted_iota(jnp.int32, (N_BLOCKS, ATT_T), 0)
    jjf = jj.astype(F32)
    lane = lax.broadcasted_iota(jnp.int32, (N_BLOCKS, ATT_T), 1)
    own = 2 * i + jnp.where(lane >= half, 1, 0)
    past = jj < own
    for g in range(ATT_G):
        q = q_ref[g]
        km = kmean_ref[g]
        km_hi = km.astype(BF16)
        km_lo = (km - km_hi.astype(F32)).astype(BF16)
        gate = _nt_dot(km_hi, q) + _nt_dot(km_lo, q)
        gm = jnp.where(past, gate, NEG_INF)
        selb = jnp.full((N_BLOCKS, ATT_T), NEG_INF, F32)
        for _ in range(MOBA_TOPK):
            mx = jnp.max(gm, axis=0, keepdims=True)
            first = jnp.min(jnp.where(gm == mx, jjf, float(N_BLOCKS)), axis=0, keepdims=True)
            pick = (jjf == first) & ((mx > NEG_INF) & (mx < POS_INF))
            selb = jnp.where(pick, 0.0, selb)
            gm = jnp.where(pick, NEG_INF, gm)
        selb_ref[g] = selb

    def score(g, tile):
        return _scores(k_ref, ek_ref[g], q_ref, eq_ref[g], g, tile)

    diag = [score(g, i) for g in range(ATT_G)]
    for g in range(ATT_G):
        sel0 = selb_ref[g, pl.ds(2 * i, 1), :]
        vis0 = jnp.where(qo[:1, :] < half, 0.0, sel0)
        s = diag[g] + jnp.where(ko < half, vis0, 0.0)
        _fold_first(st, g, jnp.where(ko <= qo, s, NEG_INF))

    def fold_past(g, t):
        cj = -slope2[g] * (ATT_T * (i - t)).astype(F32)
        r0 = selb_ref[g, pl.ds(2 * t, 1), :]
        r1 = selb_ref[g, pl.ds(2 * t + 1, 1), :]
        b0, b1 = st.s[g, :half, :], st.s[g, half:, :]
        bm = jnp.maximum(jnp.max(b0, axis=0, keepdims=True) + r0,
                         jnp.max(b1, axis=0, keepdims=True) + r1) + cj
        m_new = jnp.maximum(st.m[g], bm)
        t0 = jnp.where(r0 == 0.0, m_new - cj, POS_INF)
        t1 = jnp.where(r1 == 0.0, m_new - cj, POS_INF)
        p = jnp.concatenate([jnp.exp2(b0 - t0), jnp.exp2(b1 - t1)], axis=0)
        _fold(st, g, p, m_new)

    _stream_past_tiles(st, i, vt_ref, score, fold_past, o_ref)


def _cast_slab_specs(ws):
    steps = (N_HEADS // ATT_G) * N_TILES
    specs = [pl.BlockSpec((w.shape[0] // steps, w.shape[1]), lambda h, i: (h * N_TILES + i, 0))
             for w in ws]
    return specs, [jax.ShapeDtypeStruct(w.shape, BF16) for w in ws]


def _cast_slabs(w32_refs, w16_refs):
    for src, dst in zip(w32_refs, w16_refs):
        dst[...] = src[...].astype(BF16)


def _moba(slopes, z4, vt, ws32):
    G, T = ATT_G, ATT_T
    w_specs, w_shapes = _cast_slab_specs(ws32)
    return pl.pallas_call(
        functools.partial(_moba_kernel, len(ws32)),
        grid=(N_HEADS // G, N_TILES),
        in_specs=[
            pl.BlockSpec(memory_space=pltpu.SMEM),
            pl.BlockSpec((None, G, T, HEAD_DIM), lambda h, i: (Q_A_TILE, h, i, 0)),
            pl.BlockSpec((None, G, SEQ, HEAD_DIM), lambda h, i: (K_A_TILE, h, 0, 0),
                         pipeline_mode=pl.Buffered(1)),
            pl.BlockSpec((None, G, N_TILES, VT_ROWS, T), lambda h, i: (0, h, 0, 0, 0),
                         pipeline_mode=pl.Buffered(1)),
            *w_specs,
        ],
        out_specs=[pl.BlockSpec((G, T, HEAD_DIM), lambda h, i: (h, i, 0)), *w_specs],
        out_shape=[jax.ShapeDtypeStruct((N_HEADS, SEQ, HEAD_DIM), BF16), *w_shapes],
        scratch_shapes=[
            pltpu.VMEM((G, N_BLOCKS, HEAD_DIM), F32),
            pltpu.VMEM((G, T, HEAD_DIM), BF16),
            pltpu.VMEM((G, T, HEAD_DIM), BF16),
            pltpu.VMEM((G, N_BLOCKS, T), F32),
            *_stream_scratch(),
        ],
        compiler_params=pltpu.CompilerParams(
            dimension_semantics=("arbitrary", "arbitrary"), vmem_limit_bytes=VMEM_LIMIT),
        name="moba",
    )(slopes, z4, z4, vt, *ws32)


def _fox_kernel(n_cast, q_ref, eq_ref, k_ref, ek_ref, vt_ref, *refs):
    w32_refs, (o_ref, *w16_refs), stream_refs = (
        refs[:n_cast], refs[n_cast:2 * n_cast + 1], refs[2 * n_cast + 1:])
    st = _Stream(*stream_refs)
    i = pl.program_id(1)
    _cast_slabs(w32_refs, w16_refs)
    ko, qo = _tile_iotas()

    def score(g, tile):
        return _scores(k_ref, ek_ref[g, _tile_rows(tile), :], q_ref, eq_ref[g], g, tile)

    diag = [score(g, i) for g in range(ATT_G)]
    for g in range(ATT_G):
        _fold_first(st, g, jnp.where(ko <= qo, diag[g], NEG_INF))

    def fold_past(g, tile):
        s = st.s[g]
        m_new = jnp.maximum(st.m[g], jnp.max(s, axis=0, keepdims=True))
        _fold(st, g, jnp.exp2(s - m_new), m_new)

    _stream_past_tiles(st, i, vt_ref, score, fold_past, o_ref)


def _fox(z4, vt, ek, eq, ws32):
    G, T = ATT_G, ATT_T
    resident = pl.Buffered(1)
    w_specs, w_shapes = _cast_slab_specs(ws32)
    return pl.pallas_call(
        functools.partial(_fox_kernel, len(ws32)),
        grid=(N_HEADS // G, N_TILES),
        in_specs=[
            pl.BlockSpec((None, G, T, HEAD_DIM), lambda h, i: (Q_B_TILE, h, i, 0)),
            pl.BlockSpec((G, T, F_PAD), lambda h, i: (h, i, 0)),
            pl.BlockSpec((None, G, SEQ, HEAD_DIM), lambda h, i: (K_B_TILE, h, 0, 0),
                         pipeline_mode=resident),
            pl.BlockSpec((G, SEQ, F_PAD), lambda h, i: (h, 0, 0), pipeline_mode=resident),
            pl.BlockSpec((None, G, N_TILES, VT_ROWS, T), lambda h, i: (1, h, 0, 0, 0),
                         pipeline_mode=resident),
            *w_specs,
        ],
        out_specs=[pl.BlockSpec((G, T, HEAD_DIM), lambda h, i: (h, i, 0)), *w_specs],
        out_shape=[jax.ShapeDtypeStruct((N_HEADS, SEQ, HEAD_DIM), BF16), *w_shapes],
        scratch_shapes=list(_stream_scratch()),
        compiler_params=pltpu.CompilerParams(
            dimension_semantics=("arbitrary", "arbitrary"), vmem_limit_bytes=VMEM_LIMIT),
        name="fox",
    )(z4, eq, z4, ek, vt, *ws32)


def _mix_kernel(a_ref, b_ref, ga_ref, gb_ref, x_ref, wpa_ref, wpb_ref, wo_ref, g_ref,
                x1_ref, hm_ref):
    def heads(ref):
        return jnp.concatenate([ref[h] for h in range(N_HEADS)], axis=1)

    def gate(ref):
        cols = [ref[t, h] for t in range(D_MODEL // WIDTH) for h in range(N_HEADS)]
        return jax.nn.sigmoid(jnp.concatenate(cols, axis=1).astype(F32))

    pa = jnp.dot(heads(a_ref), wpa_ref[...], preferred_element_type=F32)
    pb = jnp.dot(heads(b_ref), wpb_ref[...], preferred_element_type=F32)
    merged = gate(ga_ref) * pa + gate(gb_ref) * pb
    x1 = x_ref[...] + jnp.dot(merged.astype(BF16), wo_ref[...], preferred_element_type=F32)
    x1_ref[...] = x1
    hm_ref[...] = _rms_rows(x1, g_ref[...]).astype(BF16)


def _mix(a, b, z4, x2, wpa, wpb, wo, g):
    tm = MIX_TM
    gate_tiles = D_MODEL // IN_TN
    const = lambda i: (0, 0)
    return pl.pallas_call(
        _mix_kernel,
        grid=(SEQ // tm,),
        in_specs=[
            pl.BlockSpec((N_HEADS, tm, HEAD_DIM), lambda i: (0, i, 0)),
            pl.BlockSpec((N_HEADS, tm, HEAD_DIM), lambda i: (0, i, 0)),
            pl.BlockSpec((gate_tiles, N_HEADS, tm, HEAD_DIM),
                         lambda i: (N_QKV_TILES // gate_tiles, 0, i, 0)),
            pl.BlockSpec((gate_tiles, N_HEADS, tm, HEAD_DIM),
                         lambda i: (N_QKV_TILES // gate_tiles + 1, 0, i, 0)),
            pl.BlockSpec((tm, D_MODEL), lambda i: (i, 0)),
            pl.BlockSpec((WIDTH, D_MODEL), const),
            pl.BlockSpec((WIDTH, D_MODEL), const),
            pl.BlockSpec((D_MODEL, D_MODEL), const),
            pl.BlockSpec((1, D_MODEL), const),
        ],
        out_specs=[
            pl.BlockSpec((tm, D_MODEL), lambda i: (i, 0)),
            pl.BlockSpec((tm, D_MODEL), lambda i: (i, 0)),
        ],
        out_shape=[
            jax.ShapeDtypeStruct((SEQ, D_MODEL), F32),
            jax.ShapeDtypeStruct((SEQ, D_MODEL), BF16),
        ],
        compiler_params=pltpu.CompilerParams(
            dimension_semantics=("arbitrary",), vmem_limit_bytes=VMEM_LIMIT),
        name="mix",
    )(a, b, z4, z4, x2, wpa, wpb, wo, g)


def _mlp_up(hm, wu):
    h = jnp.dot(hm, wu, preferred_element_type=F32)
    return jnp.square(jnp.maximum(h, 0.0)).astype(BF16)


def _mlp_kernel(wu_first_ref, hm_next_ref, x1_ref, wu_next_ref, wd_ref, g_ref, o_ref,
                acc_ref, h_ref):
    i, c = pl.program_id(0), pl.program_id(1)

    @pl.when((i == 0) & (c == 0))
    def _():
        h_ref[...] = _mlp_up(hm_next_ref[...], wu_first_ref[...])

    @pl.when(c == 0)
    def _():
        acc_ref[...] = x1_ref[...]

    acc_ref[...] += jnp.dot(h_ref[...], wd_ref[...], preferred_element_type=F32)
    h_ref[...] = _mlp_up(hm_next_ref[...], wu_next_ref[...])

    @pl.when(c == pl.num_programs(1) - 1)
    def _():
        o_ref[...] = _rms_rows(acc_ref[...], g_ref[...])


def _mlp(hm, x1, wu, wd, g):
    tm, fc = MLP_TM, MLP_FC
    n_i, n_c = SEQ // tm, D_FF // fc
    params = dict(vmem_limit_bytes=VMEM_LIMIT)

    def next_rows(i, c):
        return jnp.minimum(i + (c + 1) // n_c, n_i - 1)

    return pl.pallas_call(
        _mlp_kernel,
        grid=(n_i, n_c),
        in_specs=[
            pl.BlockSpec((D_MODEL, fc), lambda i, c: (0, 0)),
            pl.BlockSpec((tm, D_MODEL), lambda i, c: (next_rows(i, c), 0)),
            pl.BlockSpec((tm, D_MODEL), lambda i, c: (i, 0)),
            pl.BlockSpec((D_MODEL, fc), lambda i, c: (0, (c + 1) % n_c)),
            pl.BlockSpec((fc, D_MODEL), lambda i, c: (c, 0)),
            pl.BlockSpec((1, D_MODEL), lambda i, c: (0, 0)),
        ],
        out_specs=pl.BlockSpec((tm, D_MODEL), lambda i, c: (i, 0)),
        out_shape=jax.ShapeDtypeStruct((SEQ, D_MODEL), F32),
        scratch_shapes=[pltpu.VMEM((tm, D_MODEL), F32), pltpu.VMEM((tm, fc), BF16)],
        compiler_params=pltpu.CompilerParams(
            dimension_semantics=("arbitrary", "arbitrary"), **params),
        name="mlp",
    )(wu, hm, x1, wu, wd, g)


def kernel(x, norm_mix_g, w_in, b_forget, w_proj_a, w_proj_b, w_out, norm_mlp_g, w_up, w_down,
           norm_final_g):
    row = lambda g: g.reshape(1, D_MODEL).astype(F32)

    x2 = x.reshape(SEQ, D_MODEL)
    z4, f, vt = _inproj(x2, row(norm_mix_g), w_in.T)

    b_pad = jnp.pad(b_forget.astype(F32), (0, F_PAD - N_HEADS)).reshape(1, F_PAD)
    ek, eq = _fox_prep(f, b_pad)

    slopes = jnp.exp2(-8.0 * jnp.arange(1, N_HEADS + 1, dtype=F32) / N_HEADS)
    a, w_up16, w_pa16, w_pb16 = _moba(slopes, z4, vt, (w_up, w_proj_a, w_proj_b))
    b, w_down16, w_out16 = _fox(z4, vt, ek, eq, (w_down, w_out))

    x1, hm = _mix(a, b, z4, x2, w_pa16, w_pb16, w_out16, row(norm_mlp_g))
    out = _mlp(hm, x1, w_up16, w_down16, row(norm_final_g))
    return out.reshape(x.shape)
```
